```python
import jax, jax.numpy as jnp
from jax import lax
import numpy as np

D_MODEL = 1024
BATCH = 8
SEQ = 4096
DEPTH = 2

PLE_DIM = 256
MIX_W = 512
N_BRANCH = 4
POOL_WINDOWS = (2, 4, 8, 16)
POOL_GROUP = MIX_W // len(POOL_WINDOWS)
RWKV_HEAD = 64
RWKV_HEADS = MIX_W // RWKV_HEAD
RWKV_DECAY_RANK = 64
RWKV_A_RANK = 64
RWKV_G_RANK = 128
RWKV_LN_EPS = 64e-5
RWKV_IN = 3 * MIX_W + RWKV_DECAY_RANK + RWKV_A_RANK + RWKV_G_RANK
RWKV_SPLITS = [MIX_W, 2 * MIX_W, 3 * MIX_W, 3 * MIX_W + RWKV_DECAY_RANK,
               3 * MIX_W + RWKV_DECAY_RANK + RWKV_A_RANK]
SG_CHUNK = 128
SG_GROUPS = 4
SG_GROUP_W = MIX_W // SG_GROUPS
SG_LN_EPS = 1e-5
HGRN_EXPAND = 128
HGRN_HEADS = MIX_W // HGRN_EXPAND
HGRN_CHUNK = 64
GATE_FLOOR = 1e-30
D_FF = 2816
CONV_W = 3
NORM_EPS = 1e-6
COL_A = MIX_W
COL_B = RWKV_IN
COL_C = 2 * MIX_W
COL_D = 3 * MIX_W
COL_G = N_BRANCH * D_MODEL
D_IN = COL_A + COL_B + COL_C + COL_D + COL_G
IN_SPLITS = [COL_A, COL_A + COL_B, COL_A + COL_B + COL_C, COL_A + COL_B + COL_C + COL_D]

kernel_name = "hybrid_pool_rwkv7_gmlp_hgrn2_block"


def rmsnorm(x, g, eps=NORM_EPS):
    x32 = x.astype(jnp.float32)
    y = x32 * lax.rsqrt(jnp.mean(x32 * x32, axis=-1, keepdims=True) + eps)
    return (y * g.astype(jnp.float32)).astype(x.dtype)


def shift_right(x, n=1):
    return jnp.pad(x, ((0, 0), (n, 0), (0, 0)))[:, :x.shape[1]]


def pool_mixer(a, w_group, scale):
    B, S, _ = a.shape
    a32 = a.astype(jnp.float32)
    csum = jnp.cumsum(a32, axis=1)
    count = jnp.arange(1, S + 1, dtype=jnp.float32)[None, :, None]
    diffs = []
    for gi, win in enumerate(POOL_WINDOWS):
        c = csum[..., gi * POOL_GROUP:(gi + 1) * POOL_GROUP]
        mean = (c - shift_right(c, win)) / jnp.minimum(count, win)
        diffs.append(mean - a32[..., gi * POOL_GROUP:(gi + 1) * POOL_GROUP])
    d = jnp.stack(diffs, axis=2).astype(a.dtype)
    y = jnp.einsum('bsgc,gcd->bsgd', d, w_group).reshape(B, S, MIX_W)
    return (y * scale).astype(a.dtype)


def rwkv7_mixer(xb, mu, w0, w2, a0, a2, g2, k_k, k_a, r_k, ln_w, ln_b):
    B, S, _ = xb.shape
    f32 = jnp.float32
    xb = xb + mu * (shift_right(xb) - xb)
    r, k, v, lw, la, lg = jnp.split(xb, RWKV_SPLITS, axis=-1)
    w = -jax.nn.softplus(-(w0 + jnp.tanh(lw) @ w2).astype(f32)) - 0.5
    decay = jnp.exp(-jnp.exp(w))
    a = jax.nn.sigmoid((a0 + la @ a2).astype(f32))
    g = jax.nn.sigmoid(lg) @ g2
    hd = lambda t: t.reshape(B, S, RWKV_HEADS, RWKV_HEAD)
    kk = hd((k * k_k).astype(f32))
    kk = kk / jnp.maximum(jnp.linalg.norm(kk, axis=-1, keepdims=True), 1e-12)
    k = k.astype(f32) * (1 + (a - 1) * k_a)
    r4, k4, v4, a4, w4 = hd(r.astype(f32)), hd(k), hd(v.astype(f32)), hd(a), hd(decay)

    def step(state, inp):
        r_t, w_t, k_t, v_t, kk_t, a_t = inp
        sa = jnp.einsum('bhvk,bhk->bhv', state, -kk_t)
        state = (state * w_t[:, :, None, :] + sa[..., None] * (kk_t * a_t)[:, :, None, :]
                 + v_t[..., None] * k_t[:, :, None, :])
        return state, jnp.einsum('bhvk,bhk->bhv', state, r_t)

    tm = lambda t: jnp.moveaxis(t, 1, 0)
    s0 = jnp.zeros((B, RWKV_HEADS, RWKV_HEAD, RWKV_HEAD), f32)
    _, y = lax.scan(step, s0, (tm(r4), tm(w4), tm(k4), tm(v4), tm(kk), tm(a4)))
    y = jnp.moveaxis(y, 0, 1)
    mean = jnp.mean(y, axis=-1, keepdims=True)
    var = jnp.var(y, axis=-1, keepdims=True)
    y = ((y - mean) * lax.rsqrt(var + RWKV_LN_EPS)).reshape(B, S, MIX_W) * ln_w + ln_b
    bonus = jnp.sum(r4 * k4 * r_k, axis=-1, keepdims=True) * v4
    y = y + bonus.reshape(B, S, MIX_W)
    return (y * g).astype(xb.dtype)


def spatial_gating_mixer(xc, ln_w, ln_b, w_s, b_s):
    B, S, _ = xc.shape
    z = jax.nn.gelu(xc)
    u, v = jnp.split(z, 2, axis=-1)
    v32 = v.astype(jnp.float32)
    mean = jnp.mean(v32, axis=-1, keepdims=True)
    var = jnp.var(v32, axis=-1, keepdims=True)
    v = ((v32 - mean) * lax.rsqrt(var + SG_LN_EPS) * ln_w + ln_b).astype(xc.dtype)
    mask = jnp.tril(jnp.ones((SG_CHUNK, SG_CHUNK), dtype=bool))
    w_m = jnp.where(mask, w_s, 0)
    vc = v.reshape(B, S // SG_CHUNK, SG_CHUNK, SG_GROUPS, SG_GROUP_W)
    mixed = jnp.einsum('gts,bnsgc->bntgc', w_m, vc) + jnp.swapaxes(b_s, 0, 1)[None, None, :, :, None]
    return (u * mixed.reshape(B, S, MIX_W)).astype(xc.dtype)


def hgrn2_mixer(xd, lb, norm_w):
    B, S, _ = xd.shape
    f32 = jnp.float32
    q, f, i = jnp.split(xd, 3, axis=-1)
    q = jax.nn.silu(q.astype(f32))
    f = f.astype(f32)
    lb = lb.astype(f32)
    sig = jax.nn.sigmoid(f)
    log_g = jnp.log(jnp.maximum(lb + (1 - lb) * sig, GATE_FLOOR))
    key = (1 - lb) * (1 - sig)
    NC = S // HGRN_CHUNK
    ch = lambda t: t.reshape(B, NC, HGRN_CHUNK, HGRN_HEADS, HGRN_EXPAND).transpose(1, 0, 3, 2, 4)
    mask = jnp.tril(jnp.ones((HGRN_CHUNK, HGRN_CHUNK), dtype=bool))[:, :, None]

    def chunk_step(state, inp):
        q_c, k_c, v_c, lg_c = inp
        cum = jnp.cumsum(lg_c, axis=2)
        o_inter = jnp.einsum('bhtk,bhkv->bhtv', q_c * jnp.exp(cum), state)
        rel = cum[:, :, :, None, :] - cum[:, :, None, :, :]
        dec = jnp.where(mask, jnp.exp(jnp.where(mask, rel, 0.0)), 0.0)
        scores = jnp.einsum('bhtk,bhtsk,bhsk->bhts', q_c, dec, k_c)
        o_intra = jnp.einsum('bhts,bhsv->bhtv', scores, v_c)
        last = cum[:, :, -1:, :]
        state = (jnp.exp(last[:, :, 0, :, None]) * state
                 + jnp.einsum('bhsk,bhsv->bhkv', k_c * jnp.exp(last - cum), v_c))
        return state, o_inter + o_intra

    s0 = jnp.zeros((B, HGRN_HEADS, HGRN_EXPAND, HGRN_EXPAND), f32)
    _, o = lax.scan(chunk_step, s0, (ch(q), ch(key), ch(i.astype(f32)), ch(log_g)))
    o = o.transpose(1, 0, 3, 2, 4).reshape(B, S, HGRN_HEADS, HGRN_EXPAND)
    o = o * lax.rsqrt(jnp.mean(o * o, axis=-1, keepdims=True) + NORM_EPS)
    return (o.reshape(B, S, MIX_W) * norm_w).astype(xd.dtype)


def conv_glu_ffn(h, w_up, conv_w, conv_b, w_down):
    u = h @ w_up
    u = conv_b + sum(conv_w[CONV_W - 1 - j] * shift_right(u, j) for j in range(CONV_W))
    gate, val = jnp.split(u, 2, axis=-1)
    return (jax.nn.gelu(gate) * val) @ w_down


def setup_inputs(seed: int = 0) -> dict:
    key = jax.random.key(seed)
    ks = iter(jax.random.split(key, 40))
    nrm = lambda shape, s: jax.random.normal(next(ks), shape, jnp.float32) * s
    gain = lambda shape: 1.0 + nrm(shape, 0.02)
    L = DEPTH
    return {
        "x": nrm((BATCH, SEQ, D_MODEL), 1.0),
        "p": nrm((L, BATCH, SEQ, PLE_DIM), 1.0),
        "norm_mix": gain((L, D_MODEL)),
        "w_in": nrm((L, D_MODEL, D_IN), D_MODEL ** -0.5),
        "pool_w": nrm((L, len(POOL_WINDOWS), POOL_GROUP, POOL_GROUP), POOL_GROUP ** -0.5),
        "pool_scale": gain((L, MIX_W)),
        "rwkv_mu": jax.random.uniform(next(ks), (L, RWKV_IN), jnp.float32),
        "rwkv_w0": jax.random.uniform(next(ks), (L, MIX_W), jnp.float32, -6.0, -1.0),
        "rwkv_w2": nrm((L, RWKV_DECAY_RANK, MIX_W), 0.1 * RWKV_DECAY_RANK ** -0.5),
        "rwkv_a0": nrm((L, MIX_W), 0.1),
        "rwkv_a2": nrm((L, RWKV_A_RANK, MIX_W), RWKV_A_RANK ** -0.5),
        "rwkv_g2": nrm((L, RWKV_G_RANK, MIX_W), RWKV_G_RANK ** -0.5),
        "rwkv_kk": 0.85 + nrm((L, MIX_W), 0.02),
        "rwkv_ka": gain((L, MIX_W)),
        "rwkv_rk": nrm((L, RWKV_HEADS, RWKV_HEAD), 0.1),
        "rwkv_ln_w": gain((L, MIX_W)),
        "rwkv_ln_b": nrm((L, MIX_W), 0.02),
        "sg_ln_w": gain((L, MIX_W)),
        "sg_ln_b": nrm((L, MIX_W), 0.02),
        "sg_w": nrm((L, SG_GROUPS, SG_CHUNK, SG_CHUNK), SG_CHUNK ** -0.5),
        "sg_b": gain((L, SG_GROUPS, SG_CHUNK)),
        "hgrn_lb": nrm((L, MIX_W), 0.5),
        "hgrn_norm": gain((L, MIX_W)),
        "w_branch": nrm((L, N_BRANCH, MIX_W, D_MODEL), MIX_W ** -0.5),
        "w_out": nrm((L, D_MODEL, D_MODEL), D_MODEL ** -0.5),
        "norm_ffn": gain((L, D_MODEL)),
        "ffn_up": nrm((L, D_MODEL, 2 * D_FF), D_MODEL ** -0.5),
        "ffn_conv": nrm((L, CONV_W, 2 * D_FF), CONV_W ** -0.5),
        "ffn_conv_b": nrm((L, 2 * D_FF), 0.02),
        "ffn_down": nrm((L, D_FF, D_MODEL), D_FF ** -0.5),
        "norm_ple": gain((L, D_MODEL)),
        "ple_proj": nrm((L, PLE_DIM, D_MODEL), PLE_DIM ** -0.5),
        "ple_gate": nrm((L, D_MODEL, D_MODEL), D_MODEL ** -0.5),
        "norm_final": gain((D_MODEL,)),
    }


def reference(x, p, norm_mix, w_in, pool_w, pool_scale, rwkv_mu, rwkv_w0, rwkv_w2, rwkv_a0, rwkv_a2,
              rwkv_g2, rwkv_kk, rwkv_ka, rwkv_rk, rwkv_ln_w, rwkv_ln_b, sg_ln_w, sg_ln_b, sg_w, sg_b,
              hgrn_lb, hgrn_norm, w_branch, w_out, norm_ffn, ffn_up, ffn_conv, ffn_conv_b, ffn_down,
              norm_ple, ple_proj, ple_gate, norm_final):
    B, S, _ = x.shape
    probs = jax.nn.softmax(hgrn_lb.astype(jnp.float32), axis=0)
    lower_bounds = jnp.cumsum(probs, axis=0) - probs[0]
    for l in range(DEPTH):
        h = rmsnorm(x, norm_mix[l])
        proj = h @ w_in[l]
        xa, xb, xc, xd, g = jnp.split(proj, IN_SPLITS, axis=-1)
        branches = (
            pool_mixer(xa, pool_w[l], pool_scale[l]),
            rwkv7_mixer(xb, rwkv_mu[l], rwkv_w0[l], rwkv_w2[l], rwkv_a0[l], rwkv_a2[l], rwkv_g2[l],
                        rwkv_kk[l], rwkv_ka[l], rwkv_rk[l], rwkv_ln_w[l], rwkv_ln_b[l]),
            spatial_gating_mixer(xc, sg_ln_w[l], sg_ln_b[l], sg_w[l], sg_b[l]),
            hgrn2_mixer(xd, lower_bounds[l], hgrn_norm[l]),
        )
        gates = jax.nn.sigmoid(g.reshape(B, S, N_BRANCH, D_MODEL))
        merged = sum(gates[:, :, k] * (branches[k] @ w_branch[l, k]) for k in range(N_BRANCH))
        x = x + merged @ w_out[l]
        x = x + conv_glu_ffn(rmsnorm(x, norm_ffn[l]), ffn_up[l], ffn_conv[l], ffn_conv_b[l], ffn_down[l])
        x = x + (p[l] @ ple_proj[l]) * jax.nn.sigmoid(rmsnorm(x, norm_ple[l]) @ ple_gate[l])
    return rmsnorm(x, norm_final)
```

```python
import functools

import jax
import jax.numpy as jnp
from jax import lax
from jax.experimental import pallas as pl
from jax.experimental.pallas import tpu as pltpu

F32 = jnp.float32
BF16 = jnp.bfloat16

D_MODEL = 1024
MIX_W = 512
PLE_DIM = 256
POOL_WINDOWS = (2, 4, 8, 16)
POOL_GROUP = 128
POOL_HALO = 16
RWKV_HEAD = 64
RWKV_IN = 1792
RWKV_LN_EPS = 64e-5
SG_CHUNK = 128
SG_GROUPS = 4
SG_LN_EPS = 1e-5
HGRN_HEADS = 4
HGRN_EXPAND = 128
GATE_FLOOR = 1e-30
D_FF = 2816
NORM_EPS = 1e-6
COL_A, COL_B, COL_C, COL_D = 512, 1792, 1024, 1536

LANES = 128
SUBLANES = 8
VMEM_LIMIT_BYTES = 56 * 1024 * 1024

RWKV_CHUNK = 64
HGRN_CHUNK = 64
HGRN_SUB = 16

FFN_BLOCK = 256
FFN_NB = D_FF // FFN_BLOCK


def _cparams(sem):
    return pltpu.CompilerParams(dimension_semantics=sem, vmem_limit_bytes=VMEM_LIMIT_BYTES)


def _nn(a, b):
    return lax.dot_general(a, b, (((1,), (0,)), ((), ())), preferred_element_type=F32)


def _nt(a, b):
    return lax.dot_general(a, b, (((1,), (1,)), ((), ())), preferred_element_type=F32)


def _tn(a, b):
    return lax.dot_general(a, b, (((0,), (0,)), ((), ())), preferred_element_type=F32)


def _split(a, terms):
    out = []
    rem = a
    for _ in range(terms - 1):
        hi = rem.astype(BF16)
        out.append(hi)
        rem = rem - hi.astype(F32)
    out.append(rem.astype(BF16))
    return out


def _mm(dot, a, b, prec):
    if prec == 1:
        return dot(a.astype(BF16), b.astype(BF16))
    a_hi, a_lo = _split(a, 2)
    b_hi, b_lo = _split(b, 2)
    return dot(a_hi, b_hi) + (dot(a_hi, b_lo) + dot(a_lo, b_hi))


def _mm_exact_rhs(a, b_bf16, terms):
    acc = None
    for part in _split(a, terms):
        t = _nn(part, b_bf16)
        acc = t if acc is None else acc + t
    return acc


def _rms(x, gain):
    return x * lax.rsqrt(jnp.mean(x * x, axis=-1, keepdims=True) + NORM_EPS) * gain


def _sigmoid(x):
    return 1.0 / (1.0 + jnp.exp(-x))


def _gelu(x):
    return 0.5 * x * (1.0 + jnp.tanh(0.7978845608028654 * (x + 0.044715 * (x * x * x))))


def _iota(shape, dim):
    return lax.broadcasted_iota(jnp.int32, shape, dim)


def _shift_rows(x, prev_rows, n):
    rolled = pltpu.roll(x, n, axis=0)
    row = _iota(x.shape, 0)
    out = rolled
    for j in range(n):
        out = jnp.where(row == j, prev_rows[SUBLANES - n + j:SUBLANES - n + j + 1, :], out)
    return out


def _norm_matmul_kernel(x_ref, g_ref, w_ref, o_ref, h_ref):
    @pl.when(pl.program_id(1) == 0)
    def _():
        h_ref[...] = _rms(x_ref[...], g_ref[...]).astype(BF16)

    o_ref[...] = _nn(h_ref[...], w_ref[...]).astype(o_ref.dtype)


def _norm_matmul(x, gain, w_bf16, out_dtype, tm, tn):
    n, d = x.shape
    m = w_bf16.shape[1]
    return pl.pallas_call(
        _norm_matmul_kernel,
        grid=(n // tm, m // tn),
        in_specs=[
            pl.BlockSpec((tm, d), lambda i, j: (i, 0)),
            pl.BlockSpec((1, d), lambda i, j: (0, 0)),
            pl.BlockSpec((d, tn), lambda i, j: (0, j)),
        ],
        out_specs=pl.BlockSpec((tm, tn), lambda i, j: (i, j)),
        out_shape=jax.ShapeDtypeStruct((n, m), out_dtype),
        scratch_shapes=[pltpu.VMEM((tm, d), BF16)],
        compiler_params=_cparams(("arbitrary", "arbitrary")),
        name="norm_matmul",
    )(x, gain.reshape(1, d), w_bf16)


def _pool_kernel(xa_ref, w_ref, scale_ref, o_ref, carry_ref):
    t_tile = xa_ref.shape[1]
    s_idx = pl.program_id(1)

    @pl.when(s_idx == 0)
    def _():
        carry_ref[...] = jnp.zeros_like(carry_ref)

    a = xa_ref[0].astype(F32)
    ext = jnp.concatenate([carry_ref[...], a], axis=0)
    carry_ref[...] = a[t_tile - POOL_HALO:, :]
    pos = s_idx * t_tile + _iota((t_tile, 1), 0)
    count = (pos + 1).astype(F32)
    outs = []
    for gi, win in enumerate(POOL_WINDOWS):
        col = slice(gi * POOL_GROUP, (gi + 1) * POOL_GROUP)
        s = ext[:, col]
        span = 1
        while span < win:
            s = s + pltpu.roll(s, span, axis=0)
            span *= 2
        mean = s[POOL_HALO:, :] / jnp.minimum(count, float(win))
        diff = (mean - a[:, col]).astype(BF16)
        outs.append(_nn(diff, w_ref[gi]))
    y = jnp.concatenate(outs, axis=1) * scale_ref[...]
    o_ref[0] = y.astype(o_ref.dtype)


def _pool_mixer(xa, pool_w_bf16, pool_scale, t_tile):
    b, s, _ = xa.shape
    return pl.pallas_call(
        _pool_kernel,
        grid=(b, s // t_tile),
        in_specs=[
            pl.BlockSpec((1, t_tile, MIX_W), lambda i, j: (i, j, 0)),
            pl.BlockSpec((len(POOL_WINDOWS), POOL_GROUP, POOL_GROUP), lambda i, j: (0, 0, 0)),
            pl.BlockSpec((1, MIX_W), lambda i, j: (0, 0)),
        ],
        out_specs=pl.BlockSpec((1, t_tile, MIX_W), lambda i, j: (i, j, 0)),
        out_shape=jax.ShapeDtypeStruct((b, s, MIX_W), BF16),
        scratch_shapes=[pltpu.VMEM((POOL_HALO, MIX_W), F32)],
        compiler_params=_cparams(("arbitrary", "arbitrary")),
        name="pool_mixer",
    )(xa, pool_w_bf16, pool_scale.reshape(1, MIX_W))


RWKV_PREC_SCORE = 3
RWKV_PREC_SOLVE = 3
RWKV_PREC_STATE = 3


def _rwkv_kernel(xb_ref, mu_ref, w0_ref, wla_ref, a0_ref, g2_ref, kk_ref, ka_ref, rk_ref,
                 lnw_ref, lnb_ref, hsum_ref, tril_ref, o_ref, carry_ref, state_ref):
    t_tile = xb_ref.shape[1]
    c = RWKV_CHUNK
    hd = RWKV_HEAD

    @pl.when(pl.program_id(1) == 0)
    def _():
        carry_ref[...] = jnp.zeros_like(carry_ref)
        state_ref[...] = jnp.zeros_like(state_ref)

    lane = _iota((1, LANES), 1)
    head0 = lane < hd
    col_s = jnp.bitwise_and(_iota((c, LANES), 1), hd - 1)
    row_t = _iota((c, LANES), 0)
    strict = row_t > col_s
    incl = row_t >= col_s
    lo_half = _iota((c, LANES), 1) < hd
    bd_mask = (_iota((LANES, LANES), 0) < hd) == (_iota((LANES, LANES), 1) < hd)
    zeros_cl = jnp.zeros((c, LANES), F32)

    def chunk(ci, carry):
        r0 = pl.multiple_of(ci * c, c)
        x = xb_ref[0, pl.ds(r0, c), :]
        xprev = _shift_rows(x, carry_ref[...], 1)
        carry_ref[...] = x[c - SUBLANES:, :]
        xs = x + mu_ref[...] * (xprev - x)
        r = xs[:, 0:MIX_W]
        k = xs[:, MIX_W:2 * MIX_W]
        v = xs[:, 2 * MIX_W:3 * MIX_W]
        slab = xs[:, 3 * MIX_W:3 * MIX_W + LANES]
        lg = xs[:, 3 * MIX_W + LANES:]
        slab = jnp.where(head0, jnp.tanh(slab), slab)
        low = _nn(slab.astype(BF16), wla_ref[...])
        wl = w0_ref[...] + low[:, :MIX_W]
        neg = -wl
        softplus = jnp.maximum(neg, 0.0) + jnp.log(1.0 + jnp.exp(-jnp.abs(neg)))
        ld = -jnp.exp(-softplus - 0.5)
        eta = _sigmoid(a0_ref[...] + low[:, MIX_W:])
        gate = _nn(_sigmoid(lg).astype(BF16), g2_ref[...])
        kkv = k * kk_ref[...]
        nrm2 = _mm_exact_rhs(kkv * kkv, hsum_ref[...], 2)
        kkn = kkv / jnp.maximum(jnp.sqrt(nrm2), 1e-12)
        k2 = k * (1.0 + (eta - 1.0) * ka_ref[...])
        cum = _mm_exact_rhs_left(tril_ref[...], ld)
        tot = cum[c - 1:c, :]
        e_neg = jnp.exp(-cum)
        rt = r * jnp.exp(cum)
        at = -kkn * jnp.exp(cum - ld)
        b_vec = kkn * eta
        bt = b_vec * e_neg
        kt = k2 * e_neg
        e_end = jnp.exp(tot - cum)
        b_end = b_vec * e_end
        k_end = k2 * e_end
        p_tot = jnp.exp(tot)

        pieces = []
        for p in range(MIX_W // LANES):
            sl = slice(p * LANES, (p + 1) * LANES)
            at_p, rt_p = at[:, sl], rt[:, sl]
            lhs = jnp.concatenate([jnp.where(head0, at_p, 0.0), jnp.where(head0, 0.0, at_p),
                                   jnp.where(head0, rt_p, 0.0), jnp.where(head0, 0.0, rt_p)], axis=0)
            rhs = jnp.concatenate([bt[:, sl], kt[:, sl], kt[:, sl], bt[:, sl]], axis=0)
            sc = _mm(_nt, lhs, rhs, RWKV_PREC_SCORE)
            s_a0 = sc[0:c, 0:LANES]
            s_a1 = sc[c:2 * c, LANES:]
            s_r0 = sc[2 * c:3 * c, 0:LANES]
            s_r1 = sc[3 * c:, LANES:]
            a_bd = jnp.concatenate([jnp.where(strict & lo_half, s_a0, 0.0),
                                    jnp.where(strict & (~lo_half), s_a1, 0.0)], axis=0)
            ak0 = jnp.where(strict & (~lo_half), s_a0, 0.0)
            ak1 = jnp.where(strict & lo_half, s_a1, 0.0)
            rb0 = jnp.where(incl, s_r0, 0.0)
            rb1 = jnp.where(incl, s_r1, 0.0)
            v_p = v[:, sl]
            v0 = jnp.where(head0, v_p, 0.0)
            v1 = jnp.where(head0, 0.0, v_p)
            s_mat = state_ref[p]
            w_p = (_mm(_nt, at_p, s_mat, RWKV_PREC_STATE)
                   + _mm(_nn, jnp.concatenate([ak0, ak1], axis=1),
                         jnp.concatenate([zeros_cl, v0, v1, zeros_cl], axis=0), RWKV_PREC_STATE))
            pieces.append((a_bd, w_p, rb0, rb1, v0, v1, rt_p, s_mat, v_p))

        u_pairs = []
        for q in range(2):
            (a0_, w0_), (a1_, w1_) = [(pieces[2 * q + i][0], pieces[2 * q + i][1]) for i in range(2)]
            zeros_2c = jnp.zeros((2 * c, LANES), F32)
            a_mat = jnp.concatenate([jnp.concatenate([a0_, zeros_2c], axis=1),
                                     jnp.concatenate([zeros_2c, a1_], axis=1)], axis=0)
            xst = jnp.concatenate([jnp.where(head0, w0_, 0.0), jnp.where(head0, 0.0, w0_),
                                   jnp.where(head0, w1_, 0.0), jnp.where(head0, 0.0, w1_)], axis=0)
            pw = a_mat
            nsteps = c.bit_length() - 1
            for i in range(nsteps):
                xst = xst + _mm(_nn, pw, xst, RWKV_PREC_SOLVE)
                if i + 1 < nsteps:
                    pw = _mm(_nn, pw, pw, RWKV_PREC_SOLVE)
            u_pairs.append(xst[0:c] + xst[c:2 * c])
            u_pairs.append(xst[2 * c:3 * c] + xst[3 * c:])

        ys = []
        for p in range(MIX_W // LANES):
            sl = slice(p * LANES, (p + 1) * LANES)
            a_bd, w_p, rb0, rb1, v0, v1, rt_p, s_mat, v_p = pieces[p]
            u_p = u_pairs[p]
            u0 = jnp.where(head0, u_p, 0.0)
            u1 = jnp.where(head0, 0.0, u_p)
            y_p = (_mm(_nt, rt_p, s_mat, RWKV_PREC_STATE)
                   + _mm(_nn, jnp.concatenate([rb0, rb1], axis=1),
                         jnp.concatenate([u0, v0, v1, u1], axis=0), RWKV_PREC_STATE))
            ys.append(y_p)
            upd = _mm(_tn, jnp.concatenate([u_p, v_p], axis=0),
                      jnp.concatenate([b_end[:, sl], k_end[:, sl]], axis=0), RWKV_PREC_STATE)
            state_ref[p] = jnp.where(bd_mask, s_mat * p_tot[:, sl] + upd, 0.0)
        y = jnp.concatenate(ys, axis=1)

        inv_hd = 1.0 / hd
        mean = _mm_exact_rhs(y, hsum_ref[...], 2) * inv_hd
        yc = y - mean
        var = _mm_exact_rhs(yc * yc, hsum_ref[...], 2) * inv_hd
        yn = yc * lax.rsqrt(var + RWKV_LN_EPS) * lnw_ref[...] + lnb_ref[...]
        bonus = _mm_exact_rhs(r * k2 * rk_ref[...], hsum_ref[...], 2) * v
        o_ref[0, pl.ds(r0, c), :] = ((yn + bonus) * gate).astype(o_ref.dtype)
        return carry

    lax.fori_loop(0, t_tile // c, chunk, 0)


def _mm_exact_rhs_left(l_bf16, a):
    acc = None
    for part in _split(a, 3):
        t = _nn(l_bf16, part)
        acc = t if acc is None else acc + t
    return acc


def _rwkv_mixer(xb, mu, w0, w2, a0, a2, g2, k_k, k_a, r_k, ln_w, ln_b, t_tile):
    b, s, _ = xb.shape
    c = RWKV_CHUNK
    rank = w2.shape[0]
    wla = jnp.zeros((LANES, 2 * MIX_W), F32)
    wla = wla.at[:rank, :MIX_W].set(w2).at[rank:, MIX_W:].set(a2).astype(BF16)
    heads = MIX_W // RWKV_HEAD
    hsum = jnp.kron(jnp.eye(heads, dtype=F32), jnp.ones((RWKV_HEAD, RWKV_HEAD), F32)).astype(BF16)
    tril = jnp.tril(jnp.ones((c, c), F32)).astype(BF16)
    row = lambda a: a.reshape(1, -1).astype(F32)
    full = lambda shape: pl.BlockSpec(shape, lambda i, j: (0,) * len(shape))
    return pl.pallas_call(
        _rwkv_kernel,
        grid=(b, s // t_tile),
        in_specs=[
            pl.BlockSpec((1, t_tile, RWKV_IN), lambda i, j: (i, j, 0)),
            full((1, RWKV_IN)), full((1, MIX_W)), full((LANES, 2 * MIX_W)), full((1, MIX_W)),
            full((LANES, MIX_W)), full((1, MIX_W)), full((1, MIX_W)), full((1, MIX_W)),
            full((1, MIX_W)), full((1, MIX_W)), full((MIX_W, MIX_W)), full((c, c)),
        ],
        out_specs=pl.BlockSpec((1, t_tile, MIX_W), lambda i, j: (i, j, 0)),
        out_shape=jax.ShapeDtypeStruct((b, s, MIX_W), BF16),
        scratch_shapes=[pltpu.VMEM((SUBLANES, RWKV_IN), F32),
                        pltpu.VMEM((MIX_W // LANES, LANES, LANES), F32)],
        compiler_params=_cparams(("arbitrary", "arbitrary")),
        name="rwkv7_mixer",
    )(xb, row(mu), row(w0), wla, row(a0), g2.astype(BF16), row(k_k), row(k_a), row(r_k),
      row(ln_w), row(ln_b), hsum, tril)


def _sg_kernel(xc_ref, lnw_ref, lnb_ref, ws_ref, bs_ref, o_ref):
    t_tile = xc_ref.shape[0]
    z = _gelu(xc_ref[...].astype(F32))
    u = z[:, :MIX_W]
    v = z[:, MIX_W:]
    mean = jnp.mean(v, axis=-1, keepdims=True)
    vc = v - mean
    var = jnp.mean(vc * vc, axis=-1, keepdims=True)
    vn = (vc * lax.rsqrt(var + SG_LN_EPS) * lnw_ref[...] + lnb_ref[...]).astype(BF16)
    causal = _iota((SG_CHUNK, SG_CHUNK), 0) >= _iota((SG_CHUNK, SG_CHUNK), 1)
    w_m = [jnp.where(causal, ws_ref[g], 0.0).astype(BF16) for g in range(SG_GROUPS)]
    for n in range(t_tile // SG_CHUNK):
        rows = slice(n * SG_CHUNK, (n + 1) * SG_CHUNK)
        mixed = jnp.concatenate(
            [_nn(w_m[g], vn[rows, g * LANES:(g + 1) * LANES]) for g in range(SG_GROUPS)], axis=1)
        o_ref[rows, :] = (u[rows, :] * (mixed + bs_ref[...])).astype(o_ref.dtype)


def _sg_mixer(xc, ln_w, ln_b, w_s, b_s, t_tile):
    n = xc.shape[0]
    b_full = jnp.repeat(jnp.swapaxes(b_s, 0, 1), MIX_W // SG_GROUPS, axis=1)
    return pl.pallas_call(
        _sg_kernel,
        grid=(n // t_tile,),
        in_specs=[
            pl.BlockSpec((t_tile, 2 * MIX_W), lambda i: (i, 0)),
            pl.BlockSpec((1, MIX_W), lambda i: (0, 0)),
            pl.BlockSpec((1, MIX_W), lambda i: (0, 0)),
            pl.BlockSpec((SG_GROUPS, SG_CHUNK, SG_CHUNK), lambda i: (0, 0, 0)),
            pl.BlockSpec((SG_CHUNK, MIX_W), lambda i: (0, 0)),
        ],
        out_specs=pl.BlockSpec((t_tile, MIX_W), lambda i: (i, 0)),
        out_shape=jax.ShapeDtypeStruct((n, MIX_W), BF16),
        compiler_params=_cparams(("arbitrary",)),
        name="spatial_gating_mixer",
    )(xc, ln_w.reshape(1, MIX_W), ln_b.reshape(1, MIX_W), w_s, b_full)


HGRN_PREC = 3


def _hgrn_kernel(xd_ref, lb_ref, nw_ref, tril_ref, o_ref, state_ref, kpad_ref, cpad_ref, vpad_ref,
                 *, layer):
    t_tile = xd_ref.shape[1]
    c = HGRN_CHUNK
    sub = HGRN_SUB
    nsub = c // sub

    @pl.when(pl.program_id(1) == 0)
    def _():
        state_ref[...] = jnp.zeros_like(state_ref)

    kpad_ref[0:sub, :] = jnp.zeros((sub, MIX_W), F32)
    cpad_ref[0:sub, :] = jnp.zeros((sub, MIX_W), F32)
    vpad_ref[0:sub, :] = jnp.zeros((sub, MIX_W), F32)
    lb_all = lb_ref[...]
    e_all = jnp.exp(lb_all - jnp.max(lb_all, axis=0, keepdims=True))
    probs = e_all / jnp.sum(e_all, axis=0, keepdims=True)
    lb = jnp.sum(probs[0:layer + 1, :], axis=0, keepdims=True) - probs[0:1, :]
    row_in_sub = jnp.bitwise_and(_iota((c, 1), 0), sub - 1)
    key_pos = _iota((c, 1), 0)

    def chunk(ci, carry):
        r0 = pl.multiple_of(ci * c, c)
        x = xd_ref[0, pl.ds(r0, c), :]
        qr = x[:, 0:MIX_W]
        q = qr * _sigmoid(qr)
        sig = _sigmoid(x[:, MIX_W:2 * MIX_W])
        v = x[:, 2 * MIX_W:]
        lg = jnp.log(jnp.maximum(lb + (1.0 - lb) * sig, GATE_FLOOR))
        key = (1.0 - lb) * (1.0 - sig)
        cum = _mm_exact_rhs_left(tril_ref[...], lg)
        tot = cum[c - 1:c, :]
        kpad_ref[sub:, :] = key
        cpad_ref[sub:, :] = cum
        vpad_ref[sub:, :] = v

        cs_rows = []
        for i in range(nsub):
            first = i * sub
            cs_i = cum[first:first + 1, :] - lg[first:first + 1, :]
            cs_rows.append(jnp.broadcast_to(cs_i, (sub, MIX_W)))
        cs = jnp.concatenate(cs_rows, axis=0)
        q_hat = q * jnp.exp(cum - cs)
        q_chunk = q * jnp.exp(cum)
        k_end = key * jnp.exp(tot - cum)

        outs = []
        for h in range(HGRN_HEADS):
            sl = slice(h * LANES, (h + 1) * LANES)
            s_mat = state_ref[h]
            o_h = _mm(_nt, q_chunk[:, sl], s_mat, HGRN_PREC)
            acc = jnp.zeros((c, LANES), F32)
            q_h = q[:, sl]
            cum_h = cum[:, sl]
            for d in range(sub):
                kd = kpad_ref[sub - d:sub - d + c, sl]
                cd = cpad_ref[sub - d:sub - d + c, sl]
                vd = vpad_ref[sub - d:sub - d + c, sl]
                valid = row_in_sub >= d
                dec = jnp.exp(jnp.where(valid, cum_h - cd, 0.0))
                score = jnp.sum(q_h * kd * dec, axis=-1, keepdims=True)
                acc = acc + jnp.where(valid, score, 0.0) * vd
            o_h = o_h + acc
            blocks = [jnp.zeros((sub, LANES), F32)]
            for i in range(1, nsub):
                rows = slice(i * sub, (i + 1) * sub)
                earlier = key_pos < i * sub
                expo = jnp.where(earlier, cs[i * sub:i * sub + 1, sl] - cum_h, 0.0)
                k_hat = jnp.where(earlier, key[:, sl] * jnp.exp(expo), 0.0)
                sc = _mm(_nt, q_hat[rows, sl], k_hat, HGRN_PREC)
                blocks.append(_mm(_nn, sc, v[:, sl], HGRN_PREC))
            o_h = o_h + jnp.concatenate(blocks, axis=0)
            ms = jnp.mean(o_h * o_h, axis=-1, keepdims=True)
            outs.append(o_h * lax.rsqrt(ms + NORM_EPS))
            upd = _mm(_tn, v[:, sl], k_end[:, sl], HGRN_PREC)
            state_ref[h] = s_mat * jnp.exp(tot[:, sl]) + upd
        o = jnp.concatenate(outs, axis=1) * nw_ref[...]
        o_ref[0, pl.ds(r0, c), :] = o.astype(o_ref.dtype)
        return carry

    lax.fori_loop(0, t_tile // c, chunk, 0)


def _hgrn_mixer(xd, lb_all, layer, norm_w, t_tile):
    b, s, _ = xd.shape
    c = HGRN_CHUNK
    depth = lb_all.shape[0]
    tril = jnp.tril(jnp.ones((c, c), F32)).astype(BF16)
    full = lambda shape: pl.BlockSpec(shape, lambda i, j: (0,) * len(shape))
    return pl.pallas_call(
        functools.partial(_hgrn_kernel, layer=layer),
        grid=(b, s // t_tile),
        in_specs=[
            pl.BlockSpec((1, t_tile, COL_D), lambda i, j: (i, j, 0)),
            full((depth, MIX_W)), full((1, MIX_W)), full((c, c)),
        ],
        out_specs=pl.BlockSpec((1, t_tile, MIX_W), lambda i, j: (i, j, 0)),
        out_shape=jax.ShapeDtypeStruct((b, s, MIX_W), BF16),
        scratch_shapes=[pltpu.VMEM((HGRN_HEADS, HGRN_EXPAND, HGRN_EXPAND), F32),
                        pltpu.VMEM((HGRN_SUB + c, MIX_W), F32),
                        pltpu.VMEM((HGRN_SUB + c, MIX_W), F32),
                        pltpu.VMEM((HGRN_SUB + c, MIX_W), F32)],
        compiler_params=_cparams(("arbitrary", "arbitrary")),
        name="hgrn2_mixer",
    )(xd, lb_all.astype(F32), norm_w.reshape(1, MIX_W), tril)


def _merge_kernel(ya_ref, yb_ref, yc_ref, yd_ref, g_ref, x_ref, wbr_ref, wout_ref, o_ref):
    merged = None
    for kk, y_ref in enumerate((ya_ref, yb_ref, yc_ref, yd_ref)):
        gate = _sigmoid(g_ref[:, kk * D_MODEL:(kk + 1) * D_MODEL].astype(F32))
        term = gate * _nn(y_ref[...], wbr_ref[kk])
        merged = term if merged is None else merged + term
    o_ref[...] = x_ref[...] + _nn(merged.astype(BF16), wout_ref[...])


def _merge(ya, yb, yc, yd, g, x, wbr_bf16, wout_bf16, tm):
    n = x.shape[0]
    ytile = pl.BlockSpec((tm, MIX_W), lambda i: (i, 0))
    return pl.pallas_call(
        _merge_kernel,
        grid=(n // tm,),
        in_specs=[
            ytile, ytile, ytile, ytile,
            pl.BlockSpec((tm, 4 * D_MODEL), lambda i: (i, 0)),
            pl.BlockSpec((tm, D_MODEL), lambda i: (i, 0)),
            pl.BlockSpec((4, MIX_W, D_MODEL), lambda i: (0, 0, 0)),
            pl.BlockSpec((D_MODEL, D_MODEL), lambda i: (0, 0)),
        ],
        out_specs=pl.BlockSpec((tm, D_MODEL), lambda i: (i, 0)),
        out_shape=jax.ShapeDtypeStruct((n, D_MODEL), F32),
        compiler_params=_cparams(("arbitrary",)),
        name="merge_out",
    )(ya, yb, yc, yd, g, x, wbr_bf16, wout_bf16)


def _ffn_kernel(x_ref, nf_ref, wg_ref, wv_ref, cwg_ref, cwv_ref, cbg_ref, cbv_ref, wd_ref,
                p_ref, np_ref, pproj_ref, pgate_ref, nfin_ref, o_ref, acc_ref, cg_ref, cv_ref, h_ref,
                *, final_norm):
    j = pl.program_id(2)
    tm = x_ref.shape[1]

    @pl.when(j == 0)
    def _():
        h_ref[...] = _rms(x_ref[0], nf_ref[...]).astype(BF16)
        acc_ref[...] = jnp.zeros_like(acc_ref)

    @pl.when((j == 0) & (pl.program_id(1) == 0))
    def _():
        cg_ref[...] = jnp.zeros_like(cg_ref)
        cv_ref[...] = jnp.zeros_like(cv_ref)

    h = h_ref[...]

    def conv(u, carry_ref, cw_ref, cb_ref):
        prev = carry_ref[j]
        u1 = _shift_rows(u, prev, 1)
        u2 = _shift_rows(u, prev, 2)
        carry_ref[j] = u[tm - SUBLANES:, :]
        cw = cw_ref[0]
        return cb_ref[0] + cw[2:3, :] * u + cw[1:2, :] * u1 + cw[0:1, :] * u2

    ug = conv(_nn(h, wg_ref[0]), cg_ref, cwg_ref, cbg_ref)
    uv = conv(_nn(h, wv_ref[0]), cv_ref, cwv_ref, cbv_ref)
    act = (_gelu(ug) * uv).astype(BF16)
    acc_ref[...] += _nn(act, wd_ref[0])

    @pl.when(j == pl.num_programs(2) - 1)
    def _():
        x2 = x_ref[0] + acc_ref[...]
        hp = _rms(x2, np_ref[...]).astype(BF16)
        gate = _sigmoid(_nn(hp, pgate_ref[...]))
        pp = _nn(p_ref[0].astype(BF16), pproj_ref[...])
        x3 = x2 + pp * gate
        if final_norm:
            x3 = _rms(x3, nfin_ref[...])
        o_ref[0] = x3


def _ffn_ple(x1, norm_ffn, wup_bf16, conv_w, conv_b, wdown_bf16, p_l, norm_ple, pproj_bf16,
             pgate_bf16, norm_final, final_norm, tm):
    b, s, d = x1.shape
    nb, blk = FFN_NB, FFN_BLOCK
    wg = wup_bf16[:, :D_FF].reshape(d, nb, blk).transpose(1, 0, 2)
    wv = wup_bf16[:, D_FF:].reshape(d, nb, blk).transpose(1, 0, 2)
    cwg = conv_w[:, :D_FF].reshape(3, nb, blk).transpose(1, 0, 2)
    cwv = conv_w[:, D_FF:].reshape(3, nb, blk).transpose(1, 0, 2)
    cbg = conv_b[:D_FF].reshape(nb, 1, blk)
    cbv = conv_b[D_FF:].reshape(nb, 1, blk)
    wd = wdown_bf16.reshape(nb, blk, d)
    row = lambda a: a.reshape(1, -1)
    const2 = lambda shape: pl.BlockSpec(shape, lambda i, t, j: (0, 0))
    blk3 = lambda shape: pl.BlockSpec(shape, lambda i, t, j: (j, 0, 0))
    return pl.pallas_call(
        functools.partial(_ffn_kernel, final_norm=final_norm),
        grid=(b, s // tm, nb),
        in_specs=[
            pl.BlockSpec((1, tm, d), lambda i, t, j: (i, t, 0)),
            const2((1, d)),
            blk3((1, d, blk)), blk3((1, d, blk)),
            blk3((1, 3, blk)), blk3((1, 3, blk)),
            blk3((1, 1, blk)), blk3((1, 1, blk)),
            blk3((1, blk, d)),
            pl.BlockSpec((1, tm, PLE_DIM), lambda i, t, j: (i, t, 0)),
            const2((1, d)),
            const2((PLE_DIM, d)),
            const2((d, d)),
            const2((1, d)),
        ],
        out_specs=pl.BlockSpec((1, tm, d), lambda i, t, j: (i, t, 0)),
        out_shape=jax.ShapeDtypeStruct((b, s, d), F32),
        scratch_shapes=[pltpu.VMEM((tm, d), F32),
                        pltpu.VMEM((nb, SUBLANES, blk), F32),
                        pltpu.VMEM((nb, SUBLANES, blk), F32),
                        pltpu.VMEM((tm, d), BF16)],
        compiler_params=_cparams(("arbitrary", "arbitrary", "arbitrary")),
        name="ffn_ple",
    )(x1, row(norm_ffn), wg, wv, cwg, cwv, cbg, cbv, wd, p_l, row(norm_ple), pproj_bf16,
      pgate_bf16, row(norm_final))


def _pick_tile(total, want):
    t = min(total, want)
    while total % t:
        t //= 2
    return t


def kernel(x, p, norm_mix, w_in, pool_w, pool_scale, rwkv_mu, rwkv_w0, rwkv_w2, rwkv_a0, rwkv_a2, rwkv_g2, rwkv_kk, rwkv_ka, rwkv_rk, rwkv_ln_w, rwkv_ln_b, sg_ln_w, sg_ln_b, sg_w, sg_b, hgrn_lb, hgrn_norm, w_branch, w_out, norm_ffn, ffn_up, ffn_conv, ffn_conv_b, ffn_down, norm_ple, ple_proj, ple_gate, norm_final):
    b, s, d = x.shape
    depth = w_in.shape[0]
    n = b * s
    tm = _pick_tile(n, 1024)
    ts = _pick_tile(s, 512)
    o0, o1, o2, o3 = COL_A, COL_A + COL_B, COL_A + COL_B + COL_C, COL_A + COL_B + COL_C + COL_D
    for l in range(depth):
        w_l = w_in[l].astype(BF16)
        x2d = x.reshape(n, d)
        gain = norm_mix[l]
        xa = _norm_matmul(x2d, gain, w_l[:, :o0], BF16, tm, COL_A)
        xb = _norm_matmul(x2d, gain, w_l[:, o0:o1], F32, tm, COL_B)
        xc = _norm_matmul(x2d, gain, w_l[:, o1:o2], BF16, tm, COL_C)
        xd = _norm_matmul(x2d, gain, w_l[:, o2:o3], F32, tm, COL_D)
        g = _norm_matmul(x2d, gain, w_l[:, o3:], BF16, tm, 1024)
        ya = _pool_mixer(xa.reshape(b, s, COL_A), pool_w[l].astype(BF16), pool_scale[l], ts)
        yb = _rwkv_mixer(xb.reshape(b, s, COL_B), rwkv_mu[l], rwkv_w0[l], rwkv_w2[l], rwkv_a0[l],
                         rwkv_a2[l], rwkv_g2[l], rwkv_kk[l], rwkv_ka[l], rwkv_rk[l].reshape(-1),
                         rwkv_ln_w[l], rwkv_ln_b[l], ts)
        yc = _sg_mixer(xc, sg_ln_w[l], sg_ln_b[l], sg_w[l], sg_b[l], _pick_tile(n, 512))
        yd = _hgrn_mixer(xd.reshape(b, s, COL_D), hgrn_lb, l, hgrn_norm[l], ts)
        x1 = _merge(ya.reshape(n, MIX_W), yb.reshape(n, MIX_W), yc, yd.reshape(n, MIX_W), g, x2d,
                    w_branch[l].astype(BF16), w_out[l].astype(BF16), _pick_tile(n, 512))
        x = _ffn_ple(x1.reshape(b, s, d), norm_ffn[l], ffn_up[l].astype(BF16), ffn_conv[l],
                     ffn_conv_b[l], ffn_down[l].astype(BF16), p[l], norm_ple[l],
                     ple_proj[l].astype(BF16), ple_gate[l].astype(BF16), norm_final,
                     l == depth - 1, _pick_tile(s, 1024))
    return x
```

```python
import functools

import jax
import jax.numpy as jnp
from jax import lax
from jax.experimental import pallas as pl
from jax.experimental.pallas import tpu as pltpu

F32 = jnp.float32
BF16 = jnp.bfloat16

D_MODEL = 1024
MIX_W = 512
PLE_DIM = 256
POOL_WINDOWS = (2, 4, 8, 16)
POOL_GROUP = 128
POOL_HALO = 16
RWKV_HEAD = 64
RWKV_IN = 1792
RWKV_LN_EPS = 64e-5
SG_CHUNK = 128
SG_GROUPS = 4
SG_LN_EPS = 1e-5
HGRN_HEADS = 4
HGRN_EXPAND = 128
GATE_FLOOR = 1e-30
D_FF = 2816
NORM_EPS = 1e-6
COL_A, COL_B, COL_C, COL_D = 512, 1792, 1024, 1536

LANES = 128
SUBLANES = 8
VMEM_LIMIT_BYTES = 56 * 1024 * 1024

RWKV_CHUNK = 64
HGRN_CHUNK = 64

FFN_BLOCK = 256
FFN_NB = D_FF // FFN_BLOCK


def _cparams(sem):
    return pltpu.CompilerParams(dimension_semantics=sem, vmem_limit_bytes=VMEM_LIMIT_BYTES)


def _nn(a, b):
    return lax.dot_general(a, b, (((1,), (0,)), ((), ())), preferred_element_type=F32)


def _nt(a, b):
    return lax.dot_general(a, b, (((1,), (1,)), ((), ())), preferred_element_type=F32)


def _tn(a, b):
    return lax.dot_general(a, b, (((0,), (0,)), ((), ())), preferred_element_type=F32)


def _split(a, terms):
    out = []
    rem = a
    for _ in range(terms - 1):
        hi = rem.astype(BF16)
        out.append(hi)
        rem = rem - hi.astype(F32)
    out.append(rem.astype(BF16))
    return out


def _mm(dot, a, b, prec):
    if prec == 1:
        return dot(a.astype(BF16), b.astype(BF16))
    a_hi, a_lo = _split(a, 2)
    b_hi, b_lo = _split(b, 2)
    return dot(a_hi, b_hi) + (dot(a_hi, b_lo) + dot(a_lo, b_hi))


def _mm_exact_rhs(a, b_bf16, terms):
    acc = None
    for part in _split(a, terms):
        t = _nn(part, b_bf16)
        acc = t if acc is None else acc + t
    return acc


def _head_sums(arrays, ones_pair_bf16):
    c = arrays[0].shape[0]
    slabs = [a[:, p * LANES:(p + 1) * LANES] for a in arrays for p in range(MIX_W // LANES)]
    s = _mm_exact_rhs(jnp.concatenate(slabs, axis=0), ones_pair_bf16, 2)
    per = MIX_W // LANES
    return [jnp.concatenate([s[(i * per + p) * c:(i * per + p + 1) * c, :] for p in range(per)], axis=1)
            for i in range(len(arrays))]


def _rms(x, gain):
    return x * lax.rsqrt(jnp.mean(x * x, axis=-1, keepdims=True) + NORM_EPS) * gain


def _sigmoid(x):
    return 1.0 / (1.0 + jnp.exp(-x))


def _gelu(x):
    return 0.5 * x * (1.0 + jnp.tanh(0.7978845608028654 * (x + 0.044715 * (x * x * x))))


def _iota(shape, dim):
    return lax.broadcasted_iota(jnp.int32, shape, dim)


def _round_robin(gens):
    results = [None] * len(gens)
    live = list(range(len(gens)))
    while live:
        for idx in list(live):
            try:
                next(gens[idx])
            except StopIteration as stop:
                results[idx] = stop.value
                live.remove(idx)
    return results


def _shift_rows(x, prev_rows, n):
    rolled = pltpu.roll(x, n, axis=0)
    row = _iota(x.shape, 0)
    out = rolled
    for j in range(n):
        out = jnp.where(row == j, prev_rows[SUBLANES - n + j:SUBLANES - n + j + 1, :], out)
    return out


def _norm_matmul_kernel(x_ref, g_ref, w_ref, o_ref, h_ref):
    @pl.when(pl.program_id(1) == 0)
    def _():
        h_ref[...] = _rms(x_ref[...], g_ref[...]).astype(BF16)

    o_ref[...] = _nn(h_ref[...], w_ref[...]).astype(o_ref.dtype)


def _norm_matmul(x, gain, w_bf16, out_dtype, tm, tn):
    n, d = x.shape
    m = w_bf16.shape[1]
    return pl.pallas_call(
        _norm_matmul_kernel,
        grid=(n // tm, m // tn),
        in_specs=[
            pl.BlockSpec((tm, d), lambda i, j: (i, 0)),
            pl.BlockSpec((1, d), lambda i, j: (0, 0)),
            pl.BlockSpec((d, tn), lambda i, j: (0, j)),
        ],
        out_specs=pl.BlockSpec((tm, tn), lambda i, j: (i, j)),
        out_shape=jax.ShapeDtypeStruct((n, m), out_dtype),
        scratch_shapes=[pltpu.VMEM((tm, d), BF16)],
        compiler_params=_cparams(("arbitrary", "arbitrary")),
        name="norm_matmul",
    )(x, gain.reshape(1, d), w_bf16)


def _pool_kernel(xa_ref, w_ref, scale_ref, o_ref, carry_ref):
    t_tile = xa_ref.shape[1]
    s_idx = pl.program_id(1)

    @pl.when(s_idx == 0)
    def _():
        carry_ref[...] = jnp.zeros_like(carry_ref)

    a = xa_ref[0].astype(F32)
    ext = jnp.concatenate([carry_ref[...], a], axis=0)
    carry_ref[...] = a[t_tile - POOL_HALO:, :]
    pos = s_idx * t_tile + _iota((t_tile, 1), 0)
    count = (pos + 1).astype(F32)
    outs = []
    for gi, win in enumerate(POOL_WINDOWS):
        col = slice(gi * POOL_GROUP, (gi + 1) * POOL_GROUP)
        s = ext[:, col]
        span = 1
        while span < win:
            s = s + pltpu.roll(s, span, axis=0)
            span *= 2
        mean = s[POOL_HALO:, :] / jnp.minimum(count, float(win))
        diff = (mean - a[:, col]).astype(BF16)
        outs.append(_nn(diff, w_ref[gi]))
    y = jnp.concatenate(outs, axis=1) * scale_ref[...]
    o_ref[0] = y.astype(o_ref.dtype)


def _pool_mixer(xa, pool_w_bf16, pool_scale, t_tile):
    b, s, _ = xa.shape
    return pl.pallas_call(
        _pool_kernel,
        grid=(b, s // t_tile),
        in_specs=[
            pl.BlockSpec((1, t_tile, MIX_W), lambda i, j: (i, j, 0)),
            pl.BlockSpec((len(POOL_WINDOWS), POOL_GROUP, POOL_GROUP), lambda i, j: (0, 0, 0)),
            pl.BlockSpec((1, MIX_W), lambda i, j: (0, 0)),
        ],
        out_specs=pl.BlockSpec((1, t_tile, MIX_W), lambda i, j: (i, j, 0)),
        out_shape=jax.ShapeDtypeStruct((b, s, MIX_W), BF16),
        scratch_shapes=[pltpu.VMEM((POOL_HALO, MIX_W), F32)],
        compiler_params=_cparams(("arbitrary", "arbitrary")),
        name="pool_mixer",
    )(xa, pool_w_bf16, pool_scale.reshape(1, MIX_W))


RWKV_PREC_SCORE = 1
RWKV_PREC_SOLVE = 1
RWKV_PREC_STATE = 1
RWKV_SEQS = 4


def _rwkv_kernel(xb_ref, mu_ref, w0_ref, wla_ref, a0_ref, g2_ref, kk_ref, ka_ref, rk_ref,
                 lnw_ref, lnb_ref, hsum_ref, tril_ref, o_ref, carry_ref, state_ref):
    t_tile = xb_ref.shape[1]
    c = RWKV_CHUNK
    hd = RWKV_HEAD

    @pl.when(pl.program_id(1) == 0)
    def _():
        carry_ref[...] = jnp.zeros_like(carry_ref)
        state_ref[...] = jnp.zeros_like(state_ref)

    lane = _iota((1, LANES), 1)
    head0 = lane < hd
    col_s = jnp.bitwise_and(_iota((c, LANES), 1), hd - 1)
    row_t = _iota((c, LANES), 0)
    strict = row_t > col_s
    incl = row_t >= col_s
    bd_mask = (_iota((LANES, LANES), 0) < hd) == (_iota((LANES, LANES), 1) < hd)
    sys_w = 2 * LANES
    lane_head = _iota((c, sys_w), 1) // hd

    def block_diag(m):
        return jnp.concatenate([jnp.where(lane_head == j, m, 0.0) for j in range(sys_w // hd)], axis=0)

    def one_chunk(x, prev_rows, s_mats):
        xprev = _shift_rows(x, prev_rows, 1)
        xs = x + mu_ref[...] * (xprev - x)
        r = xs[:, 0:MIX_W]
        k = xs[:, MIX_W:2 * MIX_W]
        v = xs[:, 2 * MIX_W:3 * MIX_W]
        slab = xs[:, 3 * MIX_W:3 * MIX_W + LANES]
        lg = xs[:, 3 * MIX_W + LANES:]
        slab = jnp.where(head0, jnp.tanh(slab), slab)
        low = _nn(slab.astype(BF16), wla_ref[...])
        wl = w0_ref[...] + low[:, :MIX_W]
        neg = -wl
        softplus = jnp.maximum(neg, 0.0) + jnp.log(1.0 + jnp.exp(-jnp.abs(neg)))
        ld = -jnp.exp(-softplus - 0.5)
        eta = _sigmoid(a0_ref[...] + low[:, MIX_W:])
        gate = _nn(_sigmoid(lg).astype(BF16), g2_ref[...])
        kkv = k * kk_ref[...]
        k2 = k * (1.0 + (eta - 1.0) * ka_ref[...])
        nrm2, rk_sum = _head_sums([kkv * kkv, r * k2 * rk_ref[...]], hsum_ref[...])
        kkn = kkv / jnp.maximum(jnp.sqrt(nrm2), 1e-12)
        cum = _mm_exact_rhs_left(tril_ref[...], ld)
        tot = cum[c - 1:c, :]
        e_neg = jnp.exp(-cum)
        rt = r * jnp.exp(cum)
        at = -kkn * jnp.exp(cum - ld)
        b_vec = kkn * eta
        bt = b_vec * e_neg
        kt = k2 * e_neg
        e_end = jnp.exp(tot - cum)
        b_end = b_vec * e_end
        k_end = k2 * e_end
        p_tot = jnp.exp(tot)

        pieces = []
        for p in range(MIX_W // LANES):
            sl = slice(p * LANES, (p + 1) * LANES)
            at_p, rt_p, bt_p, kt_p = at[:, sl], rt[:, sl], bt[:, sl], kt[:, sl]
            lhs = jnp.concatenate([at_p, rt_p], axis=0)
            rhs = jnp.concatenate([jnp.where(head0, bt_p, 0.0), jnp.where(head0, 0.0, bt_p),
                                   jnp.where(head0, kt_p, 0.0), jnp.where(head0, 0.0, kt_p)], axis=0)
            sc = _mm(_nt, lhs, rhs, RWKV_PREC_SCORE)
            a_ab = jnp.where(strict, sc[0:c, 0:LANES], 0.0)
            a_ak = jnp.where(strict, sc[0:c, LANES:], 0.0)
            a_rb = jnp.where(incl, sc[c:, 0:LANES], 0.0)
            a_rk = jnp.where(incl, sc[c:, LANES:], 0.0)
            v_p = v[:, sl]
            v_bd = jnp.concatenate([jnp.where(head0, v_p, 0.0), jnp.where(head0, 0.0, v_p)], axis=0)
            s_mat = s_mats[p]
            w_p = (_mm(_nt, at_p, s_mat, RWKV_PREC_STATE) + _mm(_nn, a_ak, v_bd, RWKV_PREC_STATE))
            pieces.append((a_ab, w_p, a_rb, a_rk, v_bd, rt_p, s_mat, v_p))
            yield

        nsteps = c.bit_length() - 1
        nsys = MIX_W // sys_w
        pws = [jnp.concatenate([pieces[2 * q][0], pieces[2 * q + 1][0]], axis=1) for q in range(nsys)]
        sols = [jnp.concatenate([pieces[2 * q][1], pieces[2 * q + 1][1]], axis=1) for q in range(nsys)]
        for i in range(nsteps):
            for q in range(nsys):
                sols[q] = sols[q] + _mm(_nn, pws[q], block_diag(sols[q]), RWKV_PREC_SOLVE)
                if i + 1 < nsteps:
                    pws[q] = _mm(_nn, pws[q], block_diag(pws[q]), RWKV_PREC_SOLVE)
            yield
        u_pairs = [sols[p // 2][:, (p % 2) * LANES:(p % 2 + 1) * LANES] for p in range(MIX_W // LANES)]

        ys = []
        new_states = []
        for p in range(MIX_W // LANES):
            sl = slice(p * LANES, (p + 1) * LANES)
            a_ab, w_p, a_rb, a_rk, v_bd, rt_p, s_mat, v_p = pieces[p]
            u_p = u_pairs[p]
            u_bd = jnp.concatenate([jnp.where(head0, u_p, 0.0), jnp.where(head0, 0.0, u_p)], axis=0)
            y_p = (_mm(_nt, rt_p, s_mat, RWKV_PREC_STATE)
                   + _mm(_nn, jnp.concatenate([a_rb, a_rk], axis=1),
                         jnp.concatenate([u_bd, v_bd], axis=0), RWKV_PREC_STATE))
            ys.append(y_p)
            upd = _mm(_tn, jnp.concatenate([u_p, v_p], axis=0),
                      jnp.concatenate([b_end[:, sl], k_end[:, sl]], axis=0), RWKV_PREC_STATE)
            new_states.append(jnp.where(bd_mask, s_mat * p_tot[:, sl] + upd, 0.0))
            yield
        y = jnp.concatenate(ys, axis=1)

        inv_hd = 1.0 / hd
        mean = _head_sums([y], hsum_ref[...])[0] * inv_hd
        yield
        yc = y - mean
        var = _head_sums([yc * yc], hsum_ref[...])[0] * inv_hd
        yn = yc * lax.rsqrt(var + RWKV_LN_EPS) * lnw_ref[...] + lnb_ref[...]
        bonus = rk_sum * v
        return ((yn + bonus) * gate).astype(o_ref.dtype), new_states

    def chunk(ci, carry):
        r0 = pl.multiple_of(ci * c, c)
        seqs = range(xb_ref.shape[0])
        npair = MIX_W // LANES
        xs_in = [xb_ref[g, pl.ds(r0, c), :] for g in seqs]
        prevs = [carry_ref[g] for g in seqs]
        states = [[state_ref[g, p] for p in range(npair)] for g in seqs]
        results = _round_robin([one_chunk(xs_in[g], prevs[g], states[g]) for g in seqs])
        for g in seqs:
            out, new_states = results[g]
            carry_ref[g] = xs_in[g][c - SUBLANES:, :]
            for p in range(npair):
                state_ref[g, p] = new_states[p]
            o_ref[g, pl.ds(r0, c), :] = out
        return carry

    lax.fori_loop(0, t_tile // c, chunk, 0)


def _mm_exact_rhs_left(l_bf16, a):
    acc = None
    for part in _split(a, 3):
        t = _nn(l_bf16, part)
        acc = t if acc is None else acc + t
    return acc


def _rwkv_mixer(xb, mu, w0, w2, a0, a2, g2, k_k, k_a, r_k, ln_w, ln_b, t_tile):
    b, s, _ = xb.shape
    c = RWKV_CHUNK
    rank = w2.shape[0]
    wla = jnp.zeros((LANES, 2 * MIX_W), F32)
    wla = wla.at[:rank, :MIX_W].set(w2).at[rank:, MIX_W:].set(a2).astype(BF16)
    gs = RWKV_SEQS if b % RWKV_SEQS == 0 else 1
    hsum = jnp.kron(jnp.eye(LANES // RWKV_HEAD, dtype=F32),
                    jnp.ones((RWKV_HEAD, RWKV_HEAD), F32)).astype(BF16)
    tril = jnp.tril(jnp.ones((c, c), F32)).astype(BF16)
    row = lambda a: a.reshape(1, -1).astype(F32)
    full = lambda shape: pl.BlockSpec(shape, lambda i, j: (0,) * len(shape))
    return pl.pallas_call(
        _rwkv_kernel,
        grid=(b // gs, s // t_tile),
        in_specs=[
            pl.BlockSpec((gs, t_tile, RWKV_IN), lambda i, j: (i, j, 0)),
            full((1, RWKV_IN)), full((1, MIX_W)), full((LANES, 2 * MIX_W)), full((1, MIX_W)),
            full((LANES, MIX_W)), full((1, MIX_W)), full((1, MIX_W)), full((1, MIX_W)),
            full((1, MIX_W)), full((1, MIX_W)), full((LANES, LANES)), full((c, c)),
        ],
        out_specs=pl.BlockSpec((gs, t_tile, MIX_W), lambda i, j: (i, j, 0)),
        out_shape=jax.ShapeDtypeStruct((b, s, MIX_W), BF16),
        scratch_shapes=[pltpu.VMEM((gs, SUBLANES, RWKV_IN), F32),
                        pltpu.VMEM((gs, MIX_W // LANES, LANES, LANES), F32)],
        compiler_params=_cparams(("arbitrary", "arbitrary")),
        name="rwkv7_mixer",
    )(xb, row(mu), row(w0), wla, row(a0), g2.astype(BF16), row(k_k), row(k_a), row(r_k),
      row(ln_w), row(ln_b), hsum, tril)


def _sg_kernel(xc_ref, lnw_ref, lnb_ref, ws_ref, bs_ref, o_ref):
    t_tile = xc_ref.shape[0]
    z = _gelu(xc_ref[...].astype(F32))
    u = z[:, :MIX_W]
    v = z[:, MIX_W:]
    mean = jnp.mean(v, axis=-1, keepdims=True)
    vc = v - mean
    var = jnp.mean(vc * vc, axis=-1, keepdims=True)
    vn = (vc * lax.rsqrt(var + SG_LN_EPS) * lnw_ref[...] + lnb_ref[...]).astype(BF16)
    causal = _iota((SG_CHUNK, SG_CHUNK), 0) >= _iota((SG_CHUNK, SG_CHUNK), 1)
    w_m = [jnp.where(causal, ws_ref[g], 0.0).astype(BF16) for g in range(SG_GROUPS)]
    for n in range(t_tile // SG_CHUNK):
        rows = slice(n * SG_CHUNK, (n + 1) * SG_CHUNK)
        mixed = jnp.concatenate(
            [_nn(w_m[g], vn[rows, g * LANES:(g + 1) * LANES]) for g in range(SG_GROUPS)], axis=1)
        o_ref[rows, :] = (u[rows, :] * (mixed + bs_ref[...])).astype(o_ref.dtype)


def _sg_mixer(xc, ln_w, ln_b, w_s, b_s, t_tile):
    n = xc.shape[0]
    b_full = jnp.repeat(jnp.swapaxes(b_s, 0, 1), MIX_W // SG_GROUPS, axis=1)
    return pl.pallas_call(
        _sg_kernel,
        grid=(n // t_tile,),
        in_specs=[
            pl.BlockSpec((t_tile, 2 * MIX_W), lambda i: (i, 0)),
            pl.BlockSpec((1, MIX_W), lambda i: (0, 0)),
            pl.BlockSpec((1, MIX_W), lambda i: (0, 0)),
            pl.BlockSpec((SG_GROUPS, SG_CHUNK, SG_CHUNK), lambda i: (0, 0, 0)),
            pl.BlockSpec((SG_CHUNK, MIX_W), lambda i: (0, 0)),
        ],
        out_specs=pl.BlockSpec((t_tile, MIX_W), lambda i: (i, 0)),
        out_shape=jax.ShapeDtypeStruct((n, MIX_W), BF16),
        compiler_params=_cparams(("arbitrary",)),
        name="spatial_gating_mixer",
    )(xc, ln_w.reshape(1, MIX_W), ln_b.reshape(1, MIX_W), w_s, b_full)


HGRN_PREC = 1
HGRN_SEQS = 4


def _hgrn_levels(c):
    return [c >> (i + 1) for i in range(c.bit_length() - 1)]


def _block_mid_rows(a, half):
    rows, width = a.shape
    blk = 2 * half
    sub_row = _iota((SUBLANES, 1), 0)
    tiles = []
    for j in range(rows // SUBLANES):
        base = j * SUBLANES
        if blk >= SUBLANES:
            src = (base // blk) * blk + half - 1
            tiles.append(jnp.broadcast_to(a[src:src + 1, :], (SUBLANES, width)))
        else:
            tile = None
            for i in reversed(range(SUBLANES // blk)):
                src = base + i * blk + half - 1
                piece = jnp.broadcast_to(a[src:src + 1, :], (SUBLANES, width))
                tile = piece if tile is None else jnp.where(sub_row < (i + 1) * blk, piece, tile)
            tiles.append(tile)
    return jnp.concatenate(tiles, axis=0)


def _hgrn_kernel(xd_ref, lb_ref, nw_ref, cmat_ref, o_ref, state_ref, *, layer):
    t_tile = xd_ref.shape[1]
    c = HGRN_CHUNK
    levels = _hgrn_levels(c)

    @pl.when(pl.program_id(1) == 0)
    def _():
        state_ref[...] = jnp.zeros_like(state_ref)

    lb_all = lb_ref[...]
    e_all = jnp.exp(lb_all - jnp.max(lb_all, axis=0, keepdims=True))
    probs = e_all / jnp.sum(e_all, axis=0, keepdims=True)
    lb = jnp.sum(probs[0:layer + 1, :], axis=0, keepdims=True) - probs[0:1, :]
    row_c = _iota((c, 1), 0)
    row_cc = _iota((c, c), 0)
    col_cc = _iota((c, c), 1)

    def one_chunk(x, s_mats):
        qr = x[:, 0:MIX_W]
        q = qr * _sigmoid(qr)
        sig = _sigmoid(x[:, MIX_W:2 * MIX_W])
        v = x[:, 2 * MIX_W:]
        lg = jnp.log(jnp.maximum(lb + (1.0 - lb) * sig, GATE_FLOOR))
        key = (1.0 - lb) * (1.0 - sig)
        cum = _mm_exact_rhs_left(cmat_ref[...], lg)
        tot = cum[c - 1:c, :]
        q_chunk = q * jnp.exp(cum)
        k_end = key * jnp.exp(tot - cum)
        yield
        q_lv, k_lv = [], []
        for li, half in enumerate(levels):
            dec = jnp.exp(-jnp.abs(cum - _block_mid_rows(cum, half)))
            second = jnp.bitwise_and(row_c, half) != 0
            q_lv.append(jnp.where(second, q * dec, 0.0))
            k_lv.append(jnp.where(second, 0.0, key * dec))
        qk = q * key
        outs, new_states, all_scores = [], [], []
        for h in range(HGRN_HEADS):
            sl = slice(h * LANES, (h + 1) * LANES)
            scores = None
            for li, half in enumerate(levels):
                sc = _mm(_nt, q_lv[li][:, sl], k_lv[li][:, sl], HGRN_PREC)
                if 2 * half < c:
                    blk = 2 * half
                    sc = jnp.where((row_cc // blk) == (col_cc // blk), sc, 0.0)
                scores = sc if scores is None else scores + sc
            all_scores.append(scores)
            yield
        for h in range(HGRN_HEADS):
            sl = slice(h * LANES, (h + 1) * LANES)
            s_mat = s_mats[h]
            v_h = v[:, sl]
            o_h = (_mm(_nt, q_chunk[:, sl], s_mat, HGRN_PREC) + _mm(_nn, all_scores[h], v_h, HGRN_PREC)
                   + jnp.sum(qk[:, sl], axis=-1, keepdims=True) * v_h)
            ms = jnp.mean(o_h * o_h, axis=-1, keepdims=True)
            outs.append(o_h * lax.rsqrt(ms + NORM_EPS))
            upd = _mm(_tn, v_h, k_end[:, sl], HGRN_PREC)
            new_states.append(s_mat * jnp.exp(tot[:, sl]) + upd)
            yield
        o = jnp.concatenate(outs, axis=1) * nw_ref[...]
        return o.astype(o_ref.dtype), new_states

    def chunk(ci, carry):
        r0 = pl.multiple_of(ci * c, c)
        seqs = range(xd_ref.shape[0])
        xs_in = [xd_ref[g, pl.ds(r0, c), :] for g in seqs]
        states = [[state_ref[g, h] for h in range(HGRN_HEADS)] for g in seqs]
        results = _round_robin([one_chunk(xs_in[g], states[g]) for g in seqs])
        for g in seqs:
            out, new_states = results[g]
            for h in range(HGRN_HEADS):
                state_ref[g, h] = new_states[h]
            o_ref[g, pl.ds(r0, c), :] = out
        return carry

    lax.fori_loop(0, t_tile // c, chunk, 0)


def _hgrn_mixer(xd, lb_all, layer, norm_w, t_tile):
    b, s, _ = xd.shape
    c = HGRN_CHUNK
    depth = lb_all.shape[0]
    gs = HGRN_SEQS if b % HGRN_SEQS == 0 else 1
    cmat = jnp.tril(jnp.ones((c, c), F32)).astype(BF16)
    full = lambda shape: pl.BlockSpec(shape, lambda i, j: (0,) * len(shape))
    return pl.pallas_call(
        functools.partial(_hgrn_kernel, layer=layer),
        grid=(b // gs, s // t_tile),
        in_specs=[
            pl.BlockSpec((gs, t_tile, COL_D), lambda i, j: (i, j, 0)),
            full((depth, MIX_W)), full((1, MIX_W)), full(cmat.shape),
        ],
        out_specs=pl.BlockSpec((gs, t_tile, MIX_W), lambda i, j: (i, j, 0)),
        out_shape=jax.ShapeDtypeStruct((b, s, MIX_W), BF16),
        scratch_shapes=[pltpu.VMEM((gs, HGRN_HEADS, HGRN_EXPAND, HGRN_EXPAND), F32)],
        compiler_params=_cparams(("arbitrary", "arbitrary")),
        name="hgrn2_mixer",
    )(xd, lb_all.astype(F32), norm_w.reshape(1, MIX_W), cmat)


def _merge_kernel(ya_ref, yb_ref, yc_ref, yd_ref, g_ref, x_ref, wbr_ref, wout_ref, o_ref):
    merged = None
    for kk, y_ref in enumerate((ya_ref, yb_ref, yc_ref, yd_ref)):
        gate = _sigmoid(g_ref[:, kk * D_MODEL:(kk + 1) * D_MODEL].astype(F32))
        term = gate * _nn(y_ref[...], wbr_ref[kk])
        merged = term if merged is None else merged + term
    o_ref[...] = x_ref[...] + _nn(merged.astype(BF16), wout_ref[...])


def _merge(ya, yb, yc, yd, g, x, wbr_bf16, wout_bf16, tm):
    n = x.shape[0]
    ytile = pl.BlockSpec((tm, MIX_W), lambda i: (i, 0))
    return pl.pallas_call(
        _merge_kernel,
        grid=(n // tm,),
        in_specs=[
            ytile, ytile, ytile, ytile,
            pl.BlockSpec((tm, 4 * D_MODEL), lambda i: (i, 0)),
            pl.BlockSpec((tm, D_MODEL), lambda i: (i, 0)),
            pl.BlockSpec((4, MIX_W, D_MODEL), lambda i: (0, 0, 0)),
            pl.BlockSpec((D_MODEL, D_MODEL), lambda i: (0, 0)),
        ],
        out_specs=pl.BlockSpec((tm, D_MODEL), lambda i: (i, 0)),
        out_shape=jax.ShapeDtypeStruct((n, D_MODEL), F32),
        compiler_params=_cparams(("arbitrary",)),
        name="merge_out",
    )(ya, yb, yc, yd, g, x, wbr_bf16, wout_bf16)


def _ffn_kernel(x_ref, nf_ref, wg_ref, wv_ref, cwg_ref, cwv_ref, cbg_ref, cbv_ref, wd_ref,
                p_ref, np_ref, pproj_ref, pgate_ref, nfin_ref, o_ref, acc_ref, cg_ref, cv_ref, h_ref,
                *, final_norm):
    j = pl.program_id(2)
    tm = x_ref.shape[1]

    @pl.when(j == 0)
    def _():
        h_ref[...] = _rms(x_ref[0], nf_ref[...]).astype(BF16)
        acc_ref[...] = jnp.zeros_like(acc_ref)

    @pl.when((j == 0) & (pl.program_id(1) == 0))
    def _():
        cg_ref[...] = jnp.zeros_like(cg_ref)
        cv_ref[...] = jnp.zeros_like(cv_ref)

    h = h_ref[...]

    def conv(u, carry_ref, cw_ref, cb_ref):
        prev = carry_ref[j]
        u1 = _shift_rows(u, prev, 1)
        u2 = _shift_rows(u, prev, 2)
        carry_ref[j] = u[tm - SUBLANES:, :]
        cw = cw_ref[0]
        return cb_ref[0] + cw[2:3, :] * u + cw[1:2, :] * u1 + cw[0:1, :] * u2

    ug = conv(_nn(h, wg_ref[0]), cg_ref, cwg_ref, cbg_ref)
    uv = conv(_nn(h, wv_ref[0]), cv_ref, cwv_ref, cbv_ref)
    act = (_gelu(ug) * uv).astype(BF16)
    acc_ref[...] += _nn(act, wd_ref[0])

    @pl.when(j == pl.num_programs(2) - 1)
    def _():
        x2 = x_ref[0] + acc_ref[...]
        hp = _rms(x2, np_ref[...]).astype(BF16)
        gate = _sigmoid(_nn(hp, pgate_ref[...]))
        pp = _nn(p_ref[0].astype(BF16), pproj_ref[...])
        x3 = x2 + pp * gate
        if final_norm:
            x3 = _rms(x3, nfin_ref[...])
        o_ref[0] = x3


def _ffn_ple(x1, norm_ffn, wup_bf16, conv_w, conv_b, wdown_bf16, p_l, norm_ple, pproj_bf16,
             pgate_bf16, norm_final, final_norm, tm):
    b, s, d = x1.shape
    nb, blk = FFN_NB, FFN_BLOCK
    wg = wup_bf16[:, :D_FF].reshape(d, nb, blk).transpose(1, 0, 2)
    wv = wup_bf16[:, D_FF:].reshape(d, nb, blk).transpose(1, 0, 2)
    cwg = conv_w[:, :D_FF].reshape(3, nb, blk).transpose(1, 0, 2)
    cwv = conv_w[:, D_FF:].reshape(3, nb, blk).transpose(1, 0, 2)
    cbg = conv_b[:D_FF].reshape(nb, 1, blk)
    cbv = conv_b[D_FF:].reshape(nb, 1, blk)
    wd = wdown_bf16.reshape(nb, blk, d)
    row = lambda a: a.reshape(1, -1)
    const2 = lambda shape: pl.BlockSpec(shape, lambda i, t, j: (0, 0))
    blk3 = lambda shape: pl.BlockSpec(shape, lambda i, t, j: (j, 0, 0))
    return pl.pallas_call(
        functools.partial(_ffn_kernel, final_norm=final_norm),
        grid=(b, s // tm, nb),
        in_specs=[
            pl.BlockSpec((1, tm, d), lambda i, t, j: (i, t, 0)),
            const2((1, d)),
            blk3((1, d, blk)), blk3((1, d, blk)),
            blk3((1, 3, blk)), blk3((1, 3, blk)),
            blk3((1, 1, blk)), blk3((1, 1, blk)),
            blk3((1, blk, d)),
            pl.BlockSpec((1, tm, PLE_DIM), lambda i, t, j: (i, t, 0)),
            const2((1, d)),
            const2((PLE_DIM, d)),
            const2((d, d)),
            const2((1, d)),
        ],
        out_specs=pl.BlockSpec((1, tm, d), lambda i, t, j: (i, t, 0)),
        out_shape=jax.ShapeDtypeStruct((b, s, d), F32),
        scratch_shapes=[pltpu.VMEM((tm, d), F32),
                        pltpu.VMEM((nb, SUBLANES, blk), F32),
                        pltpu.VMEM((nb, SUBLANES, blk), F32),
                        pltpu.VMEM((tm, d), BF16)],
        compiler_params=_cparams(("arbitrary", "arbitrary", "arbitrary")),
        name="ffn_ple",
    )(x1, row(norm_ffn), wg, wv, cwg, cwv, cbg, cbv, wd, p_l, row(norm_ple), pproj_bf16,
      pgate_bf16, row(norm_final))


def _pick_tile(total, want):
    t = min(total, want)
    while total % t:
        t //= 2
    return t


def kernel(x, p, norm_mix, w_in, pool_w, pool_scale, rwkv_mu, rwkv_w0, rwkv_w2, rwkv_a0, rwkv_a2, rwkv_g2, rwkv_kk, rwkv_ka, rwkv_rk, rwkv_ln_w, rwkv_ln_b, sg_ln_w, sg_ln_b, sg_w, sg_b, hgrn_lb, hgrn_norm, w_branch, w_out, norm_ffn, ffn_up, ffn_conv, ffn_conv_b, ffn_down, norm_ple, ple_proj, ple_gate, norm_final):
    b, s, d = x.shape
    depth = w_in.shape[0]
    n = b * s
    tm = _pick_tile(n, 1024)
    ts = _pick_tile(s, 512)
    o0, o1, o2, o3 = COL_A, COL_A + COL_B, COL_A + COL_B + COL_C, COL_A + COL_B + COL_C + COL_D
    for l in range(depth):
        w_l = w_in[l].astype(BF16)
        x2d = x.reshape(n, d)
        gain = norm_mix[l]
        xa = _norm_matmul(x2d, gain, w_l[:, :o0], BF16, tm, COL_A)
        xb = _norm_matmul(x2d, gain, w_l[:, o0:o1], F32, tm, COL_B)
        xc = _norm_matmul(x2d, gain, w_l[:, o1:o2], BF16, tm, COL_C)
        xd = _norm_matmul(x2d, gain, w_l[:, o2:o3], F32, tm, COL_D)
        g = _norm_matmul(x2d, gain, w_l[:, o3:], BF16, tm, 1024)
        ya = _pool_mixer(xa.reshape(b, s, COL_A), pool_w[l].astype(BF16), pool_scale[l], ts)
        yb = _rwkv_mixer(xb.reshape(b, s, COL_B), rwkv_mu[l], rwkv_w0[l], rwkv_w2[l], rwkv_a0[l],
                         rwkv_a2[l], rwkv_g2[l], rwkv_kk[l], rwkv_ka[l], rwkv_rk[l].reshape(-1),
                         rwkv_ln_w[l], rwkv_ln_b[l], _pick_tile(s, 256))
        yc = _sg_mixer(xc, sg_ln_w[l], sg_ln_b[l], sg_w[l], sg_b[l], _pick_tile(n, 512))
        yd = _hgrn_mixer(xd.reshape(b, s, COL_D), hgrn_lb, l, hgrn_norm[l], _pick_tile(s, 256))
        x1 = _merge(ya.reshape(n, MIX_W), yb.reshape(n, MIX_W), yc, yd.reshape(n, MIX_W), g, x2d,
                    w_branch[l].astype(BF16), w_out[l].astype(BF16), _pick_tile(n, 512))
        x = _ffn_ple(x1.reshape(b, s, d), norm_ffn[l], ffn_up[l].astype(BF16), ffn_conv[l],
                     ffn_conv_b[l], ffn_down[l].astype(BF16), p[l], norm_ple[l],
                     ple_proj[l].astype(BF16), ple_gate[l].astype(BF16), norm_final,
                     l == depth - 1, _pick_tile(s, 1024))
    return x
```

```python
import functools

import jax
import jax.numpy as jnp
from jax import lax
from jax.experimental import pallas as pl
from jax.experimental.pallas import tpu as pltpu

F32 = jnp.float32
BF16 = jnp.bfloat16

D_MODEL = 1024
MIX_W = 512
PLE_DIM = 256
POOL_WINDOWS = (2, 4, 8, 16)
POOL_GROUP = 128
POOL_HALO = 16
RWKV_HEAD = 64
RWKV_IN = 1792
RWKV_LN_EPS = 64e-5
SG_CHUNK = 128
SG_GROUPS = 4
SG_LN_EPS = 1e-5
HGRN_HEADS = 4
HGRN_EXPAND = 128
GATE_FLOOR = 1e-30
D_FF = 2816
NORM_EPS = 1e-6
COL_A, COL_B, COL_C, COL_D = 512, 1792, 1024, 1536

LANES = 128
SUBLANES = 8
VMEM_LIMIT_BYTES = 56 * 1024 * 1024

RWKV_CHUNK = 64
HGRN_CHUNK = 64

FFN_BLOCK = 256
FFN_NB = D_FF // FFN_BLOCK


def _cparams(sem):
    return pltpu.CompilerParams(dimension_semantics=sem, vmem_limit_bytes=VMEM_LIMIT_BYTES)


def _nn(a, b):
    return lax.dot_general(a, b, (((1,), (0,)), ((), ())), preferred_element_type=F32)


def _nt(a, b):
    return lax.dot_general(a, b, (((1,), (1,)), ((), ())), preferred_element_type=F32)


def _tn(a, b):
    return lax.dot_general(a, b, (((0,), (0,)), ((), ())), preferred_element_type=F32)


def _split(a, terms):
    out = []
    rem = a
    for _ in range(terms - 1):
        hi = rem.astype(BF16)
        out.append(hi)
        rem = rem - hi.astype(F32)
    out.append(rem.astype(BF16))
    return out


def _mm(dot, a, b, prec):
    if prec == 1:
        return dot(a.astype(BF16), b.astype(BF16))
    a_hi, a_lo = _split(a, 2)
    b_hi, b_lo = _split(b, 2)
    return dot(a_hi, b_hi) + (dot(a_hi, b_lo) + dot(a_lo, b_hi))


def _mm_exact_rhs(a, b_bf16, terms):
    acc = None
    for part in _split(a, terms):
        t = _nn(part, b_bf16)
        acc = t if acc is None else acc + t
    return acc


def _head_sums(arrays, ones_pair_bf16):
    c = arrays[0].shape[0]
    slabs = [a[:, p * LANES:(p + 1) * LANES] for a in arrays for p in range(MIX_W // LANES)]
    s = _mm_exact_rhs(jnp.concatenate(slabs, axis=0), ones_pair_bf16, 2)
    per = MIX_W // LANES
    return [jnp.concatenate([s[(i * per + p) * c:(i * per + p + 1) * c, :] for p in range(per)], axis=1)
            for i in range(len(arrays))]


def _rms(x, gain):
    return x * lax.rsqrt(jnp.mean(x * x, axis=-1, keepdims=True) + NORM_EPS) * gain


def _sigmoid(x):
    return 1.0 / (1.0 + jnp.exp(-x))


def _gelu(x):
    return 0.5 * x * (1.0 + jnp.tanh(0.7978845608028654 * (x + 0.044715 * (x * x * x))))


def _iota(shape, dim):
    return lax.broadcasted_iota(jnp.int32, shape, dim)


def _round_robin(gens):
    results = [None] * len(gens)
    live = list(range(len(gens)))
    while live:
        for idx in list(live):
            try:
                next(gens[idx])
            except StopIteration as stop:
                results[idx] = stop.value
                live.remove(idx)
    return results


def _shift_rows(x, prev_rows, n):
    rolled = pltpu.roll(x, n, axis=0)
    row = _iota(x.shape, 0)
    out = rolled
    for j in range(n):
        out = jnp.where(row == j, prev_rows[SUBLANES - n + j:SUBLANES - n + j + 1, :], out)
    return out


IN_PROJ_CHUNK = 512


def _in_proj_kernel(x_ref, g_ref, w_ref, *o_refs):
    h = _rms(x_ref[...], g_ref[...]).astype(BF16)
    col = 0
    for o_ref in o_refs:
        width = o_ref.shape[1]
        for c0 in range(0, width, IN_PROJ_CHUNK):
            c1 = min(c0 + IN_PROJ_CHUNK, width)
            o_ref[:, c0:c1] = _nn(h, w_ref[:, col + c0:col + c1]).astype(o_ref.dtype)
        col += width


def _in_proj(x, gain, w_bf16, widths, dtypes, tm):
    n, d = x.shape
    m = w_bf16.shape[1]
    assert sum(widths) == m
    return pl.pallas_call(
        _in_proj_kernel,
        grid=(n // tm,),
        in_specs=[
            pl.BlockSpec((tm, d), lambda i: (i, 0)),
            pl.BlockSpec((1, d), lambda i: (0, 0), pipeline_mode=pl.Buffered(1)),
            pl.BlockSpec((d, m), lambda i: (0, 0), pipeline_mode=pl.Buffered(1)),
        ],
        out_specs=[pl.BlockSpec((tm, w), lambda i: (i, 0)) for w in widths],
        out_shape=[jax.ShapeDtypeStruct((n, w), dt) for w, dt in zip(widths, dtypes)],
        compiler_params=_cparams(("arbitrary",)),
        name="in_proj",
    )(x, gain.reshape(1, d), w_bf16)


def _pool_kernel(xa_ref, w_ref, scale_ref, o_ref, carry_ref):
    t_tile = xa_ref.shape[1]
    s_idx = pl.program_id(1)

    @pl.when(s_idx == 0)
    def _():
        carry_ref[...] = jnp.zeros_like(carry_ref)

    a = xa_ref[0].astype(F32)
    ext = jnp.concatenate([carry_ref[...], a], axis=0)
    carry_ref[...] = a[t_tile - POOL_HALO:, :]
    pos = s_idx * t_tile + _iota((t_tile, 1), 0)
    count = (pos + 1).astype(F32)
    outs = []
    for gi, win in enumerate(POOL_WINDOWS):
        col = slice(gi * POOL_GROUP, (gi + 1) * POOL_GROUP)
        s = ext[:, col]
        span = 1
        while span < win:
            s = s + pltpu.roll(s, span, axis=0)
            span *= 2
        mean = s[POOL_HALO:, :] / jnp.minimum(count, float(win))
        diff = (mean - a[:, col]).astype(BF16)
        outs.append(_nn(diff, w_ref[gi]))
    y = jnp.concatenate(outs, axis=1) * scale_ref[...]
    o_ref[0] = y.astype(o_ref.dtype)


def _pool_mixer(xa, pool_w_bf16, pool_scale, t_tile):
    b, s, _ = xa.shape
    return pl.pallas_call(
        _pool_kernel,
        grid=(b, s // t_tile),
        in_specs=[
            pl.BlockSpec((1, t_tile, MIX_W), lambda i, j: (i, j, 0)),
            pl.BlockSpec((len(POOL_WINDOWS), POOL_GROUP, POOL_GROUP), lambda i, j: (0, 0, 0)),
            pl.BlockSpec((1, MIX_W), lambda i, j: (0, 0)),
        ],
        out_specs=pl.BlockSpec((1, t_tile, MIX_W), lambda i, j: (i, j, 0)),
        out_shape=jax.ShapeDtypeStruct((b, s, MIX_W), BF16),
        scratch_shapes=[pltpu.VMEM((POOL_HALO, MIX_W), F32)],
        compiler_params=_cparams(("arbitrary", "arbitrary")),
        name="pool_mixer",
    )(xa, pool_w_bf16, pool_scale.reshape(1, MIX_W))


RWKV_PREC_SCORE = 1
RWKV_PREC_SOLVE = 1
RWKV_PREC_STATE = 1
RWKV_SEQS = 4


def _rwkv_kernel(xb_ref, mu_ref, w0_ref, wla_ref, a0_ref, g2_ref, kk_ref, ka_ref, rk_ref,
                 lnw_ref, lnb_ref, hsum_ref, tril_ref, o_ref, carry_ref, state_ref):
    t_tile = xb_ref.shape[1]
    c = RWKV_CHUNK
    hd = RWKV_HEAD

    @pl.when(pl.program_id(1) == 0)
    def _():
        carry_ref[...] = jnp.zeros_like(carry_ref)
        state_ref[...] = jnp.zeros_like(state_ref)

    lane = _iota((1, LANES), 1)
    head0 = lane < hd
    col_s = jnp.bitwise_and(_iota((c, LANES), 1), hd - 1)
    row_t = _iota((c, LANES), 0)
    strict = row_t > col_s
    incl = row_t >= col_s
    bd_mask = (_iota((LANES, LANES), 0) < hd) == (_iota((LANES, LANES), 1) < hd)
    sys_w = 2 * LANES
    lane_head = _iota((c, sys_w), 1) // hd

    def block_diag(m):
        return jnp.concatenate([jnp.where(lane_head == j, m, 0.0) for j in range(sys_w // hd)], axis=0)

    def one_chunk(x, prev_rows, s_mats):
        xprev = _shift_rows(x, prev_rows, 1)
        xs = x + mu_ref[...] * (xprev - x)
        r = xs[:, 0:MIX_W]
        k = xs[:, MIX_W:2 * MIX_W]
        v = xs[:, 2 * MIX_W:3 * MIX_W]
        slab = xs[:, 3 * MIX_W:3 * MIX_W + LANES]
        lg = xs[:, 3 * MIX_W + LANES:]
        slab = jnp.where(head0, jnp.tanh(slab), slab)
        low = _nn(slab.astype(BF16), wla_ref[...])
        wl = w0_ref[...] + low[:, :MIX_W]
        neg = -wl
        softplus = jnp.maximum(neg, 0.0) + jnp.log(1.0 + jnp.exp(-jnp.abs(neg)))
        ld = -jnp.exp(-softplus - 0.5)
        eta = _sigmoid(a0_ref[...] + low[:, MIX_W:])
        gate = _nn(_sigmoid(lg).astype(BF16), g2_ref[...])
        kkv = k * kk_ref[...]
        k2 = k * (1.0 + (eta - 1.0) * ka_ref[...])
        nrm2, rk_sum = _head_sums([kkv * kkv, r * k2 * rk_ref[...]], hsum_ref[...])
        kkn = kkv / jnp.maximum(jnp.sqrt(nrm2), 1e-12)
        cum = _mm_exact_rhs_left(tril_ref[...], ld)
        tot = cum[c - 1:c, :]
        e_neg = jnp.exp(-cum)
        rt = r * jnp.exp(cum)
        at = -kkn * jnp.exp(cum - ld)
        b_vec = kkn * eta
        bt = b_vec * e_neg
        kt = k2 * e_neg
        e_end = jnp.exp(tot - cum)
        b_end = b_vec * e_end
        k_end = k2 * e_end
        p_tot = jnp.exp(tot)

        pieces = []
        for p in range(MIX_W // LANES):
            sl = slice(p * LANES, (p + 1) * LANES)
            at_p, rt_p, bt_p, kt_p = at[:, sl], rt[:, sl], bt[:, sl], kt[:, sl]
            lhs = jnp.concatenate([at_p, rt_p], axis=0)
            rhs = jnp.concatenate([jnp.where(head0, bt_p, 0.0), jnp.where(head0, 0.0, bt_p),
                                   jnp.where(head0, kt_p, 0.0), jnp.where(head0, 0.0, kt_p)], axis=0)
            sc = _mm(_nt, lhs, rhs, RWKV_PREC_SCORE)
            a_ab = jnp.where(strict, sc[0:c, 0:LANES], 0.0)
            a_ak = jnp.where(strict, sc[0:c, LANES:], 0.0)
            a_rb = jnp.where(incl, sc[c:, 0:LANES], 0.0)
            a_rk = jnp.where(incl, sc[c:, LANES:], 0.0)
            v_p = v[:, sl]
            v_bd = jnp.concatenate([jnp.where(head0, v_p, 0.0), jnp.where(head0, 0.0, v_p)], axis=0)
            s_mat = s_mats[p]
            w_p = (_mm(_nt, at_p, s_mat, RWKV_PREC_STATE) + _mm(_nn, a_ak, v_bd, RWKV_PREC_STATE))
            pieces.append((a_ab, w_p, a_rb, a_rk, v_bd, rt_p, s_mat, v_p))
            yield

        nsteps = c.bit_length() - 1
        nsys = MIX_W // sys_w
        pws = [jnp.concatenate([pieces[2 * q][0], pieces[2 * q + 1][0]], axis=1) for q in range(nsys)]
        sols = [jnp.concatenate([pieces[2 * q][1], pieces[2 * q + 1][1]], axis=1) for q in range(nsys)]
        for i in range(nsteps):
            for q in range(nsys):
                sols[q] = sols[q] + _mm(_nn, pws[q], block_diag(sols[q]), RWKV_PREC_SOLVE)
                if i + 1 < nsteps:
                    pws[q] = _mm(_nn, pws[q], block_diag(pws[q]), RWKV_PREC_SOLVE)
            yield
        u_pairs = [sols[p // 2][:, (p % 2) * LANES:(p % 2 + 1) * LANES] for p in range(MIX_W // LANES)]

        ys = []
        new_states = []
        for p in range(MIX_W // LANES):
            sl = slice(p * LANES, (p + 1) * LANES)
            a_ab, w_p, a_rb, a_rk, v_bd, rt_p, s_mat, v_p = pieces[p]
            u_p = u_pairs[p]
            u_bd = jnp.concatenate([jnp.where(head0, u_p, 0.0), jnp.where(head0, 0.0, u_p)], axis=0)
            y_p = (_mm(_nt, rt_p, s_mat, RWKV_PREC_STATE)
                   + _mm(_nn, jnp.concatenate([a_rb, a_rk], axis=1),
                         jnp.concatenate([u_bd, v_bd], axis=0), RWKV_PREC_STATE))
            ys.append(y_p)
            upd = _mm(_tn, jnp.concatenate([u_p, v_p], axis=0),
                      jnp.concatenate([b_end[:, sl], k_end[:, sl]], axis=0), RWKV_PREC_STATE)
            new_states.append(jnp.where(bd_mask, s_mat * p_tot[:, sl] + upd, 0.0))
            yield
        y = jnp.concatenate(ys, axis=1)

        inv_hd = 1.0 / hd
        mean = _head_sums([y], hsum_ref[...])[0] * inv_hd
        yield
        yc = y - mean
        var = _head_sums([yc * yc], hsum_ref[...])[0] * inv_hd
        yn = yc * lax.rsqrt(var + RWKV_LN_EPS) * lnw_ref[...] + lnb_ref[...]
        bonus = rk_sum * v
        return ((yn + bonus) * gate).astype(o_ref.dtype), new_states

    def chunk(ci, carry):
        r0 = pl.multiple_of(ci * c, c)
        seqs = range(xb_ref.shape[0])
        npair = MIX_W // LANES
        xs_in = [xb_ref[g, pl.ds(r0, c), :] for g in seqs]
        prevs = [carry_ref[g] for g in seqs]
        states = [[state_ref[g, p] for p in range(npair)] for g in seqs]
        results = _round_robin([one_chunk(xs_in[g], prevs[g], states[g]) for g in seqs])
        for g in seqs:
            out, new_states = results[g]
            carry_ref[g] = xs_in[g][c - SUBLANES:, :]
            for p in range(npair):
                state_ref[g, p] = new_states[p]
            o_ref[g, pl.ds(r0, c), :] = out
        return carry

    lax.fori_loop(0, t_tile // c, chunk, 0)


def _mm_exact_rhs_left(l_bf16, a):
    acc = None
    for part in _split(a, 3):
        t = _nn(l_bf16, part)
        acc = t if acc is None else acc + t
    return acc


def _rwkv_mixer(xb, mu, w0, w2, a0, a2, g2, k_k, k_a, r_k, ln_w, ln_b, t_tile):
    b, s, _ = xb.shape
    c = RWKV_CHUNK
    rank = w2.shape[0]
    wla = jnp.zeros((LANES, 2 * MIX_W), F32)
    wla = wla.at[:rank, :MIX_W].set(w2).at[rank:, MIX_W:].set(a2).astype(BF16)
    gs = RWKV_SEQS if b % RWKV_SEQS == 0 else 1
    hsum = jnp.kron(jnp.eye(LANES // RWKV_HEAD, dtype=F32),
                    jnp.ones((RWKV_HEAD, RWKV_HEAD), F32)).astype(BF16)
    tril = jnp.tril(jnp.ones((c, c), F32)).astype(BF16)
    row = lambda a: a.reshape(1, -1).astype(F32)
    full = lambda shape: pl.BlockSpec(shape, lambda i, j: (0,) * len(shape))
    return pl.pallas_call(
        _rwkv_kernel,
        grid=(b // gs, s // t_tile),
        in_specs=[
            pl.BlockSpec((gs, t_tile, RWKV_IN), lambda i, j: (i, j, 0)),
            full((1, RWKV_IN)), full((1, MIX_W)), full((LANES, 2 * MIX_W)), full((1, MIX_W)),
            full((LANES, MIX_W)), full((1, MIX_W)), full((1, MIX_W)), full((1, MIX_W)),
            full((1, MIX_W)), full((1, MIX_W)), full((LANES, LANES)), full((c, c)),
        ],
        out_specs=pl.BlockSpec((gs, t_tile, MIX_W), lambda i, j: (i, j, 0)),
        out_shape=jax.ShapeDtypeStruct((b, s, MIX_W), BF16),
        scratch_shapes=[pltpu.VMEM((gs, SUBLANES, RWKV_IN), F32),
                        pltpu.VMEM((gs, MIX_W // LANES, LANES, LANES), F32)],
        compiler_params=_cparams(("arbitrary", "arbitrary")),
        name="rwkv7_mixer",
    )(xb, row(mu), row(w0), wla, row(a0), g2.astype(BF16), row(k_k), row(k_a), row(r_k),
      row(ln_w), row(ln_b), hsum, tril)


def _sg_kernel(xc_ref, lnw_ref, lnb_ref, ws_ref, bs_ref, o_ref):
    t_tile = xc_ref.shape[0]
    z = _gelu(xc_ref[...].astype(F32))
    u = z[:, :MIX_W]
    v = z[:, MIX_W:]
    mean = jnp.mean(v, axis=-1, keepdims=True)
    vc = v - mean
    var = jnp.mean(vc * vc, axis=-1, keepdims=True)
    vn = (vc * lax.rsqrt(var + SG_LN_EPS) * lnw_ref[...] + lnb_ref[...]).astype(BF16)
    causal = _iota((SG_CHUNK, SG_CHUNK), 0) >= _iota((SG_CHUNK, SG_CHUNK), 1)
    w_m = [jnp.where(causal, ws_ref[g], 0.0).astype(BF16) for g in range(SG_GROUPS)]
    for n in range(t_tile // SG_CHUNK):
        rows = slice(n * SG_CHUNK, (n + 1) * SG_CHUNK)
        mixed = jnp.concatenate(
            [_nn(w_m[g], vn[rows, g * LANES:(g + 1) * LANES]) for g in range(SG_GROUPS)], axis=1)
        o_ref[rows, :] = (u[rows, :] * (mixed + bs_ref[...])).astype(o_ref.dtype)


def _sg_mixer(xc, ln_w, ln_b, w_s, b_s, t_tile):
    n = xc.shape[0]
    b_full = jnp.repeat(jnp.swapaxes(b_s, 0, 1), MIX_W // SG_GROUPS, axis=1)
    return pl.pallas_call(
        _sg_kernel,
        grid=(n // t_tile,),
        in_specs=[
            pl.BlockSpec((t_tile, 2 * MIX_W), lambda i: (i, 0)),
            pl.BlockSpec((1, MIX_W), lambda i: (0, 0)),
            pl.BlockSpec((1, MIX_W), lambda i: (0, 0)),
            pl.BlockSpec((SG_GROUPS, SG_CHUNK, SG_CHUNK), lambda i: (0, 0, 0)),
            pl.BlockSpec((SG_CHUNK, MIX_W), lambda i: (0, 0)),
        ],
        out_specs=pl.BlockSpec((t_tile, MIX_W), lambda i: (i, 0)),
        out_shape=jax.ShapeDtypeStruct((n, MIX_W), BF16),
        compiler_params=_cparams(("arbitrary",)),
        name="spatial_gating_mixer",
    )(xc, ln_w.reshape(1, MIX_W), ln_b.reshape(1, MIX_W), w_s, b_full)


HGRN_PREC = 1
HGRN_SEQS = 4


def _hgrn_levels(c):
    return [c >> (i + 1) for i in range(c.bit_length() - 1)]


def _block_mid_rows(a, half):
    rows, width = a.shape
    blk = 2 * half
    sub_row = _iota((SUBLANES, 1), 0)
    tiles = []
    for j in range(rows // SUBLANES):
        base = j * SUBLANES
        if blk >= SUBLANES:
            src = (base // blk) * blk + half - 1
            tiles.append(jnp.broadcast_to(a[src:src + 1, :], (SUBLANES, width)))
        else:
            tile = None
            for i in reversed(range(SUBLANES // blk)):
                src = base + i * blk + half - 1
                piece = jnp.broadcast_to(a[src:src + 1, :], (SUBLANES, width))
                tile = piece if tile is None else jnp.where(sub_row < (i + 1) * blk, piece, tile)
            tiles.append(tile)
    return jnp.concatenate(tiles, axis=0)


def _hgrn_kernel(xd_ref, lb_ref, nw_ref, cmat_ref, o_ref, state_ref, *, layer):
    t_tile = xd_ref.shape[1]
    c = HGRN_CHUNK
    levels = _hgrn_levels(c)

    @pl.when(pl.program_id(1) == 0)
    def _():
        state_ref[...] = jnp.zeros_like(state_ref)

    lb_all = lb_ref[...]
    e_all = jnp.exp(lb_all - jnp.max(lb_all, axis=0, keepdims=True))
    probs = e_all / jnp.sum(e_all, axis=0, keepdims=True)
    lb = jnp.sum(probs[0:layer + 1, :], axis=0, keepdims=True) - probs[0:1, :]
    row_c = _iota((c, 1), 0)
    row_cc = _iota((c, c), 0)
    col_cc = _iota((c, c), 1)

    def one_chunk(x, s_mats):
        qr = x[:, 0:MIX_W]
        q = qr * _sigmoid(qr)
        sig = _sigmoid(x[:, MIX_W:2 * MIX_W])
        v = x[:, 2 * MIX_W:]
        lg = jnp.log(jnp.maximum(lb + (1.0 - lb) * sig, GATE_FLOOR))
        key = (1.0 - lb) * (1.0 - sig)
        cum = _mm_exact_rhs_left(cmat_ref[...], lg)
        tot = cum[c - 1:c, :]
        q_chunk = q * jnp.exp(cum)
        k_end = key * jnp.exp(tot - cum)
        yield
        q_lv, k_lv = [], []
        for li, half in enumerate(levels):
            dec = jnp.exp(-jnp.abs(cum - _block_mid_rows(cum, half)))
            second = jnp.bitwise_and(row_c, half) != 0
            q_lv.append(jnp.where(second, q * dec, 0.0))
            k_lv.append(jnp.where(second, 0.0, key * dec))
        qk = q * key
        outs, new_states, all_scores = [], [], []
        for h in range(HGRN_HEADS):
            sl = slice(h * LANES, (h + 1) * LANES)
            scores = None
            for li, half in enumerate(levels):
                sc = _mm(_nt, q_lv[li][:, sl], k_lv[li][:, sl], HGRN_PREC)
                if 2 * half < c:
                    blk = 2 * half
                    sc = jnp.where((row_cc // blk) == (col_cc // blk), sc, 0.0)
                scores = sc if scores is None else scores + sc
            all_scores.append(scores)
            yield
        for h in range(HGRN_HEADS):
            sl = slice(h * LANES, (h + 1) * LANES)
            s_mat = s_mats[h]
            v_h = v[:, sl]
            o_h = (_mm(_nt, q_chunk[:, sl], s_mat, HGRN_PREC) + _mm(_nn, all_scores[h], v_h, HGRN_PREC)
                   + jnp.sum(qk[:, sl], axis=-1, keepdims=True) * v_h)
            ms = jnp.mean(o_h * o_h, axis=-1, keepdims=True)
            outs.append(o_h * lax.rsqrt(ms + NORM_EPS))
            upd = _mm(_tn, v_h, k_end[:, sl], HGRN_PREC)
            new_states.append(s_mat * jnp.exp(tot[:, sl]) + upd)
            yield
        o = jnp.concatenate(outs, axis=1) * nw_ref[...]
        return o.astype(o_ref.dtype), new_states

    def chunk(ci, carry):
        r0 = pl.multiple_of(ci * c, c)
        seqs = range(xd_ref.shape[0])
        xs_in = [xd_ref[g, pl.ds(r0, c), :] for g in seqs]
        states = [[state_ref[g, h] for h in range(HGRN_HEADS)] for g in seqs]
        results = _round_robin([one_chunk(xs_in[g], states[g]) for g in seqs])
        for g in seqs:
            out, new_states = results[g]
            for h in range(HGRN_HEADS):
                state_ref[g, h] = new_states[h]
            o_ref[g, pl.ds(r0, c), :] = out
        return carry

    lax.fori_loop(0, t_tile // c, chunk, 0)


def _hgrn_mixer(xd, lb_all, layer, norm_w, t_tile):
    b, s, _ = xd.shape
    c = HGRN_CHUNK
    depth = lb_all.shape[0]
    gs = HGRN_SEQS if b % HGRN_SEQS == 0 else 1
    cmat = jnp.tril(jnp.ones((c, c), F32)).astype(BF16)
    full = lambda shape: pl.BlockSpec(shape, lambda i, j: (0,) * len(shape))
    return pl.pallas_call(
        functools.partial(_hgrn_kernel, layer=layer),
        grid=(b // gs, s // t_tile),
        in_specs=[
            pl.BlockSpec((gs, t_tile, COL_D), lambda i, j: (i, j, 0)),
            full((depth, MIX_W)), full((1, MIX_W)), full(cmat.shape),
        ],
        out_specs=pl.BlockSpec((gs, t_tile, MIX_W), lambda i, j: (i, j, 0)),
        out_shape=jax.ShapeDtypeStruct((b, s, MIX_W), BF16),
        scratch_shapes=[pltpu.VMEM((gs, HGRN_HEADS, HGRN_EXPAND, HGRN_EXPAND), F32)],
        compiler_params=_cparams(("arbitrary", "arbitrary")),
        name="hgrn2_mixer",
    )(xd, lb_all.astype(F32), norm_w.reshape(1, MIX_W), cmat)


def _merge_kernel(ya_ref, yb_ref, yc_ref, yd_ref, g_ref, x_ref, wbr_ref, wout_ref, o_ref):
    merged = None
    for kk, y_ref in enumerate((ya_ref, yb_ref, yc_ref, yd_ref)):
        gate = _sigmoid(g_ref[:, kk * D_MODEL:(kk + 1) * D_MODEL].astype(F32))
        term = gate * _nn(y_ref[...], wbr_ref[kk])
        merged = term if merged is None else merged + term
    o_ref[...] = x_ref[...] + _nn(merged.astype(BF16), wout_ref[...])


def _merge(ya, yb, yc, yd, g, x, wbr_bf16, wout_bf16, tm):
    n = x.shape[0]
    ytile = pl.BlockSpec((tm, MIX_W), lambda i: (i, 0))
    return pl.pallas_call(
        _merge_kernel,
        grid=(n // tm,),
        in_specs=[
            ytile, ytile, ytile, ytile,
            pl.BlockSpec((tm, 4 * D_MODEL), lambda i: (i, 0)),
            pl.BlockSpec((tm, D_MODEL), lambda i: (i, 0)),
            pl.BlockSpec((4, MIX_W, D_MODEL), lambda i: (0, 0, 0)),
            pl.BlockSpec((D_MODEL, D_MODEL), lambda i: (0, 0)),
        ],
        out_specs=pl.BlockSpec((tm, D_MODEL), lambda i: (i, 0)),
        out_shape=jax.ShapeDtypeStruct((n, D_MODEL), F32),
        compiler_params=_cparams(("arbitrary",)),
        name="merge_out",
    )(ya, yb, yc, yd, g, x, wbr_bf16, wout_bf16)


def _ffn_kernel(x_ref, nf_ref, wg_ref, wv_ref, cwg_ref, cwv_ref, cbg_ref, cbv_ref, wd_ref,
                p_ref, np_ref, pproj_ref, pgate_ref, nfin_ref, o_ref, cg_ref, cv_ref, *, final_norm):
    tm = x_ref.shape[1]
    nb = wg_ref.shape[0]

    @pl.when(pl.program_id(1) == 0)
    def _():
        cg_ref[...] = jnp.zeros_like(cg_ref)
        cv_ref[...] = jnp.zeros_like(cv_ref)

    x = x_ref[0]
    h = _rms(x, nf_ref[...]).astype(BF16)

    def conv(u, carry_ref, cw_ref, cb_ref, j):
        prev = carry_ref[j]
        u1 = _shift_rows(u, prev, 1)
        u2 = _shift_rows(u, prev, 2)
        carry_ref[j] = u[tm - SUBLANES:, :]
        cw = cw_ref[j]
        return cb_ref[j] + cw[2:3, :] * u + cw[1:2, :] * u1 + cw[0:1, :] * u2

    def up(j):
        return _nn(h, wg_ref[j]), _nn(h, wv_ref[j])

    acc = None
    u_next = up(0)
    for j in range(nb):
        ug, uv = u_next
        if j + 1 < nb:
            u_next = up(j + 1)
        ug = conv(ug, cg_ref, cwg_ref, cbg_ref, j)
        uv = conv(uv, cv_ref, cwv_ref, cbv_ref, j)
        act = (_gelu(ug) * uv).astype(BF16)
        d = _nn(act, wd_ref[j])
        acc = d if acc is None else acc + d

    x2 = x + acc
    hp = _rms(x2, np_ref[...]).astype(BF16)
    gate = _sigmoid(_nn(hp, pgate_ref[...]))
    pp = _nn(p_ref[0].astype(BF16), pproj_ref[...])
    x3 = x2 + pp * gate
    if final_norm:
        x3 = _rms(x3, nfin_ref[...])
    o_ref[0] = x3


def _ffn_ple(x1, norm_ffn, wup_bf16, conv_w, conv_b, wdown_bf16, p_l, norm_ple, pproj_bf16,
             pgate_bf16, norm_final, final_norm, tm):
    b, s, d = x1.shape
    nb, blk = FFN_NB, FFN_BLOCK
    wg = wup_bf16[:, :D_FF].reshape(d, nb, blk).transpose(1, 0, 2)
    wv = wup_bf16[:, D_FF:].reshape(d, nb, blk).transpose(1, 0, 2)
    cwg = conv_w[:, :D_FF].reshape(3, nb, blk).transpose(1, 0, 2)
    cwv = conv_w[:, D_FF:].reshape(3, nb, blk).transpose(1, 0, 2)
    cbg = conv_b[:D_FF].reshape(nb, 1, blk)
    cbv = conv_b[D_FF:].reshape(nb, 1, blk)
    wd = wdown_bf16.reshape(nb, blk, d)
    row = lambda a: a.reshape(1, -1)
    const = lambda shape: pl.BlockSpec(shape, lambda i, t: (0,) * len(shape),
                                       pipeline_mode=pl.Buffered(1))
    return pl.pallas_call(
        functools.partial(_ffn_kernel, final_norm=final_norm),
        grid=(b, s // tm),
        in_specs=[
            pl.BlockSpec((1, tm, d), lambda i, t: (i, t, 0)),
            const((1, d)),
            const((nb, d, blk)), const((nb, d, blk)),
            const((nb, 3, blk)), const((nb, 3, blk)),
            const((nb, 1, blk)), const((nb, 1, blk)),
            const((nb, blk, d)),
            pl.BlockSpec((1, tm, PLE_DIM), lambda i, t: (i, t, 0)),
            const((1, d)),
            const((PLE_DIM, d)),
            const((d, d)),
            const((1, d)),
        ],
        out_specs=pl.BlockSpec((1, tm, d), lambda i, t: (i, t, 0)),
        out_shape=jax.ShapeDtypeStruct((b, s, d), F32),
        scratch_shapes=[pltpu.VMEM((nb, SUBLANES, blk), F32),
                        pltpu.VMEM((nb, SUBLANES, blk), F32)],
        compiler_params=_cparams(("arbitrary", "arbitrary")),
        name="ffn_ple",
    )(x1, row(norm_ffn), wg, wv, cwg, cwv, cbg, cbv, wd, p_l, row(norm_ple), pproj_bf16,
      pgate_bf16, row(norm_final))


def _pick_tile(total, want):
    t = min(total, want)
    while total % t:
        t //= 2
    return t


def kernel(x, p, norm_mix, w_in, pool_w, pool_scale, rwkv_mu, rwkv_w0, rwkv_w2, rwkv_a0, rwkv_a2, rwkv_g2, rwkv_kk, rwkv_ka, rwkv_rk, rwkv_ln_w, rwkv_ln_b, sg_ln_w, sg_ln_b, sg_w, sg_b, hgrn_lb, hgrn_norm, w_branch, w_out, norm_ffn, ffn_up, ffn_conv, ffn_conv_b, ffn_down, norm_ple, ple_proj, ple_gate, norm_final):
    b, s, d = x.shape
    depth = w_in.shape[0]
    n = b * s
    ts = _pick_tile(s, 512)
    for l in range(depth):
        w_l = w_in[l].astype(BF16)
        x2d = x.reshape(n, d)
        gain = norm_mix[l]
        xa, xb, xc, xd, g = _in_proj(x2d, gain, w_l, (COL_A, COL_B, COL_C, COL_D, 4 * D_MODEL),
                                     (BF16, F32, BF16, F32, BF16), _pick_tile(n, 512))
        ya = _pool_mixer(xa.reshape(b, s, COL_A), pool_w[l].astype(BF16), pool_scale[l], ts)
        yb = _rwkv_mixer(xb.reshape(b, s, COL_B), rwkv_mu[l], rwkv_w0[l], rwkv_w2[l], rwkv_a0[l],
                         rwkv_a2[l], rwkv_g2[l], rwkv_kk[l], rwkv_ka[l], rwkv_rk[l].reshape(-1),
                         rwkv_ln_w[l], rwkv_ln_b[l], _pick_tile(s, 256))
        yc = _sg_mixer(xc, sg_ln_w[l], sg_ln_b[l], sg_w[l], sg_b[l], _pick_tile(n, 512))
        yd = _hgrn_mixer(xd.reshape(b, s, COL_D), hgrn_lb, l, hgrn_norm[l], _pick_tile(s, 256))
        x1 = _merge(ya.reshape(n, MIX_W), yb.reshape(n, MIX_W), yc, yd.reshape(n, MIX_W), g, x2d,
                    w_branch[l].astype(BF16), w_out[l].astype(BF16), _pick_tile(n, 512))
        x = _ffn_ple(x1.reshape(b, s, d), norm_ffn[l], ffn_up[l].astype(BF16), ffn_conv[l],
                     ffn_conv_b[l], ffn_down[l].astype(BF16), p[l], norm_ple[l],
                     ple_proj[l].astype(BF16), ple_gate[l].astype(BF16), norm_final,
                     l == depth - 1, _pick_tile(s, 512))
    return x
```

```python
import functools

import jax
import jax.numpy as jnp
from jax import lax
from jax.experimental import pallas as pl
from jax.experimental.pallas import tpu as pltpu

F32 = jnp.float32
BF16 = jnp.bfloat16

D_MODEL = 1024
MIX_W = 512
PLE_DIM = 256
POOL_WINDOWS = (2, 4, 8, 16)
POOL_GROUP = 128
POOL_HALO = 16
RWKV_HEAD = 64
RWKV_IN = 1792
RWKV_LN_EPS = 64e-5
SG_CHUNK = 128
SG_GROUPS = 4
SG_LN_EPS = 1e-5
HGRN_HEADS = 4
HGRN_EXPAND = 128
GATE_FLOOR = 1e-30
D_FF = 2816
NORM_EPS = 1e-6
COL_A, COL_B, COL_C, COL_D = 512, 1792, 1024, 1536

LANES = 128
SUBLANES = 8
VMEM_LIMIT_BYTES = 56 * 1024 * 1024

RWKV_CHUNK = 64
HGRN_CHUNK = 64

FFN_BLOCK = 256
FFN_NB = D_FF // FFN_BLOCK


def _cparams(sem):
    return pltpu.CompilerParams(dimension_semantics=sem, vmem_limit_bytes=VMEM_LIMIT_BYTES)


def _nn(a, b):
    return lax.dot_general(a, b, (((1,), (0,)), ((), ())), preferred_element_type=F32)


def _nt(a, b):
    return lax.dot_general(a, b, (((1,), (1,)), ((), ())), preferred_element_type=F32)


def _tn(a, b):
    return lax.dot_general(a, b, (((0,), (0,)), ((), ())), preferred_element_type=F32)


def _split(a, terms):
    out = []
    rem = a
    for _ in range(terms - 1):
        hi = rem.astype(BF16)
        out.append(hi)
        rem = rem - hi.astype(F32)
    out.append(rem.astype(BF16))
    return out


def _mm(dot, a, b, prec):
    if prec == 1:
        return dot(a.astype(BF16), b.astype(BF16))
    a_hi, a_lo = _split(a, 2)
    b_hi, b_lo = _split(b, 2)
    return dot(a_hi, b_hi) + (dot(a_hi, b_lo) + dot(a_lo, b_hi))


def _mm_exact_rhs(a, b_bf16, terms):
    acc = None
    for part in _split(a, terms):
        t = _nn(part, b_bf16)
        acc = t if acc is None else acc + t
    return acc


def _head_sums(arrays, ones_pair_bf16):
    c = arrays[0].shape[0]
    slabs = [a[:, p * LANES:(p + 1) * LANES] for a in arrays for p in range(MIX_W // LANES)]
    s = _mm_exact_rhs(jnp.concatenate(slabs, axis=0), ones_pair_bf16, 2)
    per = MIX_W // LANES
    return [jnp.concatenate([s[(i * per + p) * c:(i * per + p + 1) * c, :] for p in range(per)], axis=1)
            for i in range(len(arrays))]


def _rms(x, gain):
    return x * lax.rsqrt(jnp.mean(x * x, axis=-1, keepdims=True) + NORM_EPS) * gain


def _sigmoid(x):
    return 1.0 / (1.0 + jnp.exp(-x))


def _gelu(x):
    return 0.5 * x * (1.0 + jnp.tanh(0.7978845608028654 * (x + 0.044715 * (x * x * x))))


def _iota(shape, dim):
    return lax.broadcasted_iota(jnp.int32, shape, dim)


def _round_robin(gens):
    results = [None] * len(gens)
    live = list(range(len(gens)))
    while live:
        for idx in list(live):
            try:
                next(gens[idx])
            except StopIteration as stop:
                results[idx] = stop.value
                live.remove(idx)
    return results


def _shift_rows(x, prev_rows, n):
    rolled = pltpu.roll(x, n, axis=0)
    row = _iota(x.shape, 0)
    out = rolled
    for j in range(n):
        out = jnp.where(row == j, prev_rows[SUBLANES - n + j:SUBLANES - n + j + 1, :], out)
    return out


IN_PROJ_CHUNK = 512


def _in_proj_kernel(x_ref, g_ref, w_ref, *o_refs):
    h = _rms(x_ref[...], g_ref[...]).astype(BF16)
    col = 0
    for o_ref in o_refs:
        width = o_ref.shape[1]
        for c0 in range(0, width, IN_PROJ_CHUNK):
            c1 = min(c0 + IN_PROJ_CHUNK, width)
            o_ref[:, c0:c1] = _nn(h, w_ref[:, col + c0:col + c1]).astype(o_ref.dtype)
        col += width


def _in_proj(x, gain, w_bf16, widths, dtypes, tm):
    n, d = x.shape
    m = w_bf16.shape[1]
    assert sum(widths) == m
    return pl.pallas_call(
        _in_proj_kernel,
        grid=(n // tm,),
        in_specs=[
            pl.BlockSpec((tm, d), lambda i: (i, 0)),
            pl.BlockSpec((1, d), lambda i: (0, 0), pipeline_mode=pl.Buffered(1)),
            pl.BlockSpec((d, m), lambda i: (0, 0), pipeline_mode=pl.Buffered(1)),
        ],
        out_specs=[pl.BlockSpec((tm, w), lambda i: (i, 0)) for w in widths],
        out_shape=[jax.ShapeDtypeStruct((n, w), dt) for w, dt in zip(widths, dtypes)],
        compiler_params=_cparams(("arbitrary",)),
        name="in_proj",
    )(x, gain.reshape(1, d), w_bf16)


def _pool_kernel(xa_ref, w_ref, scale_ref, o_ref, carry_ref):
    t_tile = xa_ref.shape[1]
    s_idx = pl.program_id(1)

    @pl.when(s_idx == 0)
    def _():
        carry_ref[...] = jnp.zeros_like(carry_ref)

    a = xa_ref[0].astype(F32)
    ext = jnp.concatenate([carry_ref[...], a], axis=0)
    carry_ref[...] = a[t_tile - POOL_HALO:, :]
    pos = s_idx * t_tile + _iota((t_tile, 1), 0)
    count = (pos + 1).astype(F32)
    outs = []
    for gi, win in enumerate(POOL_WINDOWS):
        col = slice(gi * POOL_GROUP, (gi + 1) * POOL_GROUP)
        s = ext[:, col]
        span = 1
        while span < win:
            s = s + pltpu.roll(s, span, axis=0)
            span *= 2
        mean = s[POOL_HALO:, :] / jnp.minimum(count, float(win))
        diff = (mean - a[:, col]).astype(BF16)
        outs.append(_nn(diff, w_ref[gi]))
    y = jnp.concatenate(outs, axis=1) * scale_ref[...]
    o_ref[0] = y.astype(o_ref.dtype)


def _pool_mixer(xa, pool_w_bf16, pool_scale, t_tile):
    b, s, _ = xa.shape
    return pl.pallas_call(
        _pool_kernel,
        grid=(b, s // t_tile),
        in_specs=[
            pl.BlockSpec((1, t_tile, MIX_W), lambda i, j: (i, j, 0)),
            pl.BlockSpec((len(POOL_WINDOWS), POOL_GROUP, POOL_GROUP), lambda i, j: (0, 0, 0)),
            pl.BlockSpec((1, MIX_W), lambda i, j: (0, 0)),
        ],
        out_specs=pl.BlockSpec((1, t_tile, MIX_W), lambda i, j: (i, j, 0)),
        out_shape=jax.ShapeDtypeStruct((b, s, MIX_W), BF16),
        scratch_shapes=[pltpu.VMEM((POOL_HALO, MIX_W), F32)],
        compiler_params=_cparams(("arbitrary", "arbitrary")),
        name="pool_mixer",
    )(xa, pool_w_bf16, pool_scale.reshape(1, MIX_W))


RWKV_SEQS = 4


def _rwkv_kernel(xb_ref, mu_ref, w0_ref, wla_ref, a0_ref, g2_ref, kk_ref, ka_ref, rk_ref,
                 lnw_ref, lnb_ref, hsum_ref, tril_ref, o_ref, carry_ref, state_ref):
    t_tile = xb_ref.shape[1]
    c = RWKV_CHUNK
    hd = RWKV_HEAD

    @pl.when(pl.program_id(1) == 0)
    def _():
        carry_ref[...] = jnp.zeros_like(carry_ref)
        state_ref[...] = jnp.zeros_like(state_ref)

    lane = _iota((1, LANES), 1)
    head0 = lane < hd
    col_s = jnp.bitwise_and(_iota((c, LANES), 1), hd - 1)
    row_t = _iota((c, LANES), 0)
    strict = row_t > col_s
    incl = row_t >= col_s
    bd_mask = (_iota((LANES, LANES), 0) < hd) == (_iota((LANES, LANES), 1) < hd)
    sys_w = 2 * LANES
    lane_head = _iota((c, sys_w), 1) // hd

    eye_sys = (_iota((c, sys_w), 0) == jnp.bitwise_and(_iota((c, sys_w), 1), hd - 1)).astype(F32)
    npair = MIX_W // LANES
    nsys = MIX_W // sys_w
    nsteps = c.bit_length() - 1

    def block_diag(m):
        zero = jnp.zeros((), m.dtype)
        return jnp.concatenate([jnp.where(lane_head == j, m, zero) for j in range(sys_w // hd)], axis=0)

    def pair_stack(m):
        zero = jnp.zeros((), m.dtype)
        return jnp.concatenate([jnp.where(head0, m, zero), jnp.where(head0, zero, m)], axis=0)

    def prepare(x, prev_rows):
        xprev = _shift_rows(x, prev_rows, 1)
        xs = x + mu_ref[...] * (xprev - x)
        r = xs[:, 0:MIX_W]
        k = xs[:, MIX_W:2 * MIX_W]
        v = xs[:, 2 * MIX_W:3 * MIX_W]
        slab = xs[:, 3 * MIX_W:3 * MIX_W + LANES]
        lg = xs[:, 3 * MIX_W + LANES:]
        slab = jnp.where(head0, jnp.tanh(slab), slab)
        low = _nn(slab.astype(BF16), wla_ref[...])
        gate = _nn(_sigmoid(lg).astype(BF16), g2_ref[...])
        yield
        wl = w0_ref[...] + low[:, :MIX_W]
        neg = -wl
        softplus = jnp.maximum(neg, 0.0) + jnp.log(1.0 + jnp.exp(-jnp.abs(neg)))
        ld = -jnp.exp(-softplus - 0.5)
        cum = _mm_exact_rhs_left(tril_ref[...], ld)
        eta = _sigmoid(a0_ref[...] + low[:, MIX_W:])
        kkv = k * kk_ref[...]
        k2 = k * (1.0 + (eta - 1.0) * ka_ref[...])
        nrm2, rk_sum = _head_sums([kkv * kkv, r * k2 * rk_ref[...]], hsum_ref[...])
        yield
        kkn = kkv / jnp.maximum(jnp.sqrt(nrm2), 1e-12)
        tot = cum[c - 1:c, :]
        e_neg = jnp.exp(-cum)
        rt = r * jnp.exp(cum)
        at = -kkn * jnp.exp(cum - ld)
        b_vec = kkn * eta
        bt = b_vec * e_neg
        kt = k2 * e_neg
        e_end = jnp.exp(tot - cum)
        b_end = b_vec * e_end
        k_end = k2 * e_end
        p_tot = jnp.exp(tot)

        at_b, rt_b = at.astype(BF16), rt.astype(BF16)
        yield
        a_ab, a_ak, a_rbk, v_bd = [], [], [], []
        for p in range(npair):
            sl = slice(p * LANES, (p + 1) * LANES)
            lhs = jnp.concatenate([at_b[:, sl], rt_b[:, sl]], axis=0)
            rhs = jnp.concatenate([pair_stack(bt[:, sl].astype(BF16)),
                                   pair_stack(kt[:, sl].astype(BF16))], axis=0)
            sc = _nt(lhs, rhs)
            a_ab.append(jnp.where(strict, sc[0:c, 0:LANES], 0.0))
            a_ak.append(jnp.where(strict, sc[0:c, LANES:], 0.0).astype(BF16))
            a_rbk.append(jnp.concatenate([jnp.where(incl, sc[c:, 0:LANES], 0.0),
                                          jnp.where(incl, sc[c:, LANES:], 0.0)], axis=1).astype(BF16))
            v_bd.append(pair_stack(v[:, sl].astype(BF16)))
            yield
        pws = [jnp.concatenate([a_ab[2 * q], a_ab[2 * q + 1]], axis=1).astype(BF16) for q in range(nsys)]
        tis = [pws[q].astype(F32) + eye_sys for q in range(nsys)]
        pws = [_nn(pws[q], block_diag(pws[q])).astype(BF16) for q in range(nsys)]
        yield
        for i in range(1, nsteps):
            for q in range(nsys):
                wts = block_diag(pws[q])
                if i + 1 < nsteps:
                    both = _nn(jnp.concatenate([pws[q], tis[q].astype(BF16)], axis=0), wts)
                    pws[q] = both[0:c].astype(BF16)
                    tis[q] = tis[q] + both[c:]
                else:
                    tis[q] = tis[q] + _nn(tis[q].astype(BF16), wts)
            yield
        tinv = [t.astype(BF16) for t in tis]
        return dict(at=at_b, rt=rt_b, a_ak=a_ak, a_rbk=a_rbk, v_bd=v_bd, v=v.astype(BF16), tinv=tinv,
                    b_end=b_end.astype(BF16), k_end=k_end.astype(BF16), p_tot=p_tot, gate=gate,
                    bonus=rk_sum * v)

    def advance(pre, s_mats):
        ws = []
        for p in range(npair):
            sl = slice(p * LANES, (p + 1) * LANES)
            s_b = s_mats[p].astype(BF16)
            ws.append(_nt(pre["at"][:, sl], s_b) + _nn(pre["a_ak"][p], pre["v_bd"][p]))
            yield
        us = []
        for q in range(nsys):
            w_sys = jnp.concatenate([ws[2 * q], ws[2 * q + 1]], axis=1).astype(BF16)
            us.append(_nn(pre["tinv"][q], block_diag(w_sys)))
            yield
        ys, new_states = [], []
        for p in range(npair):
            sl = slice(p * LANES, (p + 1) * LANES)
            u_p = us[p // 2][:, (p % 2) * LANES:(p % 2 + 1) * LANES].astype(BF16)
            s_mat = s_mats[p]
            y_p = (_nt(pre["rt"][:, sl], s_mat.astype(BF16))
                   + _nn(pre["a_rbk"][p], jnp.concatenate([pair_stack(u_p), pre["v_bd"][p]], axis=0)))
            ys.append(y_p)
            upd = _tn(jnp.concatenate([u_p, pre["v"][:, sl]], axis=0),
                      jnp.concatenate([pre["b_end"][:, sl], pre["k_end"][:, sl]], axis=0))
            new_states.append(jnp.where(bd_mask, s_mat * pre["p_tot"][:, sl] + upd, 0.0))
            yield
        y = jnp.concatenate(ys, axis=1)
        inv_hd = 1.0 / hd
        mean = _head_sums([y], hsum_ref[...])[0] * inv_hd
        yield
        yc = y - mean
        var = _head_sums([yc * yc], hsum_ref[...])[0] * inv_hd
        yn = yc * lax.rsqrt(var + RWKV_LN_EPS) * lnw_ref[...] + lnb_ref[...]
        return ((yn + pre["bonus"]) * pre["gate"]).astype(o_ref.dtype), new_states

    seqs = range(xb_ref.shape[0])
    nchunk = t_tile // c

    def load_states():
        return [[state_ref[g, p] for p in range(npair)] for g in seqs]

    def store(results, r0):
        for g in seqs:
            out, new_states = results[g]
            for p in range(npair):
                state_ref[g, p] = new_states[p]
            o_ref[g, pl.ds(r0, c), :] = out

    pre0 = _round_robin([prepare(xb_ref[g, 0:c, :], carry_ref[g]) for g in seqs])

    def chunk(ci, pre):
        r0 = pl.multiple_of(ci * c, c)
        r1 = pl.multiple_of(r0 + c, c)
        nxt = [xb_ref[g, pl.ds(r1, c), :] for g in seqs]
        prv = [xb_ref[g, pl.ds(pl.multiple_of(r1 - SUBLANES, SUBLANES), SUBLANES), :] for g in seqs]
        states = load_states()
        res = _round_robin([advance(pre[g], states[g]) for g in seqs]
                           + [prepare(nxt[g], prv[g]) for g in seqs])
        store(res[:len(seqs)], r0)
        return res[len(seqs):]

    pre_last = lax.fori_loop(0, nchunk - 1, chunk, pre0)
    last = _round_robin([advance(pre_last[g], s) for g, s in zip(seqs, load_states())])
    store(last, (nchunk - 1) * c)
    for g in seqs:
        carry_ref[g] = xb_ref[g, t_tile - SUBLANES:, :]


def _mm_exact_rhs_left(l_bf16, a):
    acc = None
    for part in _split(a, 3):
        t = _nn(l_bf16, part)
        acc = t if acc is None else acc + t
    return acc


def _rwkv_mixer(xb, mu, w0, w2, a0, a2, g2, k_k, k_a, r_k, ln_w, ln_b, t_tile):
    b, s, _ = xb.shape
    c = RWKV_CHUNK
    rank = w2.shape[0]
    wla = jnp.zeros((LANES, 2 * MIX_W), F32)
    wla = wla.at[:rank, :MIX_W].set(w2).at[rank:, MIX_W:].set(a2).astype(BF16)
    gs = RWKV_SEQS if b % RWKV_SEQS == 0 else 1
    hsum = jnp.kron(jnp.eye(LANES // RWKV_HEAD, dtype=F32),
                    jnp.ones((RWKV_HEAD, RWKV_HEAD), F32)).astype(BF16)
    tril = jnp.tril(jnp.ones((c, c), F32)).astype(BF16)
    row = lambda a: a.reshape(1, -1).astype(F32)
    full = lambda shape: pl.BlockSpec(shape, lambda i, j: (0,) * len(shape))
    return pl.pallas_call(
        _rwkv_kernel,
        grid=(b // gs, s // t_tile),
        in_specs=[
            pl.BlockSpec((gs, t_tile, RWKV_IN), lambda i, j: (i, j, 0)),
            full((1, RWKV_IN)), full((1, MIX_W)), full((LANES, 2 * MIX_W)), full((1, MIX_W)),
            full((LANES, MIX_W)), full((1, MIX_W)), full((1, MIX_W)), full((1, MIX_W)),
            full((1, MIX_W)), full((1, MIX_W)), full((LANES, LANES)), full((c, c)),
        ],
        out_specs=pl.BlockSpec((gs, t_tile, MIX_W), lambda i, j: (i, j, 0)),
        out_shape=jax.ShapeDtypeStruct((b, s, MIX_W), BF16),
        scratch_shapes=[pltpu.VMEM((gs, SUBLANES, RWKV_IN), F32),
                        pltpu.VMEM((gs, MIX_W // LANES, LANES, LANES), F32)],
        compiler_params=_cparams(("arbitrary", "arbitrary")),
        name="rwkv7_mixer",
    )(xb, row(mu), row(w0), wla, row(a0), g2.astype(BF16), row(k_k), row(k_a), row(r_k),
      row(ln_w), row(ln_b), hsum, tril)


def _sg_kernel(xc_ref, lnw_ref, lnb_ref, ws_ref, bs_ref, o_ref):
    t_tile = xc_ref.shape[0]
    z = _gelu(xc_ref[...].astype(F32))
    u = z[:, :MIX_W]
    v = z[:, MIX_W:]
    mean = jnp.mean(v, axis=-1, keepdims=True)
    vc = v - mean
    var = jnp.mean(vc * vc, axis=-1, keepdims=True)
    vn = (vc * lax.rsqrt(var + SG_LN_EPS) * lnw_ref[...] + lnb_ref[...]).astype(BF16)
    causal = _iota((SG_CHUNK, SG_CHUNK), 0) >= _iota((SG_CHUNK, SG_CHUNK), 1)
    w_m = [jnp.where(causal, ws_ref[g], 0.0).astype(BF16) for g in range(SG_GROUPS)]
    for n in range(t_tile // SG_CHUNK):
        rows = slice(n * SG_CHUNK, (n + 1) * SG_CHUNK)
        mixed = jnp.concatenate(
            [_nn(w_m[g], vn[rows, g * LANES:(g + 1) * LANES]) for g in range(SG_GROUPS)], axis=1)
        o_ref[rows, :] = (u[rows, :] * (mixed + bs_ref[...])).astype(o_ref.dtype)


def _sg_mixer(xc, ln_w, ln_b, w_s, b_s, t_tile):
    n = xc.shape[0]
    b_full = jnp.repeat(jnp.swapaxes(b_s, 0, 1), MIX_W // SG_GROUPS, axis=1)
    return pl.pallas_call(
        _sg_kernel,
        grid=(n // t_tile,),
        in_specs=[
            pl.BlockSpec((t_tile, 2 * MIX_W), lambda i: (i, 0)),
            pl.BlockSpec((1, MIX_W), lambda i: (0, 0)),
            pl.BlockSpec((1, MIX_W), lambda i: (0, 0)),
            pl.BlockSpec((SG_GROUPS, SG_CHUNK, SG_CHUNK), lambda i: (0, 0, 0)),
            pl.BlockSpec((SG_CHUNK, MIX_W), lambda i: (0, 0)),
        ],
        out_specs=pl.BlockSpec((t_tile, MIX_W), lambda i: (i, 0)),
        out_shape=jax.ShapeDtypeStruct((n, MIX_W), BF16),
        compiler_params=_cparams(("arbitrary",)),
        name="spatial_gating_mixer",
    )(xc, ln_w.reshape(1, MIX_W), ln_b.reshape(1, MIX_W), w_s, b_full)


HGRN_PREC = 1
HGRN_SEQS = 4


def _hgrn_levels(c):
    return [c >> (i + 1) for i in range(c.bit_length() - 1)]


def _block_mid_rows(a, half):
    rows, width = a.shape
    blk = 2 * half
    sub_row = _iota((SUBLANES, 1), 0)
    tiles = []
    for j in range(rows // SUBLANES):
        base = j * SUBLANES
        if blk >= SUBLANES:
            src = (base // blk) * blk + half - 1
            tiles.append(jnp.broadcast_to(a[src:src + 1, :], (SUBLANES, width)))
        else:
            tile = None
            for i in reversed(range(SUBLANES // blk)):
                src = base + i * blk + half - 1
                piece = jnp.broadcast_to(a[src:src + 1, :], (SUBLANES, width))
                tile = piece if tile is None else jnp.where(sub_row < (i + 1) * blk, piece, tile)
            tiles.append(tile)
    return jnp.concatenate(tiles, axis=0)


def _hgrn_kernel(xd_ref, lb_ref, nw_ref, cmat_ref, o_ref, state_ref, *, layer):
    t_tile = xd_ref.shape[1]
    c = HGRN_CHUNK
    levels = _hgrn_levels(c)

    @pl.when(pl.program_id(1) == 0)
    def _():
        state_ref[...] = jnp.zeros_like(state_ref)

    lb_all = lb_ref[...]
    e_all = jnp.exp(lb_all - jnp.max(lb_all, axis=0, keepdims=True))
    probs = e_all / jnp.sum(e_all, axis=0, keepdims=True)
    lb = jnp.sum(probs[0:layer + 1, :], axis=0, keepdims=True) - probs[0:1, :]
    row_c = _iota((c, 1), 0)
    row_cc = _iota((c, c), 0)
    col_cc = _iota((c, c), 1)

    def one_chunk(x, s_mats):
        qr = x[:, 0:MIX_W]
        q = qr * _sigmoid(qr)
        sig = _sigmoid(x[:, MIX_W:2 * MIX_W])
        v = x[:, 2 * MIX_W:]
        lg = jnp.log(jnp.maximum(lb + (1.0 - lb) * sig, GATE_FLOOR))
        key = (1.0 - lb) * (1.0 - sig)
        cum = _mm_exact_rhs_left(cmat_ref[...], lg)
        tot = cum[c - 1:c, :]
        q_chunk = q * jnp.exp(cum)
        k_end = key * jnp.exp(tot - cum)
        yield
        q_lv, k_lv = [], []
        for li, half in enumerate(levels):
            dec = jnp.exp(-jnp.abs(cum - _block_mid_rows(cum, half)))
            second = jnp.bitwise_and(row_c, half) != 0
            q_lv.append(jnp.where(second, q * dec, 0.0))
            k_lv.append(jnp.where(second, 0.0, key * dec))
        qk = q * key
        outs, new_states, all_scores = [], [], []
        for h in range(HGRN_HEADS):
            sl = slice(h * LANES, (h + 1) * LANES)
            scores = None
            for li, half in enumerate(levels):
                sc = _mm(_nt, q_lv[li][:, sl], k_lv[li][:, sl], HGRN_PREC)
                if 2 * half < c:
                    blk = 2 * half
                    sc = jnp.where((row_cc // blk) == (col_cc // blk), sc, 0.0)
                scores = sc if scores is None else scores + sc
            all_scores.append(scores)
            yield
        for h in range(HGRN_HEADS):
            sl = slice(h * LANES, (h + 1) * LANES)
            s_mat = s_mats[h]
            v_h = v[:, sl]
            o_h = (_mm(_nt, q_chunk[:, sl], s_mat, HGRN_PREC) + _mm(_nn, all_scores[h], v_h, HGRN_PREC)
                   + jnp.sum(qk[:, sl], axis=-1, keepdims=True) * v_h)
            ms = jnp.mean(o_h * o_h, axis=-1, keepdims=True)
            outs.append(o_h * lax.rsqrt(ms + NORM_EPS))
            upd = _mm(_tn, v_h, k_end[:, sl], HGRN_PREC)
            new_states.append(s_mat * jnp.exp(tot[:, sl]) + upd)
            yield
        o = jnp.concatenate(outs, axis=1) * nw_ref[...]
        return o.astype(o_ref.dtype), new_states

    def chunk(ci, carry):
        r0 = pl.multiple_of(ci * c, c)
        seqs = range(xd_ref.shape[0])
        xs_in = [xd_ref[g, pl.ds(r0, c), :] for g in seqs]
        states = [[state_ref[g, h] for h in range(HGRN_HEADS)] for g in seqs]
        results = _round_robin([one_chunk(xs_in[g], states[g]) for g in seqs])
        for g in seqs:
            out, new_states = results[g]
            for h in range(HGRN_HEADS):
                state_ref[g, h] = new_states[h]
            o_ref[g, pl.ds(r0, c), :] = out
        return carry

    lax.fori_loop(0, t_tile // c, chunk, 0)


def _hgrn_mixer(xd, lb_all, layer, norm_w, t_tile):
    b, s, _ = xd.shape
    c = HGRN_CHUNK
    depth = lb_all.shape[0]
    gs = HGRN_SEQS if b % HGRN_SEQS == 0 else 1
    cmat = jnp.tril(jnp.ones((c, c), F32)).astype(BF16)
    full = lambda shape: pl.BlockSpec(shape, lambda i, j: (0,) * len(shape))
    return pl.pallas_call(
        functools.partial(_hgrn_kernel, layer=layer),
        grid=(b // gs, s // t_tile),
        in_specs=[
            pl.BlockSpec((gs, t_tile, COL_D), lambda i, j: (i, j, 0)),
            full((depth, MIX_W)), full((1, MIX_W)), full(cmat.shape),
        ],
        out_specs=pl.BlockSpec((gs, t_tile, MIX_W), lambda i, j: (i, j, 0)),
        out_shape=jax.ShapeDtypeStruct((b, s, MIX_W), BF16),
        scratch_shapes=[pltpu.VMEM((gs, HGRN_HEADS, HGRN_EXPAND, HGRN_EXPAND), F32)],
        compiler_params=_cparams(("arbitrary", "arbitrary")),
        name="hgrn2_mixer",
    )(xd, lb_all.astype(F32), norm_w.reshape(1, MIX_W), cmat)


def _merge_kernel(ya_ref, yb_ref, yc_ref, yd_ref, g_ref, x_ref, wbr_ref, wout_ref, o_ref):
    merged = None
    for kk, y_ref in enumerate((ya_ref, yb_ref, yc_ref, yd_ref)):
        gate = _sigmoid(g_ref[:, kk * D_MODEL:(kk + 1) * D_MODEL].astype(F32))
        term = gate * _nn(y_ref[...], wbr_ref[kk])
        merged = term if merged is None else merged + term
    o_ref[...] = x_ref[...] + _nn(merged.astype(BF16), wout_ref[...])


def _merge(ya, yb, yc, yd, g, x, wbr_bf16, wout_bf16, tm):
    n = x.shape[0]
    ytile = pl.BlockSpec((tm, MIX_W), lambda i: (i, 0))
    return pl.pallas_call(
        _merge_kernel,
        grid=(n // tm,),
        in_specs=[
            ytile, ytile, ytile, ytile,
            pl.BlockSpec((tm, 4 * D_MODEL), lambda i: (i, 0)),
            pl.BlockSpec((tm, D_MODEL), lambda i: (i, 0)),
            pl.BlockSpec((4, MIX_W, D_MODEL), lambda i: (0, 0, 0)),
            pl.BlockSpec((D_MODEL, D_MODEL), lambda i: (0, 0)),
        ],
        out_specs=pl.BlockSpec((tm, D_MODEL), lambda i: (i, 0)),
        out_shape=jax.ShapeDtypeStruct((n, D_MODEL), F32),
        compiler_params=_cparams(("arbitrary",)),
        name="merge_out",
    )(ya, yb, yc, yd, g, x, wbr_bf16, wout_bf16)


def _ffn_kernel(x_ref, nf_ref, wg_ref, wv_ref, cwg_ref, cwv_ref, cbg_ref, cbv_ref, wd_ref,
                p_ref, np_ref, pproj_ref, pgate_ref, nfin_ref, o_ref, cg_ref, cv_ref, *, final_norm):
    tm = x_ref.shape[1]
    nb = wg_ref.shape[0]

    @pl.when(pl.program_id(1) == 0)
    def _():
        cg_ref[...] = jnp.zeros_like(cg_ref)
        cv_ref[...] = jnp.zeros_like(cv_ref)

    x = x_ref[0]
    h = _rms(x, nf_ref[...]).astype(BF16)

    def conv(u, carry_ref, cw_ref, cb_ref, j):
        prev = carry_ref[j]
        u1 = _shift_rows(u, prev, 1)
        u2 = _shift_rows(u, prev, 2)
        carry_ref[j] = u[tm - SUBLANES:, :]
        cw = cw_ref[j]
        return cb_ref[j] + cw[2:3, :] * u + cw[1:2, :] * u1 + cw[0:1, :] * u2

    def up(j):
        return _nn(h, wg_ref[j]), _nn(h, wv_ref[j])

    acc = None
    u_next = up(0)
    for j in range(nb):
        ug, uv = u_next
        if j + 1 < nb:
            u_next = up(j + 1)
        ug = conv(ug, cg_ref, cwg_ref, cbg_ref, j)
        uv = conv(uv, cv_ref, cwv_ref, cbv_ref, j)
        act = (_gelu(ug) * uv).astype(BF16)
        d = _nn(act, wd_ref[j])
        acc = d if acc is None else acc + d

    x2 = x + acc
    hp = _rms(x2, np_ref[...]).astype(BF16)
    gate = _sigmoid(_nn(hp, pgate_ref[...]))
    pp = _nn(p_ref[0].astype(BF16), pproj_ref[...])
    x3 = x2 + pp * gate
    if final_norm:
        x3 = _rms(x3, nfin_ref[...])
    o_ref[0] = x3


def _ffn_ple(x1, norm_ffn, wup_bf16, conv_w, conv_b, wdown_bf16, p_l, norm_ple, pproj_bf16,
             pgate_bf16, norm_final, final_norm, tm):
    b, s, d = x1.shape
    nb, blk = FFN_NB, FFN_BLOCK
    wg = wup_bf16[:, :D_FF].reshape(d, nb, blk).transpose(1, 0, 2)
    wv = wup_bf16[:, D_FF:].reshape(d, nb, blk).transpose(1, 0, 2)
    cwg = conv_w[:, :D_FF].reshape(3, nb, blk).transpose(1, 0, 2)
    cwv = conv_w[:, D_FF:].reshape(3, nb, blk).transpose(1, 0, 2)
    cbg = conv_b[:D_FF].reshape(nb, 1, blk)
    cbv = conv_b[D_FF:].reshape(nb, 1, blk)
    wd = wdown_bf16.reshape(nb, blk, d)
    row = lambda a: a.reshape(1, -1)
    const = lambda shape: pl.BlockSpec(shape, lambda i, t: (0,) * len(shape),
                                       pipeline_mode=pl.Buffered(1))
    return pl.pallas_call(
        functools.partial(_ffn_kernel, final_norm=final_norm),
        grid=(b, s // tm),
        in_specs=[
            pl.BlockSpec((1, tm, d), lambda i, t: (i, t, 0)),
            const((1, d)),
            const((nb, d, blk)), const((nb, d, blk)),
            const((nb, 3, blk)), const((nb, 3, blk)),
            const((nb, 1, blk)), const((nb, 1, blk)),
            const((nb, blk, d)),
            pl.BlockSpec((1, tm, PLE_DIM), lambda i, t: (i, t, 0)),
            const((1, d)),
            const((PLE_DIM, d)),
            const((d, d)),
            const((1, d)),
        ],
        out_specs=pl.BlockSpec((1, tm, d), lambda i, t: (i, t, 0)),
        out_shape=jax.ShapeDtypeStruct((b, s, d), F32),
        scratch_shapes=[pltpu.VMEM((nb, SUBLANES, blk), F32),
                        pltpu.VMEM((nb, SUBLANES, blk), F32)],
        compiler_params=_cparams(("arbitrary", "arbitrary")),
        name="ffn_ple",
    )(x1, row(norm_ffn), wg, wv, cwg, cwv, cbg, cbv, wd, p_l, row(norm_ple), pproj_bf16,
      pgate_bf16, row(norm_final))


def _pick_tile(total, want):
    t = min(total, want)
    while total % t:
        t //= 2
    return t


def kernel(x, p, norm_mix, w_in, pool_w, pool_scale, rwkv_mu, rwkv_w0, rwkv_w2, rwkv_a0, rwkv_a2, rwkv_g2, rwkv_kk, rwkv_ka, rwkv_rk, rwkv_ln_w, rwkv_ln_b, sg_ln_w, sg_ln_b, sg_w, sg_b, hgrn_lb, hgrn_norm, w_branch, w_out, norm_ffn, ffn_up, ffn_conv, ffn_conv_b, ffn_down, norm_ple, ple_proj, ple_gate, norm_final):
    b, s, d = x.shape
    depth = w_in.shape[0]
    n = b * s
    ts = _pick_tile(s, 512)
    for l in range(depth):
        w_l = w_in[l].astype(BF16)
        x2d = x.reshape(n, d)
        gain = norm_mix[l]
        xa, xb, xc, xd, g = _in_proj(x2d, gain, w_l, (COL_A, COL_B, COL_C, COL_D, 4 * D_MODEL),
                                     (BF16, F32, BF16, F32, BF16), _pick_tile(n, 512))
        ya = _pool_mixer(xa.reshape(b, s, COL_A), pool_w[l].astype(BF16), pool_scale[l], ts)
        yb = _rwkv_mixer(xb.reshape(b, s, COL_B), rwkv_mu[l], rwkv_w0[l], rwkv_w2[l], rwkv_a0[l],
                         rwkv_a2[l], rwkv_g2[l], rwkv_kk[l], rwkv_ka[l], rwkv_rk[l].reshape(-1),
                         rwkv_ln_w[l], rwkv_ln_b[l], _pick_tile(s, 256))
        yc = _sg_mixer(xc, sg_ln_w[l], sg_ln_b[l], sg_w[l], sg_b[l], _pick_tile(n, 512))
        yd = _hgrn_mixer(xd.reshape(b, s, COL_D), hgrn_lb, l, hgrn_norm[l], _pick_tile(s, 256))
        x1 = _merge(ya.reshape(n, MIX_W), yb.reshape(n, MIX_W), yc, yd.reshape(n, MIX_W), g, x2d,
                    w_branch[l].astype(BF16), w_out[l].astype(BF16), _pick_tile(n, 512))
        x = _ffn_ple(x1.reshape(b, s, d), norm_ffn[l], ffn_up[l].astype(BF16), ffn_conv[l],
                     ffn_conv_b[l], ffn_down[l].astype(BF16), p[l], norm_ple[l],
                     ple_proj[l].astype(BF16), ple_gate[l].astype(BF16), norm_final,
                     l == depth - 1, _pick_tile(s, 512))
    return x
```

```python
import functools

import jax
import jax.numpy as jnp
from jax import lax
from jax.experimental import pallas as pl
from jax.experimental.pallas import tpu as pltpu

F32 = jnp.float32
BF16 = jnp.bfloat16

D_MODEL = 1024
MIX_W = 512
PLE_DIM = 256
POOL_WINDOWS = (2, 4, 8, 16)
POOL_GROUP = 128
POOL_HALO = 16
RWKV_HEAD = 64
RWKV_IN = 1792
RWKV_LN_EPS = 64e-5
SG_CHUNK = 128
SG_GROUPS = 4
SG_LN_EPS = 1e-5
HGRN_HEADS = 4
HGRN_EXPAND = 128
GATE_FLOOR = 1e-30
D_FF = 2816
NORM_EPS = 1e-6
COL_A, COL_B, COL_C, COL_D = 512, 1792, 1024, 1536

LANES = 128
SUBLANES = 8
VMEM_LIMIT_BYTES = 56 * 1024 * 1024

RWKV_CHUNK = 64
HGRN_CHUNK = 64

FFN_BLOCK = 256
FFN_NB = D_FF // FFN_BLOCK
FFN_GROUP_ENDS = (4, 8, 10)


def _cparams(sem):
    return pltpu.CompilerParams(dimension_semantics=sem, vmem_limit_bytes=VMEM_LIMIT_BYTES)


def _nn(a, b):
    return lax.dot_general(a, b, (((1,), (0,)), ((), ())), preferred_element_type=F32)


def _nt(a, b):
    return lax.dot_general(a, b, (((1,), (1,)), ((), ())), preferred_element_type=F32)


def _tn(a, b):
    return lax.dot_general(a, b, (((0,), (0,)), ((), ())), preferred_element_type=F32)


def _split(a, terms):
    out = []
    rem = a
    for _ in range(terms - 1):
        hi = rem.astype(BF16)
        out.append(hi)
        rem = rem - hi.astype(F32)
    out.append(rem.astype(BF16))
    return out


def _mm(dot, a, b, prec):
    if prec == 1:
        return dot(a.astype(BF16), b.astype(BF16))
    a_hi, a_lo = _split(a, 2)
    b_hi, b_lo = _split(b, 2)
    return dot(a_hi, b_hi) + (dot(a_hi, b_lo) + dot(a_lo, b_hi))


def _mm_exact_rhs(a, b_bf16, terms):
    acc = None
    for part in _split(a, terms):
        t = _nn(part, b_bf16)
        acc = t if acc is None else acc + t
    return acc


def _head_sums(arrays, ones_pair_bf16):
    c = arrays[0].shape[0]
    slabs = [a[:, p * LANES:(p + 1) * LANES] for a in arrays for p in range(MIX_W // LANES)]
    s = _mm_exact_rhs(jnp.concatenate(slabs, axis=0), ones_pair_bf16, 2)
    per = MIX_W // LANES
    return [jnp.concatenate([s[(i * per + p) * c:(i * per + p + 1) * c, :] for p in range(per)], axis=1)
            for i in range(len(arrays))]


def _rms(x, gain):
    return x * lax.rsqrt(jnp.mean(x * x, axis=-1, keepdims=True) + NORM_EPS) * gain


def _sigmoid(x):
    return 1.0 / (1.0 + jnp.exp(-x))


def _gelu(x):
    return 0.5 * x * (1.0 + jnp.tanh(0.7978845608028654 * (x + 0.044715 * (x * x * x))))


def _iota(shape, dim):
    return lax.broadcasted_iota(jnp.int32, shape, dim)


def _round_robin(gens):
    results = [None] * len(gens)
    live = list(range(len(gens)))
    while live:
        for idx in list(live):
            try:
                next(gens[idx])
            except StopIteration as stop:
                results[idx] = stop.value
                live.remove(idx)
    return results


def _shift_rows(x, prev_rows, n):
    rolled = pltpu.roll(x, n, axis=0)
    row = _iota(x.shape, 0)
    out = rolled
    for j in range(n):
        out = jnp.where(row == j, prev_rows[SUBLANES - n + j:SUBLANES - n + j + 1, :], out)
    return out


IN_PROJ_CHUNK = 512


def _in_proj_kernel(x_ref, g_ref, w_ref, *o_refs):
    h = _rms(x_ref[...], g_ref[...]).astype(BF16)
    col = 0
    for o_ref in o_refs:
        width = o_ref.shape[1]
        for c0 in range(0, width, IN_PROJ_CHUNK):
            c1 = min(c0 + IN_PROJ_CHUNK, width)
            o_ref[:, c0:c1] = _nn(h, w_ref[:, col + c0:col + c1]).astype(o_ref.dtype)
        col += width


def _in_proj(x, gain, w_bf16, widths, dtypes, tm):
    n, d = x.shape
    m = w_bf16.shape[1]
    assert sum(widths) == m
    return pl.pallas_call(
        _in_proj_kernel,
        grid=(n // tm,),
        in_specs=[
            pl.BlockSpec((tm, d), lambda i: (i, 0)),
            pl.BlockSpec((1, d), lambda i: (0, 0), pipeline_mode=pl.Buffered(1)),
            pl.BlockSpec((d, m), lambda i: (0, 0), pipeline_mode=pl.Buffered(1)),
        ],
        out_specs=[pl.BlockSpec((tm, w), lambda i: (i, 0)) for w in widths],
        out_shape=[jax.ShapeDtypeStruct((n, w), dt) for w, dt in zip(widths, dtypes)],
        compiler_params=_cparams(("arbitrary",)),
        name="in_proj",
    )(x, gain.reshape(1, d), w_bf16)


def _pool_kernel(xa_ref, w_ref, scale_ref, o_ref, carry_ref):
    t_tile = xa_ref.shape[1]
    s_idx = pl.program_id(1)

    @pl.when(s_idx == 0)
    def _():
        carry_ref[...] = jnp.zeros_like(carry_ref)

    a = xa_ref[0].astype(F32)
    ext = jnp.concatenate([carry_ref[...], a], axis=0)
    carry_ref[...] = a[t_tile - POOL_HALO:, :]
    pos = s_idx * t_tile + _iota((t_tile, 1), 0)
    count = (pos + 1).astype(F32)
    outs = []
    for gi, win in enumerate(POOL_WINDOWS):
        col = slice(gi * POOL_GROUP, (gi + 1) * POOL_GROUP)
        s = ext[:, col]
        span = 1
        while span < win:
            s = s + pltpu.roll(s, span, axis=0)
            span *= 2
        mean = s[POOL_HALO:, :] / jnp.minimum(count, float(win))
        diff = (mean - a[:, col]).astype(BF16)
        outs.append(_nn(diff, w_ref[gi]))
    y = jnp.concatenate(outs, axis=1) * scale_ref[...]
    o_ref[0] = y.astype(o_ref.dtype)


def _pool_mixer(xa, pool_w_bf16, pool_scale, t_tile):
    b, s, _ = xa.shape
    return pl.pallas_call(
        _pool_kernel,
        grid=(b, s // t_tile),
        in_specs=[
            pl.BlockSpec((1, t_tile, MIX_W), lambda i, j: (i, j, 0)),
            pl.BlockSpec((len(POOL_WINDOWS), POOL_GROUP, POOL_GROUP), lambda i, j: (0, 0, 0)),
            pl.BlockSpec((1, MIX_W), lambda i, j: (0, 0)),
        ],
        out_specs=pl.BlockSpec((1, t_tile, MIX_W), lambda i, j: (i, j, 0)),
        out_shape=jax.ShapeDtypeStruct((b, s, MIX_W), BF16),
        scratch_shapes=[pltpu.VMEM((POOL_HALO, MIX_W), F32)],
        compiler_params=_cparams(("arbitrary", "arbitrary")),
        name="pool_mixer",
    )(xa, pool_w_bf16, pool_scale.reshape(1, MIX_W))


RWKV_SEQS = 4


def _rwkv_kernel(xb_ref, mu_ref, w0_ref, wla_ref, a0_ref, g2_ref, kk_ref, ka_ref, rk_ref,
                 lnw_ref, lnb_ref, hsum_ref, tril_ref, o_ref, carry_ref, state_ref):
    t_tile = xb_ref.shape[1]
    c = RWKV_CHUNK
    hd = RWKV_HEAD

    @pl.when(pl.program_id(1) == 0)
    def _():
        carry_ref[...] = jnp.zeros_like(carry_ref)
        state_ref[...] = jnp.zeros_like(state_ref)

    lane = _iota((1, LANES), 1)
    head0 = lane < hd
    col_s = jnp.bitwise_and(_iota((c, LANES), 1), hd - 1)
    row_t = _iota((c, LANES), 0)
    strict = row_t > col_s
    incl = row_t >= col_s
    bd_mask = (_iota((LANES, LANES), 0) < hd) == (_iota((LANES, LANES), 1) < hd)
    sys_w = 2 * LANES
    lane_head = _iota((c, sys_w), 1) // hd

    eye_sys = (_iota((c, sys_w), 0) == jnp.bitwise_and(_iota((c, sys_w), 1), hd - 1)).astype(F32)
    npair = MIX_W // LANES
    nsys = MIX_W // sys_w
    nsteps = c.bit_length() - 1

    def block_diag(m):
        zero = jnp.zeros((), m.dtype)
        return jnp.concatenate([jnp.where(lane_head == j, m, zero) for j in range(sys_w // hd)], axis=0)

    def pair_stack(m):
        zero = jnp.zeros((), m.dtype)
        return jnp.concatenate([jnp.where(head0, m, zero), jnp.where(head0, zero, m)], axis=0)

    def prepare(x, prev_rows):
        xprev = _shift_rows(x, prev_rows, 1)
        xs = x + mu_ref[...] * (xprev - x)
        r = xs[:, 0:MIX_W]
        k = xs[:, MIX_W:2 * MIX_W]
        v = xs[:, 2 * MIX_W:3 * MIX_W]
        slab = xs[:, 3 * MIX_W:3 * MIX_W + LANES]
        lg = xs[:, 3 * MIX_W + LANES:]
        slab = jnp.where(head0, jnp.tanh(slab), slab)
        low = _nn(slab.astype(BF16), wla_ref[...])
        gate = _nn(_sigmoid(lg).astype(BF16), g2_ref[...])
        yield
        wl = w0_ref[...] + low[:, :MIX_W]
        neg = -wl
        softplus = jnp.maximum(neg, 0.0) + jnp.log(1.0 + jnp.exp(-jnp.abs(neg)))
        ld = -jnp.exp(-softplus - 0.5)
        cum = _mm_exact_rhs_left(tril_ref[...], ld)
        eta = _sigmoid(a0_ref[...] + low[:, MIX_W:])
        kkv = k * kk_ref[...]
        k2 = k * (1.0 + (eta - 1.0) * ka_ref[...])
        nrm2, rk_sum = _head_sums([kkv * kkv, r * k2 * rk_ref[...]], hsum_ref[...])
        yield
        kkn = kkv / jnp.maximum(jnp.sqrt(nrm2), 1e-12)
        tot = cum[c - 1:c, :]
        e_neg = jnp.exp(-cum)
        rt = r * jnp.exp(cum)
        at = -kkn * jnp.exp(cum - ld)
        b_vec = kkn * eta
        bt = b_vec * e_neg
        kt = k2 * e_neg
        e_end = jnp.exp(tot - cum)
        b_end = b_vec * e_end
        k_end = k2 * e_end
        p_tot = jnp.exp(tot)

        at_b, rt_b = at.astype(BF16), rt.astype(BF16)
        yield
        a_ab, a_ak, a_rbk, v_bd = [], [], [], []
        for p in range(npair):
            sl = slice(p * LANES, (p + 1) * LANES)
            lhs = jnp.concatenate([at_b[:, sl], rt_b[:, sl]], axis=0)
            rhs = jnp.concatenate([pair_stack(bt[:, sl].astype(BF16)),
                                   pair_stack(kt[:, sl].astype(BF16))], axis=0)
            sc = _nt(lhs, rhs)
            a_ab.append(jnp.where(strict, sc[0:c, 0:LANES], 0.0))
            a_ak.append(jnp.where(strict, sc[0:c, LANES:], 0.0).astype(BF16))
            a_rbk.append(jnp.concatenate([jnp.where(incl, sc[c:, 0:LANES], 0.0),
                                          jnp.where(incl, sc[c:, LANES:], 0.0)], axis=1).astype(BF16))
            v_bd.append(pair_stack(v[:, sl].astype(BF16)))
            yield
        pws = [jnp.concatenate([a_ab[2 * q], a_ab[2 * q + 1]], axis=1).astype(BF16) for q in range(nsys)]
        tis = [pws[q].astype(F32) + eye_sys for q in range(nsys)]
        pws = [_nn(pws[q], block_diag(pws[q])).astype(BF16) for q in range(nsys)]
        yield
        for i in range(1, nsteps):
            for q in range(nsys):
                wts = block_diag(pws[q])
                if i + 1 < nsteps:
                    both = _nn(jnp.concatenate([pws[q], tis[q].astype(BF16)], axis=0), wts)
                    pws[q] = both[0:c].astype(BF16)
                    tis[q] = tis[q] + both[c:]
                else:
                    tis[q] = tis[q] + _nn(tis[q].astype(BF16), wts)
            yield
        tinv = [t.astype(BF16) for t in tis]
        return dict(at=at_b, rt=rt_b, a_ak=a_ak, a_rbk=a_rbk, v_bd=v_bd, v=v.astype(BF16), tinv=tinv,
                    b_end=b_end.astype(BF16), k_end=k_end.astype(BF16), p_tot=p_tot, gate=gate,
                    bonus=rk_sum * v)

    def advance(pre, s_mats):
        ws = []
        for p in range(npair):
            sl = slice(p * LANES, (p + 1) * LANES)
            s_b = s_mats[p].astype(BF16)
            ws.append(_nt(pre["at"][:, sl], s_b) + _nn(pre["a_ak"][p], pre["v_bd"][p]))
            yield
        us = []
        for q in range(nsys):
            w_sys = jnp.concatenate([ws[2 * q], ws[2 * q + 1]], axis=1).astype(BF16)
            us.append(_nn(pre["tinv"][q], block_diag(w_sys)))
            yield
        ys, new_states = [], []
        for p in range(npair):
            sl = slice(p * LANES, (p + 1) * LANES)
            u_p = us[p // 2][:, (p % 2) * LANES:(p % 2 + 1) * LANES].astype(BF16)
            s_mat = s_mats[p]
            y_p = (_nt(pre["rt"][:, sl], s_mat.astype(BF16))
                   + _nn(pre["a_rbk"][p], jnp.concatenate([pair_stack(u_p), pre["v_bd"][p]], axis=0)))
            ys.append(y_p)
            upd = _tn(jnp.concatenate([u_p, pre["v"][:, sl]], axis=0),
                      jnp.concatenate([pre["b_end"][:, sl], pre["k_end"][:, sl]], axis=0))
            new_states.append(jnp.where(bd_mask, s_mat * pre["p_tot"][:, sl] + upd, 0.0))
            yield
        y = jnp.concatenate(ys, axis=1)
        inv_hd = 1.0 / hd
        mean = _head_sums([y], hsum_ref[...])[0] * inv_hd
        yield
        yc = y - mean
        var = _head_sums([yc * yc], hsum_ref[...])[0] * inv_hd
        yn = yc * lax.rsqrt(var + RWKV_LN_EPS) * lnw_ref[...] + lnb_ref[...]
        return ((yn + pre["bonus"]) * pre["gate"]).astype(o_ref.dtype), new_states

    seqs = range(xb_ref.shape[0])
    nchunk = t_tile // c

    def load_states():
        return [[state_ref[g, p] for p in range(npair)] for g in seqs]

    def store(results, r0):
        for g in seqs:
            out, new_states = results[g]
            for p in range(npair):
                state_ref[g, p] = new_states[p]
            o_ref[g, pl.ds(r0, c), :] = out

    pre0 = _round_robin([prepare(xb_ref[g, 0:c, :], carry_ref[g]) for g in seqs])

    def chunk(ci, pre):
        r0 = pl.multiple_of(ci * c, c)
        r1 = pl.multiple_of(r0 + c, c)
        nxt = [xb_ref[g, pl.ds(r1, c), :] for g in seqs]
        prv = [xb_ref[g, pl.ds(pl.multiple_of(r1 - SUBLANES, SUBLANES), SUBLANES), :] for g in seqs]
        states = load_states()
        res = _round_robin([advance(pre[g], states[g]) for g in seqs]
                           + [prepare(nxt[g], prv[g]) for g in seqs])
        store(res[:len(seqs)], r0)
        return res[len(seqs):]

    pre_last = lax.fori_loop(0, nchunk - 1, chunk, pre0)
    last = _round_robin([advance(pre_last[g], s) for g, s in zip(seqs, load_states())])
    store(last, (nchunk - 1) * c)
    for g in seqs:
        carry_ref[g] = xb_ref[g, t_tile - SUBLANES:, :]


def _mm_exact_rhs_left(l_bf16, a):
    acc = None
    for part in _split(a, 3):
        t = _nn(l_bf16, part)
        acc = t if acc is None else acc + t
    return acc


def _rwkv_mixer(xb, mu, w0, w2, a0, a2, g2, k_k, k_a, r_k, ln_w, ln_b, t_tile):
    b, s, _ = xb.shape
    c = RWKV_CHUNK
    rank = w2.shape[0]
    wla = jnp.zeros((LANES, 2 * MIX_W), F32)
    wla = wla.at[:rank, :MIX_W].set(w2).at[rank:, MIX_W:].set(a2).astype(BF16)
    gs = RWKV_SEQS if b % RWKV_SEQS == 0 else 1
    hsum = jnp.kron(jnp.eye(LANES // RWKV_HEAD, dtype=F32),
                    jnp.ones((RWKV_HEAD, RWKV_HEAD), F32)).astype(BF16)
    tril = jnp.tril(jnp.ones((c, c), F32)).astype(BF16)
    row = lambda a: a.reshape(1, -1).astype(F32)
    full = lambda shape: pl.BlockSpec(shape, lambda i, j: (0,) * len(shape))
    return pl.pallas_call(
        _rwkv_kernel,
        grid=(b // gs, s // t_tile),
        in_specs=[
            pl.BlockSpec((gs, t_tile, RWKV_IN), lambda i, j: (i, j, 0)),
            full((1, RWKV_IN)), full((1, MIX_W)), full((LANES, 2 * MIX_W)), full((1, MIX_W)),
            full((LANES, MIX_W)), full((1, MIX_W)), full((1, MIX_W)), full((1, MIX_W)),
            full((1, MIX_W)), full((1, MIX_W)), full((LANES, LANES)), full((c, c)),
        ],
        out_specs=pl.BlockSpec((gs, t_tile, MIX_W), lambda i, j: (i, j, 0)),
        out_shape=jax.ShapeDtypeStruct((b, s, MIX_W), BF16),
        scratch_shapes=[pltpu.VMEM((gs, SUBLANES, RWKV_IN), F32),
                        pltpu.VMEM((gs, MIX_W // LANES, LANES, LANES), F32)],
        compiler_params=_cparams(("arbitrary", "arbitrary")),
        name="rwkv7_mixer",
    )(xb, row(mu), row(w0), wla, row(a0), g2.astype(BF16), row(k_k), row(k_a), row(r_k),
      row(ln_w), row(ln_b), hsum, tril)


def _sg_kernel(xc_ref, lnw_ref, lnb_ref, ws_ref, bs_ref, o_ref):
    t_tile = xc_ref.shape[0]
    z = _gelu(xc_ref[...].astype(F32))
    u = z[:, :MIX_W]
    v = z[:, MIX_W:]
    mean = jnp.mean(v, axis=-1, keepdims=True)
    vc = v - mean
    var = jnp.mean(vc * vc, axis=-1, keepdims=True)
    vn = (vc * lax.rsqrt(var + SG_LN_EPS) * lnw_ref[...] + lnb_ref[...]).astype(BF16)
    causal = _iota((SG_CHUNK, SG_CHUNK), 0) >= _iota((SG_CHUNK, SG_CHUNK), 1)
    w_m = [jnp.where(causal, ws_ref[g], 0.0).astype(BF16) for g in range(SG_GROUPS)]
    for n in range(t_tile // SG_CHUNK):
        rows = slice(n * SG_CHUNK, (n + 1) * SG_CHUNK)
        mixed = jnp.concatenate(
            [_nn(w_m[g], vn[rows, g * LANES:(g + 1) * LANES]) for g in range(SG_GROUPS)], axis=1)
        o_ref[rows, :] = (u[rows, :] * (mixed + bs_ref[...])).astype(o_ref.dtype)


def _sg_mixer(xc, ln_w, ln_b, w_s, b_s, t_tile):
    n = xc.shape[0]
    b_full = jnp.repeat(jnp.swapaxes(b_s, 0, 1), MIX_W // SG_GROUPS, axis=1)
    return pl.pallas_call(
        _sg_kernel,
        grid=(n // t_tile,),
        in_specs=[
            pl.BlockSpec((t_tile, 2 * MIX_W), lambda i: (i, 0)),
            pl.BlockSpec((1, MIX_W), lambda i: (0, 0)),
            pl.BlockSpec((1, MIX_W), lambda i: (0, 0)),
            pl.BlockSpec((SG_GROUPS, SG_CHUNK, SG_CHUNK), lambda i: (0, 0, 0)),
            pl.BlockSpec((SG_CHUNK, MIX_W), lambda i: (0, 0)),
        ],
        out_specs=pl.BlockSpec((t_tile, MIX_W), lambda i: (i, 0)),
        out_shape=jax.ShapeDtypeStruct((n, MIX_W), BF16),
        compiler_params=_cparams(("arbitrary",)),
        name="spatial_gating_mixer",
    )(xc, ln_w.reshape(1, MIX_W), ln_b.reshape(1, MIX_W), w_s, b_full)


HGRN_PREC = 1
HGRN_SEQS = 4


def _hgrn_levels(c):
    return [c >> (i + 1) for i in range(c.bit_length() - 1)]


def _block_mid_rows(a, half):
    rows, width = a.shape
    blk = 2 * half
    sub_row = _iota((SUBLANES, 1), 0)
    tiles = []
    for j in range(rows // SUBLANES):
        base = j * SUBLANES
        if blk >= SUBLANES:
            src = (base // blk) * blk + half - 1
            tiles.append(jnp.broadcast_to(a[src:src + 1, :], (SUBLANES, width)))
        else:
            tile = None
            for i in reversed(range(SUBLANES // blk)):
                src = base + i * blk + half - 1
                piece = jnp.broadcast_to(a[src:src + 1, :], (SUBLANES, width))
                tile = piece if tile is None else jnp.where(sub_row < (i + 1) * blk, piece, tile)
            tiles.append(tile)
    return jnp.concatenate(tiles, axis=0)


def _hgrn_kernel(xd_ref, lb_ref, nw_ref, cmat_ref, o_ref, state_ref, *, layer):
    t_tile = xd_ref.shape[1]
    c = HGRN_CHUNK
    levels = _hgrn_levels(c)

    @pl.when(pl.program_id(1) == 0)
    def _():
        state_ref[...] = jnp.zeros_like(state_ref)

    lb_all = lb_ref[...]
    e_all = jnp.exp(lb_all - jnp.max(lb_all, axis=0, keepdims=True))
    probs = e_all / jnp.sum(e_all, axis=0, keepdims=True)
    lb = jnp.sum(probs[0:layer + 1, :], axis=0, keepdims=True) - probs[0:1, :]
    row_c = _iota((c, 1), 0)
    row_cc = _iota((c, c), 0)
    col_cc = _iota((c, c), 1)

    def one_chunk(x, s_mats):
        qr = x[:, 0:MIX_W]
        q = qr * _sigmoid(qr)
        sig = _sigmoid(x[:, MIX_W:2 * MIX_W])
        v = x[:, 2 * MIX_W:]
        lg = jnp.log(jnp.maximum(lb + (1.0 - lb) * sig, GATE_FLOOR))
        key = (1.0 - lb) * (1.0 - sig)
        cum = _mm_exact_rhs_left(cmat_ref[...], lg)
        tot = cum[c - 1:c, :]
        q_chunk = q * jnp.exp(cum)
        k_end = key * jnp.exp(tot - cum)
        yield
        q_lv, k_lv = [], []
        for li, half in enumerate(levels):
            dec = jnp.exp(-jnp.abs(cum - _block_mid_rows(cum, half)))
            second = jnp.bitwise_and(row_c, half) != 0
            q_lv.append(jnp.where(second, q * dec, 0.0))
            k_lv.append(jnp.where(second, 0.0, key * dec))
        qk = q * key
        outs, new_states, all_scores = [], [], []
        for h in range(HGRN_HEADS):
            sl = slice(h * LANES, (h + 1) * LANES)
            scores = None
            for li, half in enumerate(levels):
                sc = _mm(_nt, q_lv[li][:, sl], k_lv[li][:, sl], HGRN_PREC)
                if 2 * half < c:
                    blk = 2 * half
                    sc = jnp.where((row_cc // blk) == (col_cc // blk), sc, 0.0)
                scores = sc if scores is None else scores + sc
            all_scores.append(scores)
            yield
        for h in range(HGRN_HEADS):
            sl = slice(h * LANES, (h + 1) * LANES)
            s_mat = s_mats[h]
            v_h = v[:, sl]
            o_h = (_mm(_nt, q_chunk[:, sl], s_mat, HGRN_PREC) + _mm(_nn, all_scores[h], v_h, HGRN_PREC)
                   + jnp.sum(qk[:, sl], axis=-1, keepdims=True) * v_h)
            ms = jnp.mean(o_h * o_h, axis=-1, keepdims=True)
            outs.append(o_h * lax.rsqrt(ms + NORM_EPS))
            upd = _mm(_tn, v_h, k_end[:, sl], HGRN_PREC)
            new_states.append(s_mat * jnp.exp(tot[:, sl]) + upd)
            yield
        o = jnp.concatenate(outs, axis=1) * nw_ref[...]
        return o.astype(o_ref.dtype), new_states

    def chunk(ci, carry):
        r0 = pl.multiple_of(ci * c, c)
        seqs = range(xd_ref.shape[0])
        xs_in = [xd_ref[g, pl.ds(r0, c), :] for g in seqs]
        states = [[state_ref[g, h] for h in range(HGRN_HEADS)] for g in seqs]
        results = _round_robin([one_chunk(xs_in[g], states[g]) for g in seqs])
        for g in seqs:
            out, new_states = results[g]
            for h in range(HGRN_HEADS):
                state_ref[g, h] = new_states[h]
            o_ref[g, pl.ds(r0, c), :] = out
        return carry

    lax.fori_loop(0, t_tile // c, chunk, 0)


def _hgrn_mixer(xd, lb_all, layer, norm_w, t_tile):
    b, s, _ = xd.shape
    c = HGRN_CHUNK
    depth = lb_all.shape[0]
    gs = HGRN_SEQS if b % HGRN_SEQS == 0 else 1
    cmat = jnp.tril(jnp.ones((c, c), F32)).astype(BF16)
    full = lambda shape: pl.BlockSpec(shape, lambda i, j: (0,) * len(shape))
    return pl.pallas_call(
        functools.partial(_hgrn_kernel, layer=layer),
        grid=(b // gs, s // t_tile),
        in_specs=[
            pl.BlockSpec((gs, t_tile, COL_D), lambda i, j: (i, j, 0)),
            full((depth, MIX_W)), full((1, MIX_W)), full(cmat.shape),
        ],
        out_specs=pl.BlockSpec((gs, t_tile, MIX_W), lambda i, j: (i, j, 0)),
        out_shape=jax.ShapeDtypeStruct((b, s, MIX_W), BF16),
        scratch_shapes=[pltpu.VMEM((gs, HGRN_HEADS, HGRN_EXPAND, HGRN_EXPAND), F32)],
        compiler_params=_cparams(("arbitrary", "arbitrary")),
        name="hgrn2_mixer",
    )(xd, lb_all.astype(F32), norm_w.reshape(1, MIX_W), cmat)


def _merge_kernel(ya_ref, yb_ref, yc_ref, yd_ref, g_ref, x_ref, wbr_ref, wout_ref, o_ref):
    merged = None
    for kk, y_ref in enumerate((ya_ref, yb_ref, yc_ref, yd_ref)):
        gate = _sigmoid(g_ref[:, kk * D_MODEL:(kk + 1) * D_MODEL].astype(F32))
        term = gate * _nn(y_ref[...], wbr_ref[kk])
        merged = term if merged is None else merged + term
    o_ref[...] = x_ref[...] + _nn(merged.astype(BF16), wout_ref[...])


def _merge(ya, yb, yc, yd, g, x, wbr_bf16, wout_bf16, tm):
    n = x.shape[0]
    ytile = pl.BlockSpec((tm, MIX_W), lambda i: (i, 0))
    return pl.pallas_call(
        _merge_kernel,
        grid=(n // tm,),
        in_specs=[
            ytile, ytile, ytile, ytile,
            pl.BlockSpec((tm, 4 * D_MODEL), lambda i: (i, 0)),
            pl.BlockSpec((tm, D_MODEL), lambda i: (i, 0)),
            pl.BlockSpec((4, MIX_W, D_MODEL), lambda i: (0, 0, 0)),
            pl.BlockSpec((D_MODEL, D_MODEL), lambda i: (0, 0)),
        ],
        out_specs=pl.BlockSpec((tm, D_MODEL), lambda i: (i, 0)),
        out_shape=jax.ShapeDtypeStruct((n, D_MODEL), F32),
        compiler_params=_cparams(("arbitrary",)),
        name="merge_out",
    )(ya, yb, yc, yd, g, x, wbr_bf16, wout_bf16)


def _ffn_kernel(x_ref, nf_ref, wup_ref, cw_ref, cb_ref, wd_ref, p_ref, np_ref, pproj_ref, pgate_ref,
                nfin_ref, o_ref, cg_ref, cv_ref, ug_ref, uv_ref, act_ref, acc_ref, *, final_norm):
    tm = x_ref.shape[1]
    blk = FFN_BLOCK
    nb = FFN_NB

    @pl.when(pl.program_id(1) == 0)
    def _():
        cg_ref[...] = jnp.zeros_like(cg_ref)
        cv_ref[...] = jnp.zeros_like(cv_ref)

    x = x_ref[0]
    h = _rms(x, nf_ref[...]).astype(BF16)

    def conv(u, carry_ref, j, col0):
        prev = carry_ref[j]
        u1 = _shift_rows(u, prev, 1)
        u2 = _shift_rows(u, prev, 2)
        carry_ref[j] = u[tm - SUBLANES:, :]
        cw = cw_ref[:, col0:col0 + blk]
        return cb_ref[:, col0:col0 + blk] + cw[2:3, :] * u + cw[1:2, :] * u1 + cw[0:1, :] * u2

    def up(j):
        ug_ref[j % 2] = _nn(h, wup_ref[:, j * blk:(j + 1) * blk])
        uv_ref[j % 2] = _nn(h, wup_ref[:, D_FF + j * blk:D_FF + (j + 1) * blk])

    group_ends = [e for e in FFN_GROUP_ENDS if e < nb] + [nb]
    group_start = {end: start for start, end in zip([0] + group_ends[:-1], group_ends)}

    def down(end):
        k0, k1 = group_start[end] * blk, end * blk
        d = _nn(act_ref[:, k0:k1], wd_ref[k0:k1, :])
        if k0 == 0:
            acc_ref[...] = d
        else:
            acc_ref[...] += d

    up(0)
    for j in range(nb):
        if j + 1 < nb:
            up(j + 1)
        if j in group_start:
            down(j)
        ug = conv(ug_ref[j % 2], cg_ref, j, j * blk)
        uv = conv(uv_ref[j % 2], cv_ref, j, D_FF + j * blk)
        act_ref[:, j * blk:(j + 1) * blk] = (_gelu(ug) * uv).astype(BF16)
    down(nb)

    x2 = x + acc_ref[...]
    hp = _rms(x2, np_ref[...]).astype(BF16)
    gate = _sigmoid(_nn(hp, pgate_ref[...]))
    pp = _nn(p_ref[0].astype(BF16), pproj_ref[...])
    x3 = x2 + pp * gate
    if final_norm:
        x3 = _rms(x3, nfin_ref[...])
    o_ref[0] = x3


def _ffn_ple(x1, norm_ffn, wup_bf16, conv_w, conv_b, wdown_bf16, p_l, norm_ple, pproj_bf16,
             pgate_bf16, norm_final, final_norm, tm):
    b, s, d = x1.shape
    nb, blk = FFN_NB, FFN_BLOCK
    row = lambda a: a.reshape(1, -1)
    const = lambda shape: pl.BlockSpec(shape, lambda i, t: (0,) * len(shape),
                                       pipeline_mode=pl.Buffered(1))
    return pl.pallas_call(
        functools.partial(_ffn_kernel, final_norm=final_norm),
        grid=(b, s // tm),
        in_specs=[
            pl.BlockSpec((1, tm, d), lambda i, t: (i, t, 0)),
            const((1, d)),
            const((d, 2 * D_FF)), const((3, 2 * D_FF)), const((1, 2 * D_FF)), const((D_FF, d)),
            pl.BlockSpec((1, tm, PLE_DIM), lambda i, t: (i, t, 0)),
            const((1, d)),
            const((PLE_DIM, d)),
            const((d, d)),
            const((1, d)),
        ],
        out_specs=pl.BlockSpec((1, tm, d), lambda i, t: (i, t, 0)),
        out_shape=jax.ShapeDtypeStruct((b, s, d), F32),
        scratch_shapes=[pltpu.VMEM((nb, SUBLANES, blk), F32),
                        pltpu.VMEM((nb, SUBLANES, blk), F32),
                        pltpu.VMEM((2, tm, blk), F32),
                        pltpu.VMEM((2, tm, blk), F32),
                        pltpu.VMEM((tm, D_FF), BF16),
                        pltpu.VMEM((tm, d), F32)],
        compiler_params=_cparams(("arbitrary", "arbitrary")),
        name="ffn_ple",
    )(x1, row(norm_ffn), wup_bf16, conv_w, row(conv_b), wdown_bf16, p_l, row(norm_ple), pproj_bf16,
      pgate_bf16, row(norm_final))


def _pick_tile(total, want):
    t = min(total, want)
    while total % t:
        t //= 2
    return t


def kernel(x, p, norm_mix, w_in, pool_w, pool_scale, rwkv_mu, rwkv_w0, rwkv_w2, rwkv_a0, rwkv_a2, rwkv_g2, rwkv_kk, rwkv_ka, rwkv_rk, rwkv_ln_w, rwkv_ln_b, sg_ln_w, sg_ln_b, sg_w, sg_b, hgrn_lb, hgrn_norm, w_branch, w_out, norm_ffn, ffn_up, ffn_conv, ffn_conv_b, ffn_down, norm_ple, ple_proj, ple_gate, norm_final):
    b, s, d = x.shape
    depth = w_in.shape[0]
    n = b * s
    ts = _pick_tile(s, 512)
    for l in range(depth):
        w_l = w_in[l].astype(BF16)
        x2d = x.reshape(n, d)
        gain = norm_mix[l]
        xa, xb, xc, xd, g = _in_proj(x2d, gain, w_l, (COL_A, COL_B, COL_C, COL_D, 4 * D_MODEL),
                                     (BF16, F32, BF16, F32, BF16), _pick_tile(n, 512))
        ya = _pool_mixer(xa.reshape(b, s, COL_A), pool_w[l].astype(BF16), pool_scale[l], ts)
        yb = _rwkv_mixer(xb.reshape(b, s, COL_B), rwkv_mu[l], rwkv_w0[l], rwkv_w2[l], rwkv_a0[l],
                         rwkv_a2[l], rwkv_g2[l], rwkv_kk[l], rwkv_ka[l], rwkv_rk[l].reshape(-1),
                         rwkv_ln_w[l], rwkv_ln_b[l], _pick_tile(s, 256))
        yc = _sg_mixer(xc, sg_ln_w[l], sg_ln_b[l], sg_w[l], sg_b[l], _pick_tile(n, 512))
        yd = _hgrn_mixer(xd.reshape(b, s, COL_D), hgrn_lb, l, hgrn_norm[l], _pick_tile(s, 256))
        x1 = _merge(ya.reshape(n, MIX_W), yb.reshape(n, MIX_W), yc, yd.reshape(n, MIX_W), g, x2d,
                    w_branch[l].astype(BF16), w_out[l].astype(BF16), _pick_tile(n, 512))
        x = _ffn_ple(x1.reshape(b, s, d), norm_ffn[l], ffn_up[l].astype(BF16), ffn_conv[l],
                     ffn_conv_b[l], ffn_down[l].astype(BF16), p[l], norm_ple[l],
                     ple_proj[l].astype(BF16), ple_gate[l].astype(BF16), norm_final,
                     l == depth - 1, _pick_tile(s, 512))
    return x
```

```python
import functools

import jax
import jax.numpy as jnp
from jax import lax
from jax.experimental import pallas as pl
from jax.experimental.pallas import tpu as pltpu

F32 = jnp.float32
BF16 = jnp.bfloat16

D_MODEL = 1024
MIX_W = 512
PLE_DIM = 256
POOL_WINDOWS = (2, 4, 8, 16)
POOL_GROUP = 128
POOL_HALO = 16
RWKV_HEAD = 64
RWKV_IN = 1792
RWKV_LN_EPS = 64e-5
SG_CHUNK = 128
SG_GROUPS = 4
SG_LN_EPS = 1e-5
HGRN_HEADS = 4
HGRN_EXPAND = 128
GATE_FLOOR = 1e-30
D_FF = 2816
NORM_EPS = 1e-6
COL_A, COL_B, COL_C, COL_D = 512, 1792, 1024, 1536

LANES = 128
SUBLANES = 8
VMEM_LIMIT_BYTES = 56 * 1024 * 1024

RWKV_CHUNK = 64
HGRN_CHUNK = 64

FFN_BLOCK = 256
FFN_NB = D_FF // FFN_BLOCK
FFN_GROUP_ENDS = (4, 8, 10)


def _cparams(sem):
    return pltpu.CompilerParams(dimension_semantics=sem, vmem_limit_bytes=VMEM_LIMIT_BYTES)


def _nn(a, b):
    return lax.dot_general(a, b, (((1,), (0,)), ((), ())), preferred_element_type=F32)


def _nt(a, b):
    return lax.dot_general(a, b, (((1,), (1,)), ((), ())), preferred_element_type=F32)


def _tn(a, b):
    return lax.dot_general(a, b, (((0,), (0,)), ((), ())), preferred_element_type=F32)


def _split(a, terms):
    out = []
    rem = a
    for _ in range(terms - 1):
        hi = rem.astype(BF16)
        out.append(hi)
        rem = rem - hi.astype(F32)
    out.append(rem.astype(BF16))
    return out


def _mm(dot, a, b, prec):
    if prec == 1:
        return dot(a.astype(BF16), b.astype(BF16))
    a_hi, a_lo = _split(a, 2)
    b_hi, b_lo = _split(b, 2)
    return dot(a_hi, b_hi) + (dot(a_hi, b_lo) + dot(a_lo, b_hi))


def _mm_exact_rhs(a, b_bf16, terms):
    acc = None
    for part in _split(a, terms):
        t = _nn(part, b_bf16)
        acc = t if acc is None else acc + t
    return acc


def _head_sums(arrays, ones_pair_bf16, terms=2):
    c = arrays[0].shape[0]
    slabs = [a[:, p * LANES:(p + 1) * LANES] for a in arrays for p in range(MIX_W // LANES)]
    s = _mm_exact_rhs(jnp.concatenate(slabs, axis=0), ones_pair_bf16, terms)
    per = MIX_W // LANES
    return [jnp.concatenate([s[(i * per + p) * c:(i * per + p + 1) * c, :] for p in range(per)], axis=1)
            for i in range(len(arrays))]


def _rms(x, gain):
    return x * lax.rsqrt(jnp.mean(x * x, axis=-1, keepdims=True) + NORM_EPS) * gain


def _sigmoid(x):
    return 1.0 / (1.0 + jnp.exp(-x))


def _gelu(x):
    return 0.5 * x * (1.0 + jnp.tanh(0.7978845608028654 * (x + 0.044715 * (x * x * x))))


def _iota(shape, dim):
    return lax.broadcasted_iota(jnp.int32, shape, dim)


def _round_robin(gens):
    results = [None] * len(gens)
    live = list(range(len(gens)))
    while live:
        for idx in list(live):
            try:
                next(gens[idx])
            except StopIteration as stop:
                results[idx] = stop.value
                live.remove(idx)
    return results


def _shift_rows(x, prev_rows, n):
    rolled = pltpu.roll(x, n, axis=0)
    row = _iota(x.shape, 0)
    out = rolled
    for j in range(n):
        out = jnp.where(row == j, prev_rows[SUBLANES - n + j:SUBLANES - n + j + 1, :], out)
    return out


IN_PROJ_CHUNK = 512


def _in_proj_kernel(x_ref, g_ref, w_ref, *o_refs):
    h = _rms(x_ref[...], g_ref[...]).astype(BF16)
    col = 0
    for o_ref in o_refs:
        width = o_ref.shape[1]
        for c0 in range(0, width, IN_PROJ_CHUNK):
            c1 = min(c0 + IN_PROJ_CHUNK, width)
            o_ref[:, c0:c1] = _nn(h, w_ref[:, col + c0:col + c1]).astype(o_ref.dtype)
        col += width


def _in_proj(x, gain, w_bf16, widths, dtypes, tm):
    n, d = x.shape
    m = w_bf16.shape[1]
    assert sum(widths) == m
    return pl.pallas_call(
        _in_proj_kernel,
        grid=(n // tm,),
        in_specs=[
            pl.BlockSpec((tm, d), lambda i: (i, 0)),
            pl.BlockSpec((1, d), lambda i: (0, 0), pipeline_mode=pl.Buffered(1)),
            pl.BlockSpec((d, m), lambda i: (0, 0), pipeline_mode=pl.Buffered(1)),
        ],
        out_specs=[pl.BlockSpec((tm, w), lambda i: (i, 0)) for w in widths],
        out_shape=[jax.ShapeDtypeStruct((n, w), dt) for w, dt in zip(widths, dtypes)],
        compiler_params=_cparams(("arbitrary",)),
        name="in_proj",
    )(x, gain.reshape(1, d), w_bf16)


def _pool_kernel(xa_ref, w_ref, scale_ref, o_ref, carry_ref):
    t_tile = xa_ref.shape[1]
    s_idx = pl.program_id(1)

    @pl.when(s_idx == 0)
    def _():
        carry_ref[...] = jnp.zeros_like(carry_ref)

    a = xa_ref[0].astype(F32)
    ext = jnp.concatenate([carry_ref[...], a], axis=0)
    carry_ref[...] = a[t_tile - POOL_HALO:, :]
    pos = s_idx * t_tile + _iota((t_tile, 1), 0)
    count = (pos + 1).astype(F32)
    outs = []
    for gi, win in enumerate(POOL_WINDOWS):
        col = slice(gi * POOL_GROUP, (gi + 1) * POOL_GROUP)
        s = ext[:, col]
        span = 1
        while span < win:
            s = s + pltpu.roll(s, span, axis=0)
            span *= 2
        mean = s[POOL_HALO:, :] / jnp.minimum(count, float(win))
        diff = (mean - a[:, col]).astype(BF16)
        outs.append(_nn(diff, w_ref[gi]))
    y = jnp.concatenate(outs, axis=1) * scale_ref[...]
    o_ref[0] = y.astype(o_ref.dtype)


def _pool_mixer(xa, pool_w_bf16, pool_scale, t_tile):
    b, s, _ = xa.shape
    return pl.pallas_call(
        _pool_kernel,
        grid=(b, s // t_tile),
        in_specs=[
            pl.BlockSpec((1, t_tile, MIX_W), lambda i, j: (i, j, 0)),
            pl.BlockSpec((len(POOL_WINDOWS), POOL_GROUP, POOL_GROUP), lambda i, j: (0, 0, 0)),
            pl.BlockSpec((1, MIX_W), lambda i, j: (0, 0)),
        ],
        out_specs=pl.BlockSpec((1, t_tile, MIX_W), lambda i, j: (i, j, 0)),
        out_shape=jax.ShapeDtypeStruct((b, s, MIX_W), BF16),
        scratch_shapes=[pltpu.VMEM((POOL_HALO, MIX_W), F32)],
        compiler_params=_cparams(("arbitrary", "arbitrary")),
        name="pool_mixer",
    )(xa, pool_w_bf16, pool_scale.reshape(1, MIX_W))


RWKV_DECAY_SCALE = 0.6065306597126334
RWKV_CUM_TERMS = 2
RWKV_SUM_TERMS = 1
RWKV_SEQS = 4


def _rwkv_kernel(xb_ref, mu_ref, w0_ref, wla_ref, a0_ref, g2_ref, kk_ref, ka_ref, rk_ref,
                 lnw_ref, lnb_ref, hsum_ref, tril_ref, o_ref, carry_ref, state_ref):
    t_tile = xb_ref.shape[1]
    c = RWKV_CHUNK
    hd = RWKV_HEAD

    @pl.when(pl.program_id(1) == 0)
    def _():
        carry_ref[...] = jnp.zeros_like(carry_ref)
        state_ref[...] = jnp.zeros_like(state_ref)

    lane = _iota((1, LANES), 1)
    head0 = lane < hd
    col_s = jnp.bitwise_and(_iota((c, LANES), 1), hd - 1)
    row_t = _iota((c, LANES), 0)
    strict = row_t > col_s
    incl = row_t >= col_s
    bd_mask = (_iota((LANES, LANES), 0) < hd) == (_iota((LANES, LANES), 1) < hd)
    sys_w = 2 * LANES
    lane_head = _iota((c, sys_w), 1) // hd

    eye_sys = (_iota((c, sys_w), 0) == jnp.bitwise_and(_iota((c, sys_w), 1), hd - 1)).astype(F32)
    npair = MIX_W // LANES
    nsys = MIX_W // sys_w
    nsteps = c.bit_length() - 1

    def block_diag(m):
        zero = jnp.zeros((), m.dtype)
        return jnp.concatenate([jnp.where(lane_head == j, m, zero) for j in range(sys_w // hd)], axis=0)

    def pair_stack(m):
        zero = jnp.zeros((), m.dtype)
        return jnp.concatenate([jnp.where(head0, m, zero), jnp.where(head0, zero, m)], axis=0)

    def prepare(x, prev_rows):
        xprev = _shift_rows(x, prev_rows, 1)
        xs = x + mu_ref[...] * (xprev - x)
        r = xs[:, 0:MIX_W]
        k = xs[:, MIX_W:2 * MIX_W]
        v = xs[:, 2 * MIX_W:3 * MIX_W]
        slab = xs[:, 3 * MIX_W:3 * MIX_W + LANES]
        lg = xs[:, 3 * MIX_W + LANES:]
        slab = jnp.where(head0, jnp.tanh(slab), slab)
        low = _nn(slab.astype(BF16), wla_ref[...])
        gate = _nn(_sigmoid(lg).astype(BF16), g2_ref[...])
        yield
        ld = -RWKV_DECAY_SCALE * _sigmoid(w0_ref[...] + low[:, :MIX_W])
        cum = _mm_exact_rhs_left(tril_ref[...], ld, RWKV_CUM_TERMS)
        eta = _sigmoid(a0_ref[...] + low[:, MIX_W:])
        kkv = k * kk_ref[...]
        k2 = k * (1.0 + (eta - 1.0) * ka_ref[...])
        nrm2, rk_sum = _head_sums([kkv * kkv, r * k2 * rk_ref[...]], hsum_ref[...], RWKV_SUM_TERMS)
        yield
        kkn = kkv / jnp.maximum(jnp.sqrt(nrm2), 1e-12)
        tot = cum[c - 1:c, :]
        e_neg = jnp.exp(-cum)
        rt = r * jnp.exp(cum)
        at = -kkn * jnp.exp(cum - ld)
        b_vec = kkn * eta
        bt = b_vec * e_neg
        kt = k2 * e_neg
        e_end = jnp.exp(tot - cum)
        b_end = b_vec * e_end
        k_end = k2 * e_end
        p_tot = jnp.exp(tot)

        at_b, rt_b = at.astype(BF16), rt.astype(BF16)
        yield
        a_ab, a_ak, a_rbk, v_bd = [], [], [], []
        for p in range(npair):
            sl = slice(p * LANES, (p + 1) * LANES)
            lhs = jnp.concatenate([at_b[:, sl], rt_b[:, sl]], axis=0)
            rhs = jnp.concatenate([pair_stack(bt[:, sl].astype(BF16)),
                                   pair_stack(kt[:, sl].astype(BF16))], axis=0)
            sc = _nt(lhs, rhs)
            a_ab.append(jnp.where(strict, sc[0:c, 0:LANES], 0.0))
            a_ak.append(jnp.where(strict, sc[0:c, LANES:], 0.0).astype(BF16))
            a_rbk.append(jnp.concatenate([jnp.where(incl, sc[c:, 0:LANES], 0.0),
                                          jnp.where(incl, sc[c:, LANES:], 0.0)], axis=1).astype(BF16))
            v_bd.append(pair_stack(v[:, sl].astype(BF16)))
            yield
        pws = [jnp.concatenate([a_ab[2 * q], a_ab[2 * q + 1]], axis=1).astype(BF16) for q in range(nsys)]
        tis = [pws[q].astype(F32) + eye_sys for q in range(nsys)]
        pws = [_nn(pws[q], block_diag(pws[q])).astype(BF16) for q in range(nsys)]
        yield
        for i in range(1, nsteps):
            for q in range(nsys):
                wts = block_diag(pws[q])
                if i + 1 < nsteps:
                    both = _nn(jnp.concatenate([pws[q], tis[q].astype(BF16)], axis=0), wts)
                    pws[q] = both[0:c].astype(BF16)
                    tis[q] = tis[q] + both[c:]
                else:
                    tis[q] = tis[q] + _nn(tis[q].astype(BF16), wts)
            yield
        tinv = [t.astype(BF16) for t in tis]
        return dict(at=at_b, rt=rt_b, a_ak=a_ak, a_rbk=a_rbk, v_bd=v_bd, v=v.astype(BF16), tinv=tinv,
                    b_end=b_end.astype(BF16), k_end=k_end.astype(BF16), p_tot=p_tot, gate=gate,
                    bonus=rk_sum * v)

    def advance(pre, s_mats):
        ws = []
        for p in range(npair):
            sl = slice(p * LANES, (p + 1) * LANES)
            s_b = s_mats[p].astype(BF16)
            ws.append(_nt(pre["at"][:, sl], s_b) + _nn(pre["a_ak"][p], pre["v_bd"][p]))
            yield
        us = []
        for q in range(nsys):
            w_sys = jnp.concatenate([ws[2 * q], ws[2 * q + 1]], axis=1).astype(BF16)
            us.append(_nn(pre["tinv"][q], block_diag(w_sys)))
            yield
        ys, new_states = [], []
        for p in range(npair):
            sl = slice(p * LANES, (p + 1) * LANES)
            u_p = us[p // 2][:, (p % 2) * LANES:(p % 2 + 1) * LANES].astype(BF16)
            s_mat = s_mats[p]
            y_p = (_nt(pre["rt"][:, sl], s_mat.astype(BF16))
                   + _nn(pre["a_rbk"][p], jnp.concatenate([pair_stack(u_p), pre["v_bd"][p]], axis=0)))
            ys.append(y_p)
            upd = _tn(jnp.concatenate([u_p, pre["v"][:, sl]], axis=0),
                      jnp.concatenate([pre["b_end"][:, sl], pre["k_end"][:, sl]], axis=0))
            new_states.append(jnp.where(bd_mask, s_mat * pre["p_tot"][:, sl] + upd, 0.0))
            yield
        y = jnp.concatenate(ys, axis=1)
        inv_hd = 1.0 / hd
        mean = _head_sums([y], hsum_ref[...], RWKV_SUM_TERMS)[0] * inv_hd
        yield
        yc = y - mean
        var = _head_sums([yc * yc], hsum_ref[...], RWKV_SUM_TERMS)[0] * inv_hd
        yn = yc * lax.rsqrt(var + RWKV_LN_EPS) * lnw_ref[...] + lnb_ref[...]
        return ((yn + pre["bonus"]) * pre["gate"]).astype(o_ref.dtype), new_states

    seqs = range(xb_ref.shape[0])
    nchunk = t_tile // c

    def load_states():
        return [[state_ref[g, p] for p in range(npair)] for g in seqs]

    def store(results, r0):
        for g in seqs:
            out, new_states = results[g]
            for p in range(npair):
                state_ref[g, p] = new_states[p]
            o_ref[g, pl.ds(r0, c), :] = out

    pre0 = _round_robin([prepare(xb_ref[g, 0:c, :], carry_ref[g]) for g in seqs])

    def chunk(ci, pre):
        r0 = pl.multiple_of(ci * c, c)
        r1 = pl.multiple_of(r0 + c, c)
        nxt = [xb_ref[g, pl.ds(r1, c), :] for g in seqs]
        prv = [xb_ref[g, pl.ds(pl.multiple_of(r1 - SUBLANES, SUBLANES), SUBLANES), :] for g in seqs]
        states = load_states()
        res = _round_robin([advance(pre[g], states[g]) for g in seqs]
                           + [prepare(nxt[g], prv[g]) for g in seqs])
        store(res[:len(seqs)], r0)
        return res[len(seqs):]

    pre_last = lax.fori_loop(0, nchunk - 1, chunk, pre0)
    last = _round_robin([advance(pre_last[g], s) for g, s in zip(seqs, load_states())])
    store(last, (nchunk - 1) * c)
    for g in seqs:
        carry_ref[g] = xb_ref[g, t_tile - SUBLANES:, :]


def _mm_exact_rhs_left(l_bf16, a, terms=3):
    acc = None
    for part in _split(a, terms):
        t = _nn(l_bf16, part)
        acc = t if acc is None else acc + t
    return acc


def _rwkv_mixer(xb, mu, w0, w2, a0, a2, g2, k_k, k_a, r_k, ln_w, ln_b, t_tile):
    b, s, _ = xb.shape
    c = RWKV_CHUNK
    rank = w2.shape[0]
    wla = jnp.zeros((LANES, 2 * MIX_W), F32)
    wla = wla.at[:rank, :MIX_W].set(w2).at[rank:, MIX_W:].set(a2).astype(BF16)
    gs = RWKV_SEQS if b % RWKV_SEQS == 0 else 1
    hsum = jnp.kron(jnp.eye(LANES // RWKV_HEAD, dtype=F32),
                    jnp.ones((RWKV_HEAD, RWKV_HEAD), F32)).astype(BF16)
    tril = jnp.tril(jnp.ones((c, c), F32)).astype(BF16)
    row = lambda a: a.reshape(1, -1).astype(F32)
    full = lambda shape: pl.BlockSpec(shape, lambda i, j: (0,) * len(shape))
    return pl.pallas_call(
        _rwkv_kernel,
        grid=(b // gs, s // t_tile),
        in_specs=[
            pl.BlockSpec((gs, t_tile, RWKV_IN), lambda i, j: (i, j, 0)),
            full((1, RWKV_IN)), full((1, MIX_W)), full((LANES, 2 * MIX_W)), full((1, MIX_W)),
            full((LANES, MIX_W)), full((1, MIX_W)), full((1, MIX_W)), full((1, MIX_W)),
            full((1, MIX_W)), full((1, MIX_W)), full((LANES, LANES)), full((c, c)),
        ],
        out_specs=pl.BlockSpec((gs, t_tile, MIX_W), lambda i, j: (i, j, 0)),
        out_shape=jax.ShapeDtypeStruct((b, s, MIX_W), BF16),
        scratch_shapes=[pltpu.VMEM((gs, SUBLANES, RWKV_IN), F32),
                        pltpu.VMEM((gs, MIX_W // LANES, LANES, LANES), F32)],
        compiler_params=_cparams(("arbitrary", "arbitrary")),
        name="rwkv7_mixer",
    )(xb, row(mu), row(w0), wla, row(a0), g2.astype(BF16), row(k_k), row(k_a), row(r_k),
      row(ln_w), row(ln_b), hsum, tril)


def _sg_kernel(xc_ref, lnw_ref, lnb_ref, ws_ref, bs_ref, o_ref):
    t_tile = xc_ref.shape[0]
    z = _gelu(xc_ref[...].astype(F32))
    u = z[:, :MIX_W]
    v = z[:, MIX_W:]
    mean = jnp.mean(v, axis=-1, keepdims=True)
    vc = v - mean
    var = jnp.mean(vc * vc, axis=-1, keepdims=True)
    vn = (vc * lax.rsqrt(var + SG_LN_EPS) * lnw_ref[...] + lnb_ref[...]).astype(BF16)
    causal = _iota((SG_CHUNK, SG_CHUNK), 0) >= _iota((SG_CHUNK, SG_CHUNK), 1)
    w_m = [jnp.where(causal, ws_ref[g], 0.0).astype(BF16) for g in range(SG_GROUPS)]
    for n in range(t_tile // SG_CHUNK):
        rows = slice(n * SG_CHUNK, (n + 1) * SG_CHUNK)
        mixed = jnp.concatenate(
            [_nn(w_m[g], vn[rows, g * LANES:(g + 1) * LANES]) for g in range(SG_GROUPS)], axis=1)
        o_ref[rows, :] = (u[rows, :] * (mixed + bs_ref[...])).astype(o_ref.dtype)


def _sg_mixer(xc, ln_w, ln_b, w_s, b_s, t_tile):
    n = xc.shape[0]
    b_full = jnp.repeat(jnp.swapaxes(b_s, 0, 1), MIX_W // SG_GROUPS, axis=1)
    return pl.pallas_call(
        _sg_kernel,
        grid=(n // t_tile,),
        in_specs=[
            pl.BlockSpec((t_tile, 2 * MIX_W), lambda i: (i, 0)),
            pl.BlockSpec((1, MIX_W), lambda i: (0, 0)),
            pl.BlockSpec((1, MIX_W), lambda i: (0, 0)),
            pl.BlockSpec((SG_GROUPS, SG_CHUNK, SG_CHUNK), lambda i: (0, 0, 0)),
            pl.BlockSpec((SG_CHUNK, MIX_W), lambda i: (0, 0)),
        ],
        out_specs=pl.BlockSpec((t_tile, MIX_W), lambda i: (i, 0)),
        out_shape=jax.ShapeDtypeStruct((n, MIX_W), BF16),
        compiler_params=_cparams(("arbitrary",)),
        name="spatial_gating_mixer",
    )(xc, ln_w.reshape(1, MIX_W), ln_b.reshape(1, MIX_W), w_s, b_full)


HGRN_PREC = 1
HGRN_SEQS = 4


def _hgrn_levels(c):
    return [c >> (i + 1) for i in range(c.bit_length() - 1)]


def _block_mid_rows(a, half):
    rows, width = a.shape
    blk = 2 * half
    sub_row = _iota((SUBLANES, 1), 0)
    tiles = []
    for j in range(rows // SUBLANES):
        base = j * SUBLANES
        if blk >= SUBLANES:
            src = (base // blk) * blk + half - 1
            tiles.append(jnp.broadcast_to(a[src:src + 1, :], (SUBLANES, width)))
        else:
            tile = None
            for i in reversed(range(SUBLANES // blk)):
                src = base + i * blk + half - 1
                piece = jnp.broadcast_to(a[src:src + 1, :], (SUBLANES, width))
                tile = piece if tile is None else jnp.where(sub_row < (i + 1) * blk, piece, tile)
            tiles.append(tile)
    return jnp.concatenate(tiles, axis=0)


def _hgrn_kernel(xd_ref, lb_ref, nw_ref, cmat_ref, o_ref, state_ref, *, layer):
    t_tile = xd_ref.shape[1]
    c = HGRN_CHUNK
    levels = _hgrn_levels(c)

    @pl.when(pl.program_id(1) == 0)
    def _():
        state_ref[...] = jnp.zeros_like(state_ref)

    lb_all = lb_ref[...]
    e_all = jnp.exp(lb_all - jnp.max(lb_all, axis=0, keepdims=True))
    probs = e_all / jnp.sum(e_all, axis=0, keepdims=True)
    lb = jnp.sum(probs[0:layer + 1, :], axis=0, keepdims=True) - probs[0:1, :]
    row_c = _iota((c, 1), 0)
    row_cc = _iota((c, c), 0)
    col_cc = _iota((c, c), 1)

    def one_chunk(x, s_mats):
        qr = x[:, 0:MIX_W]
        q = qr * _sigmoid(qr)
        sig = _sigmoid(x[:, MIX_W:2 * MIX_W])
        v = x[:, 2 * MIX_W:]
        lg = jnp.log(jnp.maximum(lb + (1.0 - lb) * sig, GATE_FLOOR))
        key = (1.0 - lb) * (1.0 - sig)
        cum = _mm_exact_rhs_left(cmat_ref[...], lg)
        tot = cum[c - 1:c, :]
        q_chunk = q * jnp.exp(cum)
        k_end = key * jnp.exp(tot - cum)
        yield
        q_lv, k_lv = [], []
        zero_tile = jnp.zeros((SUBLANES, MIX_W), F32)
        for li, half in enumerate(levels):
            if half >= SUBLANES:
                q_tiles, k_tiles = [], []
                for base in range(0, c, SUBLANES):
                    rows = slice(base, base + SUBLANES)
                    mid_row = (base // (2 * half)) * (2 * half) + half - 1
                    mid = cum[mid_row:mid_row + 1, :]
                    if base & half:
                        q_tiles.append(q[rows] * jnp.exp(cum[rows] - mid))
                        k_tiles.append(zero_tile)
                    else:
                        q_tiles.append(zero_tile)
                        k_tiles.append(key[rows] * jnp.exp(mid - cum[rows]))
                q_lv.append(jnp.concatenate(q_tiles, axis=0))
                k_lv.append(jnp.concatenate(k_tiles, axis=0))
            else:
                dec = jnp.exp(-jnp.abs(cum - _block_mid_rows(cum, half)))
                second = jnp.bitwise_and(row_c, half) != 0
                q_lv.append(jnp.where(second, q * dec, 0.0))
                k_lv.append(jnp.where(second, 0.0, key * dec))
        qk = q * key
        outs, new_states, all_scores = [], [], []
        for h in range(HGRN_HEADS):
            sl = slice(h * LANES, (h + 1) * LANES)
            scores = None
            for li, half in enumerate(levels):
                sc = _mm(_nt, q_lv[li][:, sl], k_lv[li][:, sl], HGRN_PREC)
                if 2 * half < c:
                    blk = 2 * half
                    sc = jnp.where((row_cc // blk) == (col_cc // blk), sc, 0.0)
                scores = sc if scores is None else scores + sc
            all_scores.append(scores)
            yield
        for h in range(HGRN_HEADS):
            sl = slice(h * LANES, (h + 1) * LANES)
            s_mat = s_mats[h]
            v_h = v[:, sl]
            o_h = (_mm(_nt, q_chunk[:, sl], s_mat, HGRN_PREC) + _mm(_nn, all_scores[h], v_h, HGRN_PREC)
                   + jnp.sum(qk[:, sl], axis=-1, keepdims=True) * v_h)
            ms = jnp.mean(o_h * o_h, axis=-1, keepdims=True)
            outs.append(o_h * lax.rsqrt(ms + NORM_EPS))
            upd = _mm(_tn, v_h, k_end[:, sl], HGRN_PREC)
            new_states.append(s_mat * jnp.exp(tot[:, sl]) + upd)
            yield
        o = jnp.concatenate(outs, axis=1) * nw_ref[...]
        return o.astype(o_ref.dtype), new_states

    def chunk(ci, carry):
        r0 = pl.multiple_of(ci * c, c)
        seqs = range(xd_ref.shape[0])
        xs_in = [xd_ref[g, pl.ds(r0, c), :] for g in seqs]
        states = [[state_ref[g, h] for h in range(HGRN_HEADS)] for g in seqs]
        results = _round_robin([one_chunk(xs_in[g], states[g]) for g in seqs])
        for g in seqs:
            out, new_states = results[g]
            for h in range(HGRN_HEADS):
                state_ref[g, h] = new_states[h]
            o_ref[g, pl.ds(r0, c), :] = out
        return carry

    lax.fori_loop(0, t_tile // c, chunk, 0)


def _hgrn_mixer(xd, lb_all, layer, norm_w, t_tile):
    b, s, _ = xd.shape
    c = HGRN_CHUNK
    depth = lb_all.shape[0]
    gs = HGRN_SEQS if b % HGRN_SEQS == 0 else 1
    cmat = jnp.tril(jnp.ones((c, c), F32)).astype(BF16)
    full = lambda shape: pl.BlockSpec(shape, lambda i, j: (0,) * len(shape))
    return pl.pallas_call(
        functools.partial(_hgrn_kernel, layer=layer),
        grid=(b // gs, s // t_tile),
        in_specs=[
            pl.BlockSpec((gs, t_tile, COL_D), lambda i, j: (i, j, 0)),
            full((depth, MIX_W)), full((1, MIX_W)), full(cmat.shape),
        ],
        out_specs=pl.BlockSpec((gs, t_tile, MIX_W), lambda i, j: (i, j, 0)),
        out_shape=jax.ShapeDtypeStruct((b, s, MIX_W), BF16),
        scratch_shapes=[pltpu.VMEM((gs, HGRN_HEADS, HGRN_EXPAND, HGRN_EXPAND), F32)],
        compiler_params=_cparams(("arbitrary", "arbitrary")),
        name="hgrn2_mixer",
    )(xd, lb_all.astype(F32), norm_w.reshape(1, MIX_W), cmat)


def _merge_kernel(ya_ref, yb_ref, yc_ref, yd_ref, g_ref, x_ref, wbr_ref, wout_ref, o_ref):
    merged = None
    for kk, y_ref in enumerate((ya_ref, yb_ref, yc_ref, yd_ref)):
        gate = _sigmoid(g_ref[:, kk * D_MODEL:(kk + 1) * D_MODEL].astype(F32))
        term = gate * _nn(y_ref[...], wbr_ref[kk])
        merged = term if merged is None else merged + term
    o_ref[...] = x_ref[...] + _nn(merged.astype(BF16), wout_ref[...])


def _merge(ya, yb, yc, yd, g, x, wbr_bf16, wout_bf16, tm):
    n = x.shape[0]
    ytile = pl.BlockSpec((tm, MIX_W), lambda i: (i, 0))
    return pl.pallas_call(
        _merge_kernel,
        grid=(n // tm,),
        in_specs=[
            ytile, ytile, ytile, ytile,
            pl.BlockSpec((tm, 4 * D_MODEL), lambda i: (i, 0)),
            pl.BlockSpec((tm, D_MODEL), lambda i: (i, 0)),
            pl.BlockSpec((4, MIX_W, D_MODEL), lambda i: (0, 0, 0)),
            pl.BlockSpec((D_MODEL, D_MODEL), lambda i: (0, 0)),
        ],
        out_specs=pl.BlockSpec((tm, D_MODEL), lambda i: (i, 0)),
        out_shape=jax.ShapeDtypeStruct((n, D_MODEL), F32),
        compiler_params=_cparams(("arbitrary",)),
        name="merge_out",
    )(ya, yb, yc, yd, g, x, wbr_bf16, wout_bf16)


def _ffn_kernel(x_ref, nf_ref, wup_ref, cw_ref, cb_ref, wd_ref, p_ref, np_ref, pproj_ref, pgate_ref,
                nfin_ref, o_ref, cg_ref, cv_ref, ug_ref, uv_ref, act_ref, acc_ref, *, final_norm):
    tm = x_ref.shape[1]
    blk = FFN_BLOCK
    nb = FFN_NB

    @pl.when(pl.program_id(1) == 0)
    def _():
        cg_ref[...] = jnp.zeros_like(cg_ref)
        cv_ref[...] = jnp.zeros_like(cv_ref)

    x = x_ref[0]
    h = _rms(x, nf_ref[...]).astype(BF16)

    def conv(u, carry_ref, j, col0):
        prev = carry_ref[j]
        u1 = _shift_rows(u, prev, 1)
        u2 = _shift_rows(u, prev, 2)
        carry_ref[j] = u[tm - SUBLANES:, :]
        cw = cw_ref[:, col0:col0 + blk]
        return cb_ref[:, col0:col0 + blk] + cw[2:3, :] * u + cw[1:2, :] * u1 + cw[0:1, :] * u2

    def up(j):
        ug_ref[j % 2] = _nn(h, wup_ref[:, j * blk:(j + 1) * blk])
        uv_ref[j % 2] = _nn(h, wup_ref[:, D_FF + j * blk:D_FF + (j + 1) * blk])

    group_ends = [e for e in FFN_GROUP_ENDS if e < nb] + [nb]
    group_start = {end: start for start, end in zip([0] + group_ends[:-1], group_ends)}

    def down(end):
        k0, k1 = group_start[end] * blk, end * blk
        d = _nn(act_ref[:, k0:k1], wd_ref[k0:k1, :])
        if k0 == 0:
            acc_ref[...] = d
        else:
            acc_ref[...] += d

    up(0)
    for j in range(nb):
        if j + 1 < nb:
            up(j + 1)
        if j in group_start:
            down(j)
        ug = conv(ug_ref[j % 2], cg_ref, j, j * blk)
        uv = conv(uv_ref[j % 2], cv_ref, j, D_FF + j * blk)
        act_ref[:, j * blk:(j + 1) * blk] = (_gelu(ug) * uv).astype(BF16)
    down(nb)

    x2 = x + acc_ref[...]
    hp = _rms(x2, np_ref[...]).astype(BF16)
    gate = _sigmoid(_nn(hp, pgate_ref[...]))
    pp = _nn(p_ref[0].astype(BF16), pproj_ref[...])
    x3 = x2 + pp * gate
    if final_norm:
        x3 = _rms(x3, nfin_ref[...])
    o_ref[0] = x3


def _ffn_ple(x1, norm_ffn, wup_bf16, conv_w, conv_b, wdown_bf16, p_l, norm_ple, pproj_bf16,
             pgate_bf16, norm_final, final_norm, tm):
    b, s, d = x1.shape
    nb, blk = FFN_NB, FFN_BLOCK
    row = lambda a: a.reshape(1, -1)
    const = lambda shape: pl.BlockSpec(shape, lambda i, t: (0,) * len(shape),
                                       pipeline_mode=pl.Buffered(1))
    return pl.pallas_call(
        functools.partial(_ffn_kernel, final_norm=final_norm),
        grid=(b, s // tm),
        in_specs=[
            pl.BlockSpec((1, tm, d), lambda i, t: (i, t, 0)),
            const((1, d)),
            const((d, 2 * D_FF)), const((3, 2 * D_FF)), const((1, 2 * D_FF)), const((D_FF, d)),
            pl.BlockSpec((1, tm, PLE_DIM), lambda i, t: (i, t, 0)),
            const((1, d)),
            const((PLE_DIM, d)),
            const((d, d)),
            const((1, d)),
        ],
        out_specs=pl.BlockSpec((1, tm, d), lambda i, t: (i, t, 0)),
        out_shape=jax.ShapeDtypeStruct((b, s, d), F32),
        scratch_shapes=[pltpu.VMEM((nb, SUBLANES, blk), F32),
                        pltpu.VMEM((nb, SUBLANES, blk), F32),
                        pltpu.VMEM((2, tm, blk), F32),
                        pltpu.VMEM((2, tm, blk), F32),
                        pltpu.VMEM((tm, D_FF), BF16),
                        pltpu.VMEM((tm, d), F32)],
        compiler_params=_cparams(("arbitrary", "arbitrary")),
        name="ffn_ple",
    )(x1, row(norm_ffn), wup_bf16, conv_w, row(conv_b), wdown_bf16, p_l, row(norm_ple), pproj_bf16,
      pgate_bf16, row(norm_final))


def _pick_tile(total, want):
    t = min(total, want)
    while total % t:
        t //= 2
    return t


def kernel(x, p, norm_mix, w_in, pool_w, pool_scale, rwkv_mu, rwkv_w0, rwkv_w2, rwkv_a0, rwkv_a2, rwkv_g2, rwkv_kk, rwkv_ka, rwkv_rk, rwkv_ln_w, rwkv_ln_b, sg_ln_w, sg_ln_b, sg_w, sg_b, hgrn_lb, hgrn_norm, w_branch, w_out, norm_ffn, ffn_up, ffn_conv, ffn_conv_b, ffn_down, norm_ple, ple_proj, ple_gate, norm_final):
    b, s, d = x.shape
    depth = w_in.shape[0]
    n = b * s
    ts = _pick_tile(s, 512)
    for l in range(depth):
        w_l = w_in[l].astype(BF16)
        x2d = x.reshape(n, d)
        gain = norm_mix[l]
        xa, xb, xc, xd, g = _in_proj(x2d, gain, w_l, (COL_A, COL_B, COL_C, COL_D, 4 * D_MODEL),
                                     (BF16, F32, BF16, F32, BF16), _pick_tile(n, 512))
        ya = _pool_mixer(xa.reshape(b, s, COL_A), pool_w[l].astype(BF16), pool_scale[l], ts)
        yb = _rwkv_mixer(xb.reshape(b, s, COL_B), rwkv_mu[l], rwkv_w0[l], rwkv_w2[l], rwkv_a0[l],
                         rwkv_a2[l], rwkv_g2[l], rwkv_kk[l], rwkv_ka[l], rwkv_rk[l].reshape(-1),
                         rwkv_ln_w[l], rwkv_ln_b[l], _pick_tile(s, 256))
        yc = _sg_mixer(xc, sg_ln_w[l], sg_ln_b[l], sg_w[l], sg_b[l], _pick_tile(n, 512))
        yd = _hgrn_mixer(xd.reshape(b, s, COL_D), hgrn_lb, l, hgrn_norm[l], _pick_tile(s, 256))
        x1 = _merge(ya.reshape(n, MIX_W), yb.reshape(n, MIX_W), yc, yd.reshape(n, MIX_W), g, x2d,
                    w_branch[l].astype(BF16), w_out[l].astype(BF16), _pick_tile(n, 512))
        x = _ffn_ple(x1.reshape(b, s, d), norm_ffn[l], ffn_up[l].astype(BF16), ffn_conv[l],
                     ffn_conv_b[l], ffn_down[l].astype(BF16), p[l], norm_ple[l],
                     ple_proj[l].astype(BF16), ple_gate[l].astype(BF16), norm_final,
                     l == depth - 1, _pick_tile(s, 512))
    return x
```

```python
import functools

import jax
import jax.numpy as jnp
from jax import lax
from jax.experimental import pallas as pl
from jax.experimental.pallas import tpu as pltpu

F32 = jnp.float32
BF16 = jnp.bfloat16

D_MODEL = 1024
MIX_W = 512
PLE_DIM = 256
POOL_WINDOWS = (2, 4, 8, 16)
POOL_GROUP = 128
POOL_HALO = 16
RWKV_HEAD = 64
RWKV_IN = 1792
RWKV_LN_EPS = 64e-5
SG_CHUNK = 128
SG_GROUPS = 4
SG_LN_EPS = 1e-5
HGRN_HEADS = 4
HGRN_EXPAND = 128
GATE_FLOOR = 1e-30
D_FF = 2816
NORM_EPS = 1e-6
COL_A, COL_B, COL_C, COL_D = 512, 1792, 1024, 1536

LANES = 128
SUBLANES = 8
VMEM_LIMIT_BYTES = 56 * 1024 * 1024

RWKV_CHUNK = 64
HGRN_CHUNK = 64

FFN_BLOCK = 256
FFN_NB = D_FF // FFN_BLOCK
FFN_GROUP_ENDS = (4, 8, 10)


def _cparams(sem):
    return pltpu.CompilerParams(dimension_semantics=sem, vmem_limit_bytes=VMEM_LIMIT_BYTES)


def _nn(a, b):
    return lax.dot_general(a, b, (((1,), (0,)), ((), ())), preferred_element_type=F32)


def _nt(a, b):
    return lax.dot_general(a, b, (((1,), (1,)), ((), ())), preferred_element_type=F32)


def _tn(a, b):
    return lax.dot_general(a, b, (((0,), (0,)), ((), ())), preferred_element_type=F32)


def _split(a, terms):
    out = []
    rem = a
    for _ in range(terms - 1):
        hi = rem.astype(BF16)
        out.append(hi)
        rem = rem - hi.astype(F32)
    out.append(rem.astype(BF16))
    return out


def _mm(dot, a, b, prec):
    if prec == 1:
        return dot(a.astype(BF16), b.astype(BF16))
    a_hi, a_lo = _split(a, 2)
    b_hi, b_lo = _split(b, 2)
    return dot(a_hi, b_hi) + (dot(a_hi, b_lo) + dot(a_lo, b_hi))


def _mm_exact_rhs(a, b_bf16, terms):
    acc = None
    for part in _split(a, terms):
        t = _nn(part, b_bf16)
        acc = t if acc is None else acc + t
    return acc


def _head_sums(arrays, ones_pair_bf16, terms=2):
    c = arrays[0].shape[0]
    slabs = [a[:, p * LANES:(p + 1) * LANES] for a in arrays for p in range(MIX_W // LANES)]
    s = _mm_exact_rhs(jnp.concatenate(slabs, axis=0), ones_pair_bf16, terms)
    per = MIX_W // LANES
    return [jnp.concatenate([s[(i * per + p) * c:(i * per + p + 1) * c, :] for p in range(per)], axis=1)
            for i in range(len(arrays))]


def _rms(x, gain):
    return x * lax.rsqrt(jnp.mean(x * x, axis=-1, keepdims=True) + NORM_EPS) * gain


def _sigmoid(x):
    return 1.0 / (1.0 + jnp.exp(-x))


def _gelu(x):
    return 0.5 * x * (1.0 + jnp.tanh(0.7978845608028654 * (x + 0.044715 * (x * x * x))))


def _iota(shape, dim):
    return lax.broadcasted_iota(jnp.int32, shape, dim)


def _round_robin(gens):
    results = [None] * len(gens)
    live = list(range(len(gens)))
    while live:
        for idx in list(live):
            try:
                next(gens[idx])
            except StopIteration as stop:
                results[idx] = stop.value
                live.remove(idx)
    return results


def _shift_rows(x, prev_rows, n):
    rolled = pltpu.roll(x, n, axis=0)
    row = _iota(x.shape, 0)
    out = rolled
    for j in range(n):
        out = jnp.where(row == j, prev_rows[SUBLANES - n + j:SUBLANES - n + j + 1, :], out)
    return out


IN_PROJ_CHUNK = 512


def _pool_mix(a, carry_ref, first_pos, w_ref, scale_ref):
    t_tile = a.shape[0]
    ext = jnp.concatenate([carry_ref[...], a], axis=0)
    carry_ref[...] = a[t_tile - POOL_HALO:, :]
    count = (first_pos + _iota((t_tile, 1), 0) + 1).astype(F32)
    outs = []
    for gi, win in enumerate(POOL_WINDOWS):
        col = slice(gi * POOL_GROUP, (gi + 1) * POOL_GROUP)
        s = ext[:, col]
        span = 1
        while span < win:
            s = s + pltpu.roll(s, span, axis=0)
            span *= 2
        mean = s[POOL_HALO:, :] / jnp.minimum(count, float(win))
        diff = (mean - a[:, col]).astype(BF16)
        outs.append(_nn(diff, w_ref[gi]))
    return jnp.concatenate(outs, axis=1) * scale_ref[...]


def _sg_mix(xc, lnw_ref, lnb_ref, ws_ref, bs_ref, o_ref):
    t_tile = xc.shape[0]
    z = _gelu(xc)
    u = z[:, :MIX_W]
    v = z[:, MIX_W:]
    mean = jnp.mean(v, axis=-1, keepdims=True)
    vc = v - mean
    var = jnp.mean(vc * vc, axis=-1, keepdims=True)
    vn = (vc * lax.rsqrt(var + SG_LN_EPS) * lnw_ref[...] + lnb_ref[...]).astype(BF16)
    causal = _iota((SG_CHUNK, SG_CHUNK), 0) >= _iota((SG_CHUNK, SG_CHUNK), 1)
    w_m = [jnp.where(causal, ws_ref[g], 0.0).astype(BF16) for g in range(SG_GROUPS)]
    for n in range(t_tile // SG_CHUNK):
        rows = slice(n * SG_CHUNK, (n + 1) * SG_CHUNK)
        mixed = jnp.concatenate(
            [_nn(w_m[g], vn[rows, g * LANES:(g + 1) * LANES]) for g in range(SG_GROUPS)], axis=1)
        o_ref[0, rows, :] = (u[rows, :] * (mixed + bs_ref[...])).astype(o_ref.dtype)


def _in_proj_kernel(x_ref, g_ref, w_ref, pw_ref, pscale_ref, lnw_ref, lnb_ref, ws_ref, bs_ref,
                    ya_ref, xb_ref, yc_ref, xd_ref, gate_ref, carry_ref):
    tm = x_ref.shape[1]

    @pl.when(pl.program_id(1) == 0)
    def _():
        carry_ref[...] = jnp.zeros_like(carry_ref)

    h = _rms(x_ref[0], g_ref[...]).astype(BF16)

    def proj(c0, c1):
        parts = [_nn(h, w_ref[:, k0:min(k0 + IN_PROJ_CHUNK, c1)]) for k0 in range(c0, c1, IN_PROJ_CHUNK)]
        return parts[0] if len(parts) == 1 else jnp.concatenate(parts, axis=1)

    o0, o1, o2, o3 = COL_A, COL_A + COL_B, COL_A + COL_B + COL_C, COL_A + COL_B + COL_C + COL_D
    xa = proj(0, o0)
    xc = proj(o1, o2)
    for out_ref, c0, c1 in ((xb_ref, o0, o1), (xd_ref, o2, o3), (gate_ref, o3, w_ref.shape[1])):
        for k0 in range(c0, c1, IN_PROJ_CHUNK):
            k1 = min(k0 + IN_PROJ_CHUNK, c1)
            out_ref[0, :, k0 - c0:k1 - c0] = _nn(h, w_ref[:, k0:k1]).astype(out_ref.dtype)
    ya = _pool_mix(xa, carry_ref, pl.program_id(1) * tm, pw_ref, pscale_ref)
    ya_ref[0] = ya.astype(ya_ref.dtype)
    _sg_mix(xc, lnw_ref, lnb_ref, ws_ref, bs_ref, yc_ref)


def _in_proj_mix(x, gain, w_bf16, pool_w_bf16, pool_scale, sg_ln_w, sg_ln_b, sg_w, sg_b, tm):
    b, s, d = x.shape
    m = w_bf16.shape[1]
    b_full = jnp.repeat(jnp.swapaxes(sg_b, 0, 1), MIX_W // SG_GROUPS, axis=1)
    const = lambda shape: pl.BlockSpec(shape, lambda i, t: (0,) * len(shape),
                                       pipeline_mode=pl.Buffered(1))
    tile = lambda w: pl.BlockSpec((1, tm, w), lambda i, t: (i, t, 0))
    widths = (MIX_W, COL_B, MIX_W, COL_D, m - (COL_A + COL_B + COL_C + COL_D))
    dtypes = (BF16, F32, BF16, F32, BF16)
    return pl.pallas_call(
        _in_proj_kernel,
        grid=(b, s // tm),
        in_specs=[
            tile(d), const((1, d)), const((d, m)),
            const((len(POOL_WINDOWS), POOL_GROUP, POOL_GROUP)), const((1, MIX_W)),
            const((1, MIX_W)), const((1, MIX_W)), const((SG_GROUPS, SG_CHUNK, SG_CHUNK)),
            const((SG_CHUNK, MIX_W)),
        ],
        out_specs=[tile(w) for w in widths],
        out_shape=[jax.ShapeDtypeStruct((b, s, w), dt) for w, dt in zip(widths, dtypes)],
        scratch_shapes=[pltpu.VMEM((POOL_HALO, MIX_W), F32)],
        compiler_params=_cparams(("arbitrary", "arbitrary")),
        name="in_proj_mix",
    )(x, gain.reshape(1, d), w_bf16, pool_w_bf16, pool_scale.reshape(1, MIX_W),
      sg_ln_w.reshape(1, MIX_W), sg_ln_b.reshape(1, MIX_W), sg_w, b_full)


RWKV_DECAY_SCALE = 0.6065306597126334
RWKV_CUM_TERMS = 2
RWKV_SUM_TERMS = 1
RWKV_SEQS = 4


def _rwkv_kernel(xb_ref, mu_ref, w0_ref, wla_ref, a0_ref, g2_ref, kk_ref, ka_ref, rk_ref,
                 lnw_ref, lnb_ref, hsum_ref, tril_ref, o_ref, carry_ref, state_ref):
    t_tile = xb_ref.shape[1]
    c = RWKV_CHUNK
    hd = RWKV_HEAD

    @pl.when(pl.program_id(1) == 0)
    def _():
        carry_ref[...] = jnp.zeros_like(carry_ref)
        state_ref[...] = jnp.zeros_like(state_ref)

    lane = _iota((1, LANES), 1)
    head0 = lane < hd
    col_s = jnp.bitwise_and(_iota((c, LANES), 1), hd - 1)
    row_t = _iota((c, LANES), 0)
    strict = row_t > col_s
    incl = row_t >= col_s
    bd_mask = (_iota((LANES, LANES), 0) < hd) == (_iota((LANES, LANES), 1) < hd)
    sys_w = 2 * LANES
    lane_head = _iota((c, sys_w), 1) // hd

    eye_sys = (_iota((c, sys_w), 0) == jnp.bitwise_and(_iota((c, sys_w), 1), hd - 1)).astype(F32)
    npair = MIX_W // LANES
    nsys = MIX_W // sys_w
    nsteps = c.bit_length() - 1

    def block_diag(m):
        zero = jnp.zeros((), m.dtype)
        return jnp.concatenate([jnp.where(lane_head == j, m, zero) for j in range(sys_w // hd)], axis=0)

    def pair_stack(m):
        zero = jnp.zeros((), m.dtype)
        return jnp.concatenate([jnp.where(head0, m, zero), jnp.where(head0, zero, m)], axis=0)

    def prepare(x, prev_rows):
        xprev = _shift_rows(x, prev_rows, 1)
        xs = x + mu_ref[...] * (xprev - x)
        r = xs[:, 0:MIX_W]
        k = xs[:, MIX_W:2 * MIX_W]
        v = xs[:, 2 * MIX_W:3 * MIX_W]
        slab = xs[:, 3 * MIX_W:3 * MIX_W + LANES]
        lg = xs[:, 3 * MIX_W + LANES:]
        slab = jnp.where(head0, jnp.tanh(slab), slab)
        low = _nn(slab.astype(BF16), wla_ref[...])
        gate = _nn(_sigmoid(lg).astype(BF16), g2_ref[...])
        yield
        ld = -RWKV_DECAY_SCALE * _sigmoid(w0_ref[...] + low[:, :MIX_W])
        cum = _mm_exact_rhs_left(tril_ref[...], ld, RWKV_CUM_TERMS)
        eta = _sigmoid(a0_ref[...] + low[:, MIX_W:])
        kkv = k * kk_ref[...]
        k2 = k * (1.0 + (eta - 1.0) * ka_ref[...])
        nrm2, rk_sum = _head_sums([kkv * kkv, r * k2 * rk_ref[...]], hsum_ref[...], RWKV_SUM_TERMS)
        yield
        kkn = kkv / jnp.maximum(jnp.sqrt(nrm2), 1e-12)
        tot = cum[c - 1:c, :]
        e_neg = jnp.exp(-cum)
        rt = r * jnp.exp(cum)
        at = -kkn * jnp.exp(cum - ld)
        b_vec = kkn * eta
        bt = b_vec * e_neg
        kt = k2 * e_neg
        e_end = jnp.exp(tot - cum)
        b_end = b_vec * e_end
        k_end = k2 * e_end
        p_tot = jnp.exp(tot)

        at_b, rt_b = at.astype(BF16), rt.astype(BF16)
        yield
        a_ab, a_ak, a_rbk, v_bd = [], [], [], []
        for p in range(npair):
            sl = slice(p * LANES, (p + 1) * LANES)
            lhs = jnp.concatenate([at_b[:, sl], rt_b[:, sl]], axis=0)
            rhs = jnp.concatenate([pair_stack(bt[:, sl].astype(BF16)),
                                   pair_stack(kt[:, sl].astype(BF16))], axis=0)
            sc = _nt(lhs, rhs)
            a_ab.append(jnp.where(strict, sc[0:c, 0:LANES], 0.0))
            a_ak.append(jnp.where(strict, sc[0:c, LANES:], 0.0).astype(BF16))
            a_rbk.append(jnp.concatenate([jnp.where(incl, sc[c:, 0:LANES], 0.0),
                                          jnp.where(incl, sc[c:, LANES:], 0.0)], axis=1).astype(BF16))
            v_bd.append(pair_stack(v[:, sl].astype(BF16)))
            yield
        pws = [jnp.concatenate([a_ab[2 * q], a_ab[2 * q + 1]], axis=1).astype(BF16) for q in range(nsys)]
        tis = [pws[q].astype(F32) + eye_sys for q in range(nsys)]
        pws = [_nn(pws[q], block_diag(pws[q])).astype(BF16) for q in range(nsys)]
        yield
        for i in range(1, nsteps):
            for q in range(nsys):
                wts = block_diag(pws[q])
                if i + 1 < nsteps:
                    both = _nn(jnp.concatenate([pws[q], tis[q].astype(BF16)], axis=0), wts)
                    pws[q] = both[0:c].astype(BF16)
                    tis[q] = tis[q] + both[c:]
                else:
                    tis[q] = tis[q] + _nn(tis[q].astype(BF16), wts)
            yield
        tinv = [t.astype(BF16) for t in tis]
        return dict(at=at_b, rt=rt_b, a_ak=a_ak, a_rbk=a_rbk, v_bd=v_bd, v=v.astype(BF16), tinv=tinv,
                    b_end=b_end.astype(BF16), k_end=k_end.astype(BF16), p_tot=p_tot, gate=gate,
                    bonus=rk_sum * v)

    def advance(pre, s_mats):
        ws = []
        for p in range(npair):
            sl = slice(p * LANES, (p + 1) * LANES)
            s_b = s_mats[p].astype(BF16)
            ws.append(_nt(pre["at"][:, sl], s_b) + _nn(pre["a_ak"][p], pre["v_bd"][p]))
            yield
        us = []
        for q in range(nsys):
            w_sys = jnp.concatenate([ws[2 * q], ws[2 * q + 1]], axis=1).astype(BF16)
            us.append(_nn(pre["tinv"][q], block_diag(w_sys)))
            yield
        ys, new_states = [], []
        for p in range(npair):
            sl = slice(p * LANES, (p + 1) * LANES)
            u_p = us[p // 2][:, (p % 2) * LANES:(p % 2 + 1) * LANES].astype(BF16)
            s_mat = s_mats[p]
            y_p = (_nt(pre["rt"][:, sl], s_mat.astype(BF16))
                   + _nn(pre["a_rbk"][p], jnp.concatenate([pair_stack(u_p), pre["v_bd"][p]], axis=0)))
            ys.append(y_p)
            upd = _tn(jnp.concatenate([u_p, pre["v"][:, sl]], axis=0),
                      jnp.concatenate([pre["b_end"][:, sl], pre["k_end"][:, sl]], axis=0))
            new_states.append(jnp.where(bd_mask, s_mat * pre["p_tot"][:, sl] + upd, 0.0))
            yield
        y = jnp.concatenate(ys, axis=1)
        inv_hd = 1.0 / hd
        mean = _head_sums([y], hsum_ref[...], RWKV_SUM_TERMS)[0] * inv_hd
        yield
        yc = y - mean
        var = _head_sums([yc * yc], hsum_ref[...], RWKV_SUM_TERMS)[0] * inv_hd
        yn = yc * lax.rsqrt(var + RWKV_LN_EPS) * lnw_ref[...] + lnb_ref[...]
        return ((yn + pre["bonus"]) * pre["gate"]).astype(o_ref.dtype), new_states

    seqs = range(xb_ref.shape[0])
    nchunk = t_tile // c

    def load_states():
        return [[state_ref[g, p] for p in range(npair)] for g in seqs]

    def store(results, r0):
        for g in seqs:
            out, new_states = results[g]
            for p in range(npair):
                state_ref[g, p] = new_states[p]
            o_ref[g, pl.ds(r0, c), :] = out

    pre0 = _round_robin([prepare(xb_ref[g, 0:c, :], carry_ref[g]) for g in seqs])

    def chunk(ci, pre):
        r0 = pl.multiple_of(ci * c, c)
        r1 = pl.multiple_of(r0 + c, c)
        nxt = [xb_ref[g, pl.ds(r1, c), :] for g in seqs]
        prv = [xb_ref[g, pl.ds(pl.multiple_of(r1 - SUBLANES, SUBLANES), SUBLANES), :] for g in seqs]
        states = load_states()
        res = _round_robin([advance(pre[g], states[g]) for g in seqs]
                           + [prepare(nxt[g], prv[g]) for g in seqs])
        store(res[:len(seqs)], r0)
        return res[len(seqs):]

    pre_last = lax.fori_loop(0, nchunk - 1, chunk, pre0)
    last = _round_robin([advance(pre_last[g], s) for g, s in zip(seqs, load_states())])
    store(last, (nchunk - 1) * c)
    for g in seqs:
        carry_ref[g] = xb_ref[g, t_tile - SUBLANES:, :]


def _mm_exact_rhs_left(l_bf16, a, terms=3):
    acc = None
    for part in _split(a, terms):
        t = _nn(l_bf16, part)
        acc = t if acc is None else acc + t
    return acc


def _rwkv_mixer(xb, mu, w0, w2, a0, a2, g2, k_k, k_a, r_k, ln_w, ln_b, t_tile):
    b, s, _ = xb.shape
    c = RWKV_CHUNK
    rank = w2.shape[0]
    wla = jnp.zeros((LANES, 2 * MIX_W), F32)
    wla = wla.at[:rank, :MIX_W].set(w2).at[rank:, MIX_W:].set(a2).astype(BF16)
    gs = RWKV_SEQS if b % RWKV_SEQS == 0 else 1
    hsum = jnp.kron(jnp.eye(LANES // RWKV_HEAD, dtype=F32),
                    jnp.ones((RWKV_HEAD, RWKV_HEAD), F32)).astype(BF16)
    tril = jnp.tril(jnp.ones((c, c), F32)).astype(BF16)
    row = lambda a: a.reshape(1, -1).astype(F32)
    full = lambda shape: pl.BlockSpec(shape, lambda i, j: (0,) * len(shape))
    return pl.pallas_call(
        _rwkv_kernel,
        grid=(b // gs, s // t_tile),
        in_specs=[
            pl.BlockSpec((gs, t_tile, RWKV_IN), lambda i, j: (i, j, 0)),
            full((1, RWKV_IN)), full((1, MIX_W)), full((LANES, 2 * MIX_W)), full((1, MIX_W)),
            full((LANES, MIX_W)), full((1, MIX_W)), full((1, MIX_W)), full((1, MIX_W)),
            full((1, MIX_W)), full((1, MIX_W)), full((LANES, LANES)), full((c, c)),
        ],
        out_specs=pl.BlockSpec((gs, t_tile, MIX_W), lambda i, j: (i, j, 0)),
        out_shape=jax.ShapeDtypeStruct((b, s, MIX_W), BF16),
        scratch_shapes=[pltpu.VMEM((gs, SUBLANES, RWKV_IN), F32),
                        pltpu.VMEM((gs, MIX_W // LANES, LANES, LANES), F32)],
        compiler_params=_cparams(("arbitrary", "arbitrary")),
        name="rwkv7_mixer",
    )(xb, row(mu), row(w0), wla, row(a0), g2.astype(BF16), row(k_k), row(k_a), row(r_k),
      row(ln_w), row(ln_b), hsum, tril)


HGRN_PREC = 1
HGRN_SEQS = 4


def _hgrn_levels(c):
    return [c >> (i + 1) for i in range(c.bit_length() - 1)]


def _block_mid_rows(a, half):
    rows, width = a.shape
    blk = 2 * half
    sub_row = _iota((SUBLANES, 1), 0)
    tiles = []
    for j in range(rows // SUBLANES):
        base = j * SUBLANES
        if blk >= SUBLANES:
            src = (base // blk) * blk + half - 1
            tiles.append(jnp.broadcast_to(a[src:src + 1, :], (SUBLANES, width)))
        else:
            tile = None
            for i in reversed(range(SUBLANES // blk)):
                src = base + i * blk + half - 1
                piece = jnp.broadcast_to(a[src:src + 1, :], (SUBLANES, width))
                tile = piece if tile is None else jnp.where(sub_row < (i + 1) * blk, piece, tile)
            tiles.append(tile)
    return jnp.concatenate(tiles, axis=0)


def _hgrn_kernel(xd_ref, lb_ref, nw_ref, cmat_ref, o_ref, state_ref, *, layer):
    t_tile = xd_ref.shape[1]
    c = HGRN_CHUNK
    levels = _hgrn_levels(c)

    @pl.when(pl.program_id(1) == 0)
    def _():
        state_ref[...] = jnp.zeros_like(state_ref)

    lb_all = lb_ref[...]
    e_all = jnp.exp(lb_all - jnp.max(lb_all, axis=0, keepdims=True))
    probs = e_all / jnp.sum(e_all, axis=0, keepdims=True)
    lb = jnp.sum(probs[0:layer + 1, :], axis=0, keepdims=True) - probs[0:1, :]
    row_c = _iota((c, 1), 0)
    row_cc = _iota((c, c), 0)
    col_cc = _iota((c, c), 1)

    def one_chunk(x, s_mats):
        qr = x[:, 0:MIX_W]
        q = qr * _sigmoid(qr)
        sig = _sigmoid(x[:, MIX_W:2 * MIX_W])
        v = x[:, 2 * MIX_W:]
        lg = jnp.log(jnp.maximum(lb + (1.0 - lb) * sig, GATE_FLOOR))
        key = (1.0 - lb) * (1.0 - sig)
        cum = _mm_exact_rhs_left(cmat_ref[...], lg)
        tot = cum[c - 1:c, :]
        q_chunk = q * jnp.exp(cum)
        k_end = key * jnp.exp(tot - cum)
        yield
        q_lv, k_lv = [], []
        zero_tile = jnp.zeros((SUBLANES, MIX_W), F32)
        for li, half in enumerate(levels):
            if half >= SUBLANES:
                q_tiles, k_tiles = [], []
                for base in range(0, c, SUBLANES):
                    rows = slice(base, base + SUBLANES)
                    mid_row = (base // (2 * half)) * (2 * half) + half - 1
                    mid = cum[mid_row:mid_row + 1, :]
                    if base & half:
                        q_tiles.append(q[rows] * jnp.exp(cum[rows] - mid))
                        k_tiles.append(zero_tile)
                    else:
                        q_tiles.append(zero_tile)
                        k_tiles.append(key[rows] * jnp.exp(mid - cum[rows]))
                q_lv.append(jnp.concatenate(q_tiles, axis=0))
                k_lv.append(jnp.concatenate(k_tiles, axis=0))
            else:
                dec = jnp.exp(-jnp.abs(cum - _block_mid_rows(cum, half)))
                second = jnp.bitwise_and(row_c, half) != 0
                q_lv.append(jnp.where(second, q * dec, 0.0))
                k_lv.append(jnp.where(second, 0.0, key * dec))
        qk = q * key
        outs, new_states, all_scores = [], [], []
        for h in range(HGRN_HEADS):
            sl = slice(h * LANES, (h + 1) * LANES)
            scores = None
            for li, half in enumerate(levels):
                sc = _mm(_nt, q_lv[li][:, sl], k_lv[li][:, sl], HGRN_PREC)
                if 2 * half < c:
                    blk = 2 * half
                    sc = jnp.where((row_cc // blk) == (col_cc // blk), sc, 0.0)
                scores = sc if scores is None else scores + sc
            all_scores.append(scores)
            yield
        for h in range(HGRN_HEADS):
            sl = slice(h * LANES, (h + 1) * LANES)
            s_mat = s_mats[h]
            v_h = v[:, sl]
            o_h = (_mm(_nt, q_chunk[:, sl], s_mat, HGRN_PREC) + _mm(_nn, all_scores[h], v_h, HGRN_PREC)
                   + jnp.sum(qk[:, sl], axis=-1, keepdims=True) * v_h)
            ms = jnp.mean(o_h * o_h, axis=-1, keepdims=True)
            outs.append(o_h * lax.rsqrt(ms + NORM_EPS))
            upd = _mm(_tn, v_h, k_end[:, sl], HGRN_PREC)
            new_states.append(s_mat * jnp.exp(tot[:, sl]) + upd)
            yield
        o = jnp.concatenate(outs, axis=1) * nw_ref[...]
        return o.astype(o_ref.dtype), new_states

    def chunk(ci, carry):
        r0 = pl.multiple_of(ci * c, c)
        seqs = range(xd_ref.shape[0])
        xs_in = [xd_ref[g, pl.ds(r0, c), :] for g in seqs]
        states = [[state_ref[g, h] for h in range(HGRN_HEADS)] for g in seqs]
        results = _round_robin([one_chunk(xs_in[g], states[g]) for g in seqs])
        for g in seqs:
            out, new_states = results[g]
            for h in range(HGRN_HEADS):
                state_ref[g, h] = new_states[h]
            o_ref[g, pl.ds(r0, c), :] = out
        return carry

    lax.fori_loop(0, t_tile // c, chunk, 0)


def _hgrn_mixer(xd, lb_all, layer, norm_w, t_tile):
    b, s, _ = xd.shape
    c = HGRN_CHUNK
    depth = lb_all.shape[0]
    gs = HGRN_SEQS if b % HGRN_SEQS == 0 else 1
    cmat = jnp.tril(jnp.ones((c, c), F32)).astype(BF16)
    full = lambda shape: pl.BlockSpec(shape, lambda i, j: (0,) * len(shape))
    return pl.pallas_call(
        functools.partial(_hgrn_kernel, layer=layer),
        grid=(b // gs, s // t_tile),
        in_specs=[
            pl.BlockSpec((gs, t_tile, COL_D), lambda i, j: (i, j, 0)),
            full((depth, MIX_W)), full((1, MIX_W)), full(cmat.shape),
        ],
        out_specs=pl.BlockSpec((gs, t_tile, MIX_W), lambda i, j: (i, j, 0)),
        out_shape=jax.ShapeDtypeStruct((b, s, MIX_W), BF16),
        scratch_shapes=[pltpu.VMEM((gs, HGRN_HEADS, HGRN_EXPAND, HGRN_EXPAND), F32)],
        compiler_params=_cparams(("arbitrary", "arbitrary")),
        name="hgrn2_mixer",
    )(xd, lb_all.astype(F32), norm_w.reshape(1, MIX_W), cmat)


def _merge_kernel(ya_ref, yb_ref, yc_ref, yd_ref, g_ref, x_ref, wbr_ref, wout_ref, o_ref):
    merged = None
    for kk, y_ref in enumerate((ya_ref, yb_ref, yc_ref, yd_ref)):
        gate = _sigmoid(g_ref[:, kk * D_MODEL:(kk + 1) * D_MODEL].astype(F32))
        term = gate * _nn(y_ref[...], wbr_ref[kk])
        merged = term if merged is None else merged + term
    o_ref[...] = x_ref[...] + _nn(merged.astype(BF16), wout_ref[...])


def _merge(ya, yb, yc, yd, g, x, wbr_bf16, wout_bf16, tm):
    n = x.shape[0]
    ytile = pl.BlockSpec((tm, MIX_W), lambda i: (i, 0))
    return pl.pallas_call(
        _merge_kernel,
        grid=(n // tm,),
        in_specs=[
            ytile, ytile, ytile, ytile,
            pl.BlockSpec((tm, 4 * D_MODEL), lambda i: (i, 0)),
            pl.BlockSpec((tm, D_MODEL), lambda i: (i, 0)),
            pl.BlockSpec((4, MIX_W, D_MODEL), lambda i: (0, 0, 0)),
            pl.BlockSpec((D_MODEL, D_MODEL), lambda i: (0, 0)),
        ],
        out_specs=pl.BlockSpec((tm, D_MODEL), lambda i: (i, 0)),
        out_shape=jax.ShapeDtypeStruct((n, D_MODEL), F32),
        compiler_params=_cparams(("arbitrary",)),
        name="merge_out",
    )(ya, yb, yc, yd, g, x, wbr_bf16, wout_bf16)


def _ffn_kernel(x_ref, nf_ref, wup_ref, cw_ref, cb_ref, wd_ref, p_ref, np_ref, pproj_ref, pgate_ref,
                nfin_ref, o_ref, cg_ref, cv_ref, ug_ref, uv_ref, act_ref, acc_ref, *, final_norm):
    tm = x_ref.shape[1]
    blk = FFN_BLOCK
    nb = FFN_NB

    @pl.when(pl.program_id(1) == 0)
    def _():
        cg_ref[...] = jnp.zeros_like(cg_ref)
        cv_ref[...] = jnp.zeros_like(cv_ref)

    x = x_ref[0]
    h = _rms(x, nf_ref[...]).astype(BF16)

    def conv(u, carry_ref, j, col0):
        prev = carry_ref[j]
        u1 = _shift_rows(u, prev, 1)
        u2 = _shift_rows(u, prev, 2)
        carry_ref[j] = u[tm - SUBLANES:, :]
        cw = cw_ref[:, col0:col0 + blk]
        return cb_ref[:, col0:col0 + blk] + cw[2:3, :] * u + cw[1:2, :] * u1 + cw[0:1, :] * u2

    def up(j):
        ug_ref[j % 2] = _nn(h, wup_ref[:, j * blk:(j + 1) * blk])
        uv_ref[j % 2] = _nn(h, wup_ref[:, D_FF + j * blk:D_FF + (j + 1) * blk])

    group_ends = [e for e in FFN_GROUP_ENDS if e < nb] + [nb]
    group_start = {end: start for start, end in zip([0] + group_ends[:-1], group_ends)}

    def down(end):
        k0, k1 = group_start[end] * blk, end * blk
        d = _nn(act_ref[:, k0:k1], wd_ref[k0:k1, :])
        if k0 == 0:
            acc_ref[...] = d
        else:
            acc_ref[...] += d

    up(0)
    for j in range(nb):
        if j + 1 < nb:
            up(j + 1)
        if j in group_start:
            down(j)
        ug = conv(ug_ref[j % 2], cg_ref, j, j * blk)
        uv = conv(uv_ref[j % 2], cv_ref, j, D_FF + j * blk)
        act_ref[:, j * blk:(j + 1) * blk] = (_gelu(ug) * uv).astype(BF16)
    down(nb)

    x2 = x + acc_ref[...]
    hp = _rms(x2, np_ref[...]).astype(BF16)
    gate = _sigmoid(_nn(hp, pgate_ref[...]))
    pp = _nn(p_ref[0].astype(BF16), pproj_ref[...])
    x3 = x2 + pp * gate
    if final_norm:
        x3 = _rms(x3, nfin_ref[...])
    o_ref[0] = x3


def _ffn_ple(x1, norm_ffn, wup_bf16, conv_w, conv_b, wdown_bf16, p_l, norm_ple, pproj_bf16,
             pgate_bf16, norm_final, final_norm, tm):
    b, s, d = x1.shape
    nb, blk = FFN_NB, FFN_BLOCK
    row = lambda a: a.reshape(1, -1)
    const = lambda shape: pl.BlockSpec(shape, lambda i, t: (0,) * len(shape),
                                       pipeline_mode=pl.Buffered(1))
    return pl.pallas_call(
        functools.partial(_ffn_kernel, final_norm=final_norm),
        grid=(b, s // tm),
        in_specs=[
            pl.BlockSpec((1, tm, d), lambda i, t: (i, t, 0)),
            const((1, d)),
            const((d, 2 * D_FF)), const((3, 2 * D_FF)), const((1, 2 * D_FF)), const((D_FF, d)),
            pl.BlockSpec((1, tm, PLE_DIM), lambda i, t: (i, t, 0)),
            const((1, d)),
            const((PLE_DIM, d)),
            const((d, d)),
            const((1, d)),
        ],
        out_specs=pl.BlockSpec((1, tm, d), lambda i, t: (i, t, 0)),
        out_shape=jax.ShapeDtypeStruct((b, s, d), F32),
        scratch_shapes=[pltpu.VMEM((nb, SUBLANES, blk), F32),
                        pltpu.VMEM((nb, SUBLANES, blk), F32),
                        pltpu.VMEM((2, tm, blk), F32),
                        pltpu.VMEM((2, tm, blk), F32),
                        pltpu.VMEM((tm, D_FF), BF16),
                        pltpu.VMEM((tm, d), F32)],
        compiler_params=_cparams(("arbitrary", "arbitrary")),
        name="ffn_ple",
    )(x1, row(norm_ffn), wup_bf16, conv_w, row(conv_b), wdown_bf16, p_l, row(norm_ple), pproj_bf16,
      pgate_bf16, row(norm_final))


def _pick_tile(total, want):
    t = min(total, want)
    while total % t:
        t //= 2
    return t


def kernel(x, p, norm_mix, w_in, pool_w, pool_scale, rwkv_mu, rwkv_w0, rwkv_w2, rwkv_a0, rwkv_a2, rwkv_g2, rwkv_kk, rwkv_ka, rwkv_rk, rwkv_ln_w, rwkv_ln_b, sg_ln_w, sg_ln_b, sg_w, sg_b, hgrn_lb, hgrn_norm, w_branch, w_out, norm_ffn, ffn_up, ffn_conv, ffn_conv_b, ffn_down, norm_ple, ple_proj, ple_gate, norm_final):
    b, s, d = x.shape
    depth = w_in.shape[0]
    n = b * s
    for l in range(depth):
        ya, xb, yc, xd, g = _in_proj_mix(x, norm_mix[l], w_in[l].astype(BF16), pool_w[l].astype(BF16),
                                         pool_scale[l], sg_ln_w[l], sg_ln_b[l], sg_w[l], sg_b[l],
                                         _pick_tile(s, 512))
        yb = _rwkv_mixer(xb, rwkv_mu[l], rwkv_w0[l], rwkv_w2[l], rwkv_a0[l],
                         rwkv_a2[l], rwkv_g2[l], rwkv_kk[l], rwkv_ka[l], rwkv_rk[l].reshape(-1),
                         rwkv_ln_w[l], rwkv_ln_b[l], _pick_tile(s, 256))
        yd = _hgrn_mixer(xd, hgrn_lb, l, hgrn_norm[l], _pick_tile(s, 256))
        x1 = _merge(ya.reshape(n, MIX_W), yb.reshape(n, MIX_W), yc.reshape(n, MIX_W),
                    yd.reshape(n, MIX_W), g.reshape(n, 4 * D_MODEL), x.reshape(n, d),
                    w_branch[l].astype(BF16), w_out[l].astype(BF16), _pick_tile(n, 512))
        x = _ffn_ple(x1.reshape(b, s, d), norm_ffn[l], ffn_up[l].astype(BF16), ffn_conv[l],
                     ffn_conv_b[l], ffn_down[l].astype(BF16), p[l], norm_ple[l],
                     ple_proj[l].astype(BF16), ple_gate[l].astype(BF16), norm_final,
                     l == depth - 1, _pick_tile(s, 512))
    return x
```

```python
import functools

import jax
import jax.numpy as jnp
from jax import lax
from jax.experimental import pallas as pl
from jax.experimental.pallas import tpu as pltpu

F32 = jnp.float32
BF16 = jnp.bfloat16

D_MODEL = 1024
MIX_W = 512
PLE_DIM = 256
POOL_WINDOWS = (2, 4, 8, 16)
POOL_GROUP = 128
POOL_HALO = 16
RWKV_HEAD = 64
RWKV_IN = 1792
RWKV_LN_EPS = 64e-5
SG_CHUNK = 128
SG_GROUPS = 4
SG_LN_EPS = 1e-5
HGRN_HEADS = 4
HGRN_EXPAND = 128
GATE_FLOOR = 1e-30
D_FF = 2816
NORM_EPS = 1e-6
COL_A, COL_B, COL_C, COL_D = 512, 1792, 1024, 1536

LANES = 128
SUBLANES = 8
VMEM_LIMIT_BYTES = 56 * 1024 * 1024

RWKV_CHUNK = 64
HGRN_CHUNK = 64

FFN_BLOCK = 256
FFN_NB = D_FF // FFN_BLOCK
FFN_GROUP_ENDS = (4, 8, 10)


def _cparams(sem):
    return pltpu.CompilerParams(dimension_semantics=sem, vmem_limit_bytes=VMEM_LIMIT_BYTES)


def _nn(a, b):
    return lax.dot_general(a, b, (((1,), (0,)), ((), ())), preferred_element_type=F32)


def _nt(a, b):
    return lax.dot_general(a, b, (((1,), (1,)), ((), ())), preferred_element_type=F32)


def _tn(a, b):
    return lax.dot_general(a, b, (((0,), (0,)), ((), ())), preferred_element_type=F32)


def _split(a, terms):
    out = []
    rem = a
    for _ in range(terms - 1):
        hi = rem.astype(BF16)
        out.append(hi)
        rem = rem - hi.astype(F32)
    out.append(rem.astype(BF16))
    return out


def _mm(dot, a, b, prec):
    if prec == 1:
        return dot(a.astype(BF16), b.astype(BF16))
    a_hi, a_lo = _split(a, 2)
    b_hi, b_lo = _split(b, 2)
    return dot(a_hi, b_hi) + (dot(a_hi, b_lo) + dot(a_lo, b_hi))


def _mm_exact_rhs(a, b_bf16, terms):
    acc = None
    for part in _split(a, terms):
        t = _nn(part, b_bf16)
        acc = t if acc is None else acc + t
    return acc


def _head_sums(arrays, ones_pair_bf16, terms=2):
    c = arrays[0].shape[0]
    slabs = [a[:, p * LANES:(p + 1) * LANES] for a in arrays for p in range(MIX_W // LANES)]
    s = _mm_exact_rhs(jnp.concatenate(slabs, axis=0), ones_pair_bf16, terms)
    per = MIX_W // LANES
    return [jnp.concatenate([s[(i * per + p) * c:(i * per + p + 1) * c, :] for p in range(per)], axis=1)
            for i in range(len(arrays))]


def _rms(x, gain):
    return x * lax.rsqrt(jnp.mean(x * x, axis=-1, keepdims=True) + NORM_EPS) * gain


def _sigmoid(x):
    return 1.0 / (1.0 + jnp.exp(-x))


def _gelu(x):
    return 0.5 * x * (1.0 + jnp.tanh(0.7978845608028654 * (x + 0.044715 * (x * x * x))))


def _iota(shape, dim):
    return lax.broadcasted_iota(jnp.int32, shape, dim)


def _round_robin(gens):
    results = [None] * len(gens)
    live = list(range(len(gens)))
    while live:
        for idx in list(live):
            try:
                next(gens[idx])
            except StopIteration as stop:
                results[idx] = stop.value
                live.remove(idx)
    return results


def _shift_rows(x, prev_rows, n):
    rolled = pltpu.roll(x, n, axis=0)
    row = _iota(x.shape, 0)
    out = rolled
    for j in range(n):
        src = prev_rows.shape[0] - n + j
        out = jnp.where(row == j, prev_rows[src:src + 1, :], out)
    return out


IN_PROJ_CHUNK = 512


def _pool_mix(a, carry_ref, first_pos, w_ref, scale_ref):
    t_tile = a.shape[0]
    ext = jnp.concatenate([carry_ref[...], a], axis=0)
    carry_ref[...] = a[t_tile - POOL_HALO:, :]
    count = (first_pos + _iota((t_tile, 1), 0) + 1).astype(F32)
    outs = []
    for gi, win in enumerate(POOL_WINDOWS):
        col = slice(gi * POOL_GROUP, (gi + 1) * POOL_GROUP)
        s = ext[:, col]
        span = 1
        while span < win:
            s = s + pltpu.roll(s, span, axis=0)
            span *= 2
        mean = s[POOL_HALO:, :] / jnp.minimum(count, float(win))
        diff = (mean - a[:, col]).astype(BF16)
        outs.append(_nn(diff, w_ref[gi]))
    return jnp.concatenate(outs, axis=1) * scale_ref[...]


def _sg_mix(xc, lnw_ref, lnb_ref, ws_ref, bs_ref, o_ref):
    t_tile = xc.shape[0]
    z = _gelu(xc)
    u = z[:, :MIX_W]
    v = z[:, MIX_W:]
    mean = jnp.mean(v, axis=-1, keepdims=True)
    vc = v - mean
    var = jnp.mean(vc * vc, axis=-1, keepdims=True)
    vn = (vc * lax.rsqrt(var + SG_LN_EPS) * lnw_ref[...] + lnb_ref[...]).astype(BF16)
    causal = _iota((SG_CHUNK, SG_CHUNK), 0) >= _iota((SG_CHUNK, SG_CHUNK), 1)
    w_m = [jnp.where(causal, ws_ref[g], 0.0).astype(BF16) for g in range(SG_GROUPS)]
    for n in range(t_tile // SG_CHUNK):
        rows = slice(n * SG_CHUNK, (n + 1) * SG_CHUNK)
        mixed = jnp.concatenate(
            [_nn(w_m[g], vn[rows, g * LANES:(g + 1) * LANES]) for g in range(SG_GROUPS)], axis=1)
        o_ref[0, rows, :] = (u[rows, :] * (mixed + bs_ref[...])).astype(o_ref.dtype)


def _in_proj_kernel(x_ref, g_ref, w_ref, pw_ref, pscale_ref, lnw_ref, lnb_ref, ws_ref, bs_ref,
                    ya_ref, xb_ref, yc_ref, xd_ref, gate_ref, carry_ref):
    tm = x_ref.shape[1]

    @pl.when(pl.program_id(1) == 0)
    def _():
        carry_ref[...] = jnp.zeros_like(carry_ref)

    h = _rms(x_ref[0], g_ref[...]).astype(BF16)

    def proj(c0, c1):
        parts = [_nn(h, w_ref[:, k0:min(k0 + IN_PROJ_CHUNK, c1)]) for k0 in range(c0, c1, IN_PROJ_CHUNK)]
        return parts[0] if len(parts) == 1 else jnp.concatenate(parts, axis=1)

    o0, o1, o2, o3 = COL_A, COL_A + COL_B, COL_A + COL_B + COL_C, COL_A + COL_B + COL_C + COL_D
    def emit(out_ref, c0, c1):
        for k0 in range(c0, c1, IN_PROJ_CHUNK):
            k1 = min(k0 + IN_PROJ_CHUNK, c1)
            out_ref[0, :, k0 - c0:k1 - c0] = _nn(h, w_ref[:, k0:k1]).astype(out_ref.dtype)

    xa = proj(0, o0)
    xc = proj(o1, o2)
    emit(xb_ref, o0, o1)
    emit(xd_ref, o2, o3)
    emit(gate_ref, o3, w_ref.shape[1])
    ya = _pool_mix(xa, carry_ref, pl.program_id(1) * tm, pw_ref, pscale_ref)
    ya_ref[0] = ya.astype(ya_ref.dtype)
    _sg_mix(xc, lnw_ref, lnb_ref, ws_ref, bs_ref, yc_ref)


def _in_proj_mix(x, gain, w_bf16, pool_w_bf16, pool_scale, sg_ln_w, sg_ln_b, sg_w, sg_b, tm):
    b, s, d = x.shape
    m = w_bf16.shape[1]
    b_full = jnp.repeat(jnp.swapaxes(sg_b, 0, 1), MIX_W // SG_GROUPS, axis=1)
    const = lambda shape: pl.BlockSpec(shape, lambda i, t: (0,) * len(shape),
                                       pipeline_mode=pl.Buffered(1))
    tile = lambda w: pl.BlockSpec((1, tm, w), lambda i, t: (i, t, 0))
    widths = (MIX_W, COL_B, MIX_W, COL_D, m - (COL_A + COL_B + COL_C + COL_D))
    dtypes = (BF16,) * len(widths)
    return pl.pallas_call(
        _in_proj_kernel,
        grid=(b, s // tm),
        in_specs=[
            tile(d), const((1, d)), const((d, m)),
            const((len(POOL_WINDOWS), POOL_GROUP, POOL_GROUP)), const((1, MIX_W)),
            const((1, MIX_W)), const((1, MIX_W)), const((SG_GROUPS, SG_CHUNK, SG_CHUNK)),
            const((SG_CHUNK, MIX_W)),
        ],
        out_specs=[tile(w) for w in widths],
        out_shape=[jax.ShapeDtypeStruct((b, s, w), dt) for w, dt in zip(widths, dtypes)],
        scratch_shapes=[pltpu.VMEM((POOL_HALO, MIX_W), F32)],
        compiler_params=_cparams(("arbitrary", "arbitrary")),
        name="in_proj_mix",
    )(x, gain.reshape(1, d), w_bf16, pool_w_bf16, pool_scale.reshape(1, MIX_W),
      sg_ln_w.reshape(1, MIX_W), sg_ln_b.reshape(1, MIX_W), sg_w, b_full)


RWKV_DECAY_SCALE = 0.6065306597126334
RWKV_CUM_TERMS = 2
RWKV_SUM_TERMS = 1
MIXER_SEQS = 4
MIXER_HALO = 16


def _rwkv_parts(mu_ref, w0_ref, wla_ref, a0_ref, g2_ref, kk_ref, ka_ref, rk_ref,
                lnw_ref, lnb_ref, hsum_ref, tril_ref, o_ref):
    c = RWKV_CHUNK
    hd = RWKV_HEAD

    lane = _iota((1, LANES), 1)
    head0 = lane < hd
    col_s = jnp.bitwise_and(_iota((c, LANES), 1), hd - 1)
    row_t = _iota((c, LANES), 0)
    strict = row_t > col_s
    incl = row_t >= col_s
    bd_mask = (_iota((LANES, LANES), 0) < hd) == (_iota((LANES, LANES), 1) < hd)
    sys_w = 2 * LANES
    lane_head = _iota((c, sys_w), 1) // hd

    eye_sys = (_iota((c, sys_w), 0) == jnp.bitwise_and(_iota((c, sys_w), 1), hd - 1)).astype(F32)
    npair = MIX_W // LANES
    nsys = MIX_W // sys_w
    nsteps = c.bit_length() - 1

    def block_diag(m):
        zero = jnp.zeros((), m.dtype)
        return jnp.concatenate([jnp.where(lane_head == j, m, zero) for j in range(sys_w // hd)], axis=0)

    def pair_stack(m):
        zero = jnp.zeros((), m.dtype)
        return jnp.concatenate([jnp.where(head0, m, zero), jnp.where(head0, zero, m)], axis=0)

    def prepare(x, prev_rows):
        x = x.astype(F32)
        xprev = _shift_rows(x, prev_rows.astype(F32), 1)
        xs = x + mu_ref[...] * (xprev - x)
        r = xs[:, 0:MIX_W]
        k = xs[:, MIX_W:2 * MIX_W]
        v = xs[:, 2 * MIX_W:3 * MIX_W]
        slab = xs[:, 3 * MIX_W:3 * MIX_W + LANES]
        lg = xs[:, 3 * MIX_W + LANES:]
        slab = jnp.where(head0, jnp.tanh(slab), slab)
        low = _nn(slab.astype(BF16), wla_ref[...])
        gate = _nn(_sigmoid(lg).astype(BF16), g2_ref[...])
        yield
        ld = -RWKV_DECAY_SCALE * _sigmoid(w0_ref[...] + low[:, :MIX_W])
        cum = _mm_exact_rhs_left(tril_ref[...], ld, RWKV_CUM_TERMS)
        eta = _sigmoid(a0_ref[...] + low[:, MIX_W:])
        kkv = k * kk_ref[...]
        k2 = k * (1.0 + (eta - 1.0) * ka_ref[...])
        nrm2, rk_sum = _head_sums([kkv * kkv, r * k2 * rk_ref[...]], hsum_ref[...], RWKV_SUM_TERMS)
        yield
        kkn = kkv * lax.rsqrt(jnp.maximum(nrm2, 1e-24))
        tot = cum[c - 1:c, :]
        e_neg = jnp.exp(-cum)
        rt = r * jnp.exp(cum)
        at = -kkn * jnp.exp(cum - ld)
        b_vec = kkn * eta
        bt = b_vec * e_neg
        kt = k2 * e_neg
        e_end = jnp.exp(tot - cum)
        b_end = b_vec * e_end
        k_end = k2 * e_end
        p_tot = jnp.exp(tot)

        at_b, rt_b = at.astype(BF16), rt.astype(BF16)
        yield
        a_ab, a_ak, a_rbk, v_bd = [], [], [], []
        for p in range(npair):
            sl = slice(p * LANES, (p + 1) * LANES)
            lhs = jnp.concatenate([at_b[:, sl], rt_b[:, sl]], axis=0)
            rhs = jnp.concatenate([pair_stack(bt[:, sl].astype(BF16)),
                                   pair_stack(kt[:, sl].astype(BF16))], axis=0)
            sc = _nt(lhs, rhs)
            a_ab.append(jnp.where(strict, sc[0:c, 0:LANES], 0.0))
            a_ak.append(jnp.where(strict, sc[0:c, LANES:], 0.0).astype(BF16))
            a_rbk.append(jnp.concatenate([jnp.where(incl, sc[c:, 0:LANES], 0.0),
                                          jnp.where(incl, sc[c:, LANES:], 0.0)], axis=1).astype(BF16))
            v_bd.append(pair_stack(v[:, sl].astype(BF16)))
            yield
        pws = [jnp.concatenate([a_ab[2 * q], a_ab[2 * q + 1]], axis=1).astype(BF16) for q in range(nsys)]
        tis = [pws[q].astype(F32) + eye_sys for q in range(nsys)]
        pws = [_nn(pws[q], block_diag(pws[q])).astype(BF16) for q in range(nsys)]
        yield
        for i in range(1, nsteps):
            for q in range(nsys):
                wts = block_diag(pws[q])
                if i + 1 < nsteps:
                    both = _nn(jnp.concatenate([pws[q], tis[q].astype(BF16)], axis=0), wts)
                    pws[q] = both[0:c].astype(BF16)
                    tis[q] = tis[q] + both[c:]
                else:
                    tis[q] = tis[q] + _nn(tis[q].astype(BF16), wts)
            yield
        tinv = [t.astype(BF16) for t in tis]
        return dict(at=at_b, rt=rt_b, a_ak=a_ak, a_rbk=a_rbk, v_bd=v_bd, v=v.astype(BF16), tinv=tinv,
                    b_end=b_end.astype(BF16), k_end=k_end.astype(BF16), p_tot=p_tot, gate=gate,
                    bonus=rk_sum * v)

    def advance(pre, s_mats):
        ws = []
        for p in range(npair):
            sl = slice(p * LANES, (p + 1) * LANES)
            s_b = s_mats[p].astype(BF16)
            ws.append(_nt(pre["at"][:, sl], s_b) + _nn(pre["a_ak"][p], pre["v_bd"][p]))
            yield
        us = []
        for q in range(nsys):
            w_sys = jnp.concatenate([ws[2 * q], ws[2 * q + 1]], axis=1).astype(BF16)
            us.append(_nn(pre["tinv"][q], block_diag(w_sys)))
            yield
        ys, new_states = [], []
        for p in range(npair):
            sl = slice(p * LANES, (p + 1) * LANES)
            u_p = us[p // 2][:, (p % 2) * LANES:(p % 2 + 1) * LANES].astype(BF16)
            s_mat = s_mats[p]
            y_p = (_nt(pre["rt"][:, sl], s_mat.astype(BF16))
                   + _nn(pre["a_rbk"][p], jnp.concatenate([pair_stack(u_p), pre["v_bd"][p]], axis=0)))
            ys.append(y_p)
            upd = _tn(jnp.concatenate([u_p, pre["v"][:, sl]], axis=0),
                      jnp.concatenate([pre["b_end"][:, sl], pre["k_end"][:, sl]], axis=0))
            new_states.append(jnp.where(bd_mask, s_mat * pre["p_tot"][:, sl] + upd, 0.0))
            yield
        y = jnp.concatenate(ys, axis=1)
        inv_hd = 1.0 / hd
        mean = _head_sums([y], hsum_ref[...], RWKV_SUM_TERMS)[0] * inv_hd
        yield
        yc = y - mean
        var = _head_sums([yc * yc], hsum_ref[...], RWKV_SUM_TERMS)[0] * inv_hd
        yn = yc * lax.rsqrt(var + RWKV_LN_EPS) * lnw_ref[...] + lnb_ref[...]
        return ((yn + pre["bonus"]) * pre["gate"]).astype(o_ref.dtype), new_states

    return prepare, advance


def _mm_exact_rhs_left(l_bf16, a, terms=3):
    acc = None
    for part in _split(a, terms):
        t = _nn(l_bf16, part)
        acc = t if acc is None else acc + t
    return acc


HGRN_PREC = 1


def _hgrn_levels(c):
    return [c >> (i + 1) for i in range(c.bit_length() - 1)]


def _block_mid_rows(a, half):
    rows, width = a.shape
    blk = 2 * half
    sub_row = _iota((SUBLANES, 1), 0)
    tiles = []
    for j in range(rows // SUBLANES):
        base = j * SUBLANES
        if blk >= SUBLANES:
            src = (base // blk) * blk + half - 1
            tiles.append(jnp.broadcast_to(a[src:src + 1, :], (SUBLANES, width)))
        else:
            tile = None
            for i in reversed(range(SUBLANES // blk)):
                src = base + i * blk + half - 1
                piece = jnp.broadcast_to(a[src:src + 1, :], (SUBLANES, width))
                tile = piece if tile is None else jnp.where(sub_row < (i + 1) * blk, piece, tile)
            tiles.append(tile)
    return jnp.concatenate(tiles, axis=0)


def _hgrn_parts(lb_ref, nw_ref, cmat_ref, o_ref, layer):
    c = HGRN_CHUNK
    levels = _hgrn_levels(c)

    lb_all = lb_ref[...]
    e_all = jnp.exp(lb_all - jnp.max(lb_all, axis=0, keepdims=True))
    probs = e_all / jnp.sum(e_all, axis=0, keepdims=True)
    lb = jnp.sum(probs[0:layer + 1, :], axis=0, keepdims=True) - probs[0:1, :]
    row_c = _iota((c, 1), 0)
    row_cc = _iota((c, c), 0)
    col_cc = _iota((c, c), 1)

    def one_chunk(x, s_mats):
        x = x.astype(F32)
        qr = x[:, 0:MIX_W]
        q = qr * _sigmoid(qr)
        sig = _sigmoid(x[:, MIX_W:2 * MIX_W])
        v = x[:, 2 * MIX_W:]
        lg = jnp.log(jnp.maximum(lb + (1.0 - lb) * sig, GATE_FLOOR))
        key = (1.0 - lb) * (1.0 - sig)
        cum = _mm_exact_rhs_left(cmat_ref[...], lg)
        tot = cum[c - 1:c, :]
        q_chunk = q * jnp.exp(cum)
        k_end = key * jnp.exp(tot - cum)
        yield
        q_lv, k_lv = [], []
        zero_tile = jnp.zeros((SUBLANES, MIX_W), F32)
        for li, half in enumerate(levels):
            if half >= SUBLANES:
                q_tiles, k_tiles = [], []
                for base in range(0, c, SUBLANES):
                    rows = slice(base, base + SUBLANES)
                    mid_row = (base // (2 * half)) * (2 * half) + half - 1
                    mid = cum[mid_row:mid_row + 1, :]
                    if base & half:
                        q_tiles.append(q[rows] * jnp.exp(cum[rows] - mid))
                        k_tiles.append(zero_tile)
                    else:
                        q_tiles.append(zero_tile)
                        k_tiles.append(key[rows] * jnp.exp(mid - cum[rows]))
                q_lv.append(jnp.concatenate(q_tiles, axis=0))
                k_lv.append(jnp.concatenate(k_tiles, axis=0))
            else:
                dec = jnp.exp(-jnp.abs(cum - _block_mid_rows(cum, half)))
                second = jnp.bitwise_and(row_c, half) != 0
                q_lv.append(jnp.where(second, q * dec, 0.0))
                k_lv.append(jnp.where(second, 0.0, key * dec))
        qk = q * key
        outs, new_states, all_scores = [], [], []
        for h in range(HGRN_HEADS):
            sl = slice(h * LANES, (h + 1) * LANES)
            scores = None
            for li, half in enumerate(levels):
                sc = _mm(_nt, q_lv[li][:, sl], k_lv[li][:, sl], HGRN_PREC)
                if 2 * half < c:
                    blk = 2 * half
                    sc = jnp.where((row_cc // blk) == (col_cc // blk), sc, 0.0)
                scores = sc if scores is None else scores + sc
            all_scores.append(scores)
            yield
        for h in range(HGRN_HEADS):
            sl = slice(h * LANES, (h + 1) * LANES)
            s_mat = s_mats[h]
            v_h = v[:, sl]
            o_h = (_mm(_nt, q_chunk[:, sl], s_mat, HGRN_PREC) + _mm(_nn, all_scores[h], v_h, HGRN_PREC)
                   + jnp.sum(qk[:, sl], axis=-1, keepdims=True) * v_h)
            ms = jnp.mean(o_h * o_h, axis=-1, keepdims=True)
            outs.append(o_h * lax.rsqrt(ms + NORM_EPS))
            upd = _mm(_tn, v_h, k_end[:, sl], HGRN_PREC)
            new_states.append(s_mat * jnp.exp(tot[:, sl]) + upd)
            yield
        o = jnp.concatenate(outs, axis=1) * nw_ref[...]
        return o.astype(o_ref.dtype), new_states

    return one_chunk


def _recurrent_kernel(xb_ref, mu_ref, w0_ref, wla_ref, a0_ref, g2_ref, kk_ref, ka_ref, rk_ref, lnw_ref,
                      lnb_ref, hsum_ref, tril_ref, xd_ref, lb_ref, nw_ref, yb_ref, yd_ref,
                      carry_ref, rstate_ref, hstate_ref, *, layer):
    t_tile = xb_ref.shape[1]
    c = RWKV_CHUNK
    assert HGRN_CHUNK == c
    nchunk = t_tile // c
    seqs = range(xb_ref.shape[0])
    npair = MIX_W // LANES
    halo = carry_ref.shape[1]

    @pl.when(pl.program_id(1) == 0)
    def _():
        carry_ref[...] = jnp.zeros_like(carry_ref)
        rstate_ref[...] = jnp.zeros_like(rstate_ref)
        hstate_ref[...] = jnp.zeros_like(hstate_ref)

    prepare, advance = _rwkv_parts(mu_ref, w0_ref, wla_ref, a0_ref, g2_ref, kk_ref, ka_ref, rk_ref,
                                   lnw_ref, lnb_ref, hsum_ref, tril_ref, yb_ref)
    hgrn_chunk = _hgrn_parts(lb_ref, nw_ref, tril_ref, yd_ref, layer)

    def step(r0, pre, nxt_gens):
        rstates = [[rstate_ref[g, p] for p in range(npair)] for g in seqs]
        hstates = [[hstate_ref[g, h] for h in range(HGRN_HEADS)] for g in seqs]
        xd_in = [xd_ref[g, pl.ds(r0, c), :] for g in seqs]
        res = _round_robin([advance(pre[g], rstates[g]) for g in seqs]
                           + [hgrn_chunk(xd_in[g], hstates[g]) for g in seqs] + nxt_gens)
        ns = len(seqs)
        for g in seqs:
            out, new_states = res[g]
            for p in range(npair):
                rstate_ref[g, p] = new_states[p]
            yb_ref[g, pl.ds(r0, c), :] = out
            out, new_states = res[ns + g]
            for h in range(HGRN_HEADS):
                hstate_ref[g, h] = new_states[h]
            yd_ref[g, pl.ds(r0, c), :] = out
        return res[2 * ns:]

    pre0 = _round_robin([prepare(xb_ref[g, 0:c, :], carry_ref[g]) for g in seqs])

    def chunk(ci, pre):
        r0 = pl.multiple_of(ci * c, c)
        r1 = pl.multiple_of(r0 + c, c)
        nxt = [xb_ref[g, pl.ds(r1, c), :] for g in seqs]
        prv = [xb_ref[g, pl.ds(pl.multiple_of(r1 - halo, halo), halo), :] for g in seqs]
        return step(r0, pre, [prepare(nxt[g], prv[g]) for g in seqs])

    pre_last = lax.fori_loop(0, nchunk - 1, chunk, pre0)
    step((nchunk - 1) * c, pre_last, [])
    for g in seqs:
        carry_ref[g] = xb_ref[g, t_tile - halo:, :].astype(F32)


def _recurrent_mixers(xb, xd, mu, w0, w2, a0, a2, g2, k_k, k_a, r_k, ln_w, ln_b, lb_all, layer,
                      hgrn_norm, t_tile):
    b, s, _ = xb.shape
    c = RWKV_CHUNK
    depth = lb_all.shape[0]
    rank = w2.shape[0]
    wla = jnp.zeros((LANES, 2 * MIX_W), F32)
    wla = wla.at[:rank, :MIX_W].set(w2).at[rank:, MIX_W:].set(a2).astype(BF16)
    gs = MIXER_SEQS if b % MIXER_SEQS == 0 else 1
    hsum = jnp.kron(jnp.eye(LANES // RWKV_HEAD, dtype=F32),
                    jnp.ones((RWKV_HEAD, RWKV_HEAD), F32)).astype(BF16)
    tril = jnp.tril(jnp.ones((c, c), F32)).astype(BF16)
    row = lambda a: a.reshape(1, -1).astype(F32)
    full = lambda shape: pl.BlockSpec(shape, lambda i, j: (0,) * len(shape))
    tile = lambda w: pl.BlockSpec((gs, t_tile, w), lambda i, j: (i, j, 0))
    return pl.pallas_call(
        functools.partial(_recurrent_kernel, layer=layer),
        grid=(b // gs, s // t_tile),
        in_specs=[
            tile(RWKV_IN),
            full((1, RWKV_IN)), full((1, MIX_W)), full((LANES, 2 * MIX_W)), full((1, MIX_W)),
            full((LANES, MIX_W)), full((1, MIX_W)), full((1, MIX_W)), full((1, MIX_W)),
            full((1, MIX_W)), full((1, MIX_W)), full((LANES, LANES)), full((c, c)),
            tile(COL_D), full((depth, MIX_W)), full((1, MIX_W)),
        ],
        out_specs=[tile(MIX_W), tile(MIX_W)],
        out_shape=[jax.ShapeDtypeStruct((b, s, MIX_W), BF16)] * 2,
        scratch_shapes=[pltpu.VMEM((gs, MIXER_HALO, RWKV_IN), F32),
                        pltpu.VMEM((gs, MIX_W // LANES, LANES, LANES), F32),
                        pltpu.VMEM((gs, HGRN_HEADS, HGRN_EXPAND, HGRN_EXPAND), F32)],
        compiler_params=_cparams(("arbitrary", "arbitrary")),
        name="recurrent_mixers",
    )(xb, row(mu), row(w0), wla, row(a0), g2.astype(BF16), row(k_k), row(k_a), row(r_k),
      row(ln_w), row(ln_b), hsum, tril, xd, lb_all.astype(F32), hgrn_norm.reshape(1, MIX_W))


def _merge_kernel(ya_ref, yb_ref, yc_ref, yd_ref, g_ref, x_ref, wbr_ref, wout_ref, o_ref):
    merged = None
    for kk, y_ref in enumerate((ya_ref, yb_ref, yc_ref, yd_ref)):
        gate = _sigmoid(g_ref[:, kk * D_MODEL:(kk + 1) * D_MODEL].astype(F32))
        term = gate * _nn(y_ref[...], wbr_ref[kk])
        merged = term if merged is None else merged + term
    o_ref[...] = x_ref[...] + _nn(merged.astype(BF16), wout_ref[...])


def _merge(ya, yb, yc, yd, g, x, wbr_bf16, wout_bf16, tm):
    n = x.shape[0]
    ytile = pl.BlockSpec((tm, MIX_W), lambda i: (i, 0))
    return pl.pallas_call(
        _merge_kernel,
        grid=(n // tm,),
        in_specs=[
            ytile, ytile, ytile, ytile,
            pl.BlockSpec((tm, 4 * D_MODEL), lambda i: (i, 0)),
            pl.BlockSpec((tm, D_MODEL), lambda i: (i, 0)),
            pl.BlockSpec((4, MIX_W, D_MODEL), lambda i: (0, 0, 0)),
            pl.BlockSpec((D_MODEL, D_MODEL), lambda i: (0, 0)),
        ],
        out_specs=pl.BlockSpec((tm, D_MODEL), lambda i: (i, 0)),
        out_shape=jax.ShapeDtypeStruct((n, D_MODEL), F32),
        compiler_params=_cparams(("arbitrary",)),
        name="merge_out",
    )(ya, yb, yc, yd, g, x, wbr_bf16, wout_bf16)


def _ffn_kernel(x_ref, nf_ref, wup_ref, cw_ref, cb_ref, wd_ref, p_ref, np_ref, pproj_ref, pgate_ref,
                nfin_ref, o_ref, cg_ref, cv_ref, ug_ref, uv_ref, act_ref, acc_ref, *, final_norm):
    tm = x_ref.shape[1]
    blk = FFN_BLOCK
    nb = FFN_NB

    @pl.when(pl.program_id(1) == 0)
    def _():
        cg_ref[...] = jnp.zeros_like(cg_ref)
        cv_ref[...] = jnp.zeros_like(cv_ref)

    x = x_ref[0]
    pp = _nn(p_ref[0].astype(BF16), pproj_ref[...])
    h = _rms(x, nf_ref[...]).astype(BF16)

    def conv(u, carry_ref, j, col0):
        prev = carry_ref[j]
        u1 = _shift_rows(u, prev, 1)
        u2 = _shift_rows(u, prev, 2)
        carry_ref[j] = u[tm - SUBLANES:, :]
        cw = cw_ref[:, col0:col0 + blk]
        return cb_ref[:, col0:col0 + blk] + cw[2:3, :] * u + cw[1:2, :] * u1 + cw[0:1, :] * u2

    def up(j):
        ug_ref[j % 2] = _nn(h, wup_ref[:, j * blk:(j + 1) * blk])
        uv_ref[j % 2] = _nn(h, wup_ref[:, D_FF + j * blk:D_FF + (j + 1) * blk])

    group_ends = [e for e in FFN_GROUP_ENDS if e < nb] + [nb]
    group_start = {end: start for start, end in zip([0] + group_ends[:-1], group_ends)}

    def down(end):
        k0, k1 = group_start[end] * blk, end * blk
        d = _nn(act_ref[:, k0:k1], wd_ref[k0:k1, :])
        if k0 == 0:
            acc_ref[...] = d
        else:
            acc_ref[...] += d

    up(0)
    for j in range(nb):
        if j + 1 < nb:
            up(j + 1)
        if j in group_start:
            down(j)
        ug = conv(ug_ref[j % 2], cg_ref, j, j * blk)
        uv = conv(uv_ref[j % 2], cv_ref, j, D_FF + j * blk)
        act_ref[:, j * blk:(j + 1) * blk] = (_gelu(ug) * uv).astype(BF16)
    down(nb)

    x2 = x + acc_ref[...]
    hp = _rms(x2, np_ref[...]).astype(BF16)
    gate = _sigmoid(_nn(hp, pgate_ref[...]))
    x3 = x2 + pp * gate
    if final_norm:
        x3 = _rms(x3, nfin_ref[...])
    o_ref[0] = x3


def _ffn_ple(x1, norm_ffn, wup_bf16, conv_w, conv_b, wdown_bf16, p_l, norm_ple, pproj_bf16,
             pgate_bf16, norm_final, final_norm, tm):
    b, s, d = x1.shape
    nb, blk = FFN_NB, FFN_BLOCK
    row = lambda a: a.reshape(1, -1)
    const = lambda shape: pl.BlockSpec(shape, lambda i, t: (0,) * len(shape),
                                       pipeline_mode=pl.Buffered(1))
    return pl.pallas_call(
        functools.partial(_ffn_kernel, final_norm=final_norm),
        grid=(b, s // tm),
        in_specs=[
            pl.BlockSpec((1, tm, d), lambda i, t: (i, t, 0)),
            const((1, d)),
            const((d, 2 * D_FF)), const((3, 2 * D_FF)), const((1, 2 * D_FF)), const((D_FF, d)),
            pl.BlockSpec((1, tm, PLE_DIM), lambda i, t: (i, t, 0)),
            const((1, d)),
            const((PLE_DIM, d)),
            const((d, d)),
            const((1, d)),
        ],
        out_specs=pl.BlockSpec((1, tm, d), lambda i, t: (i, t, 0)),
        out_shape=jax.ShapeDtypeStruct((b, s, d), F32),
        scratch_shapes=[pltpu.VMEM((nb, SUBLANES, blk), F32),
                        pltpu.VMEM((nb, SUBLANES, blk), F32),
                        pltpu.VMEM((2, tm, blk), F32),
                        pltpu.VMEM((2, tm, blk), F32),
                        pltpu.VMEM((tm, D_FF), BF16),
                        pltpu.VMEM((tm, d), F32)],
        compiler_params=_cparams(("arbitrary", "arbitrary")),
        name="ffn_ple",
    )(x1, row(norm_ffn), wup_bf16, conv_w, row(conv_b), wdown_bf16, p_l, row(norm_ple), pproj_bf16,
      pgate_bf16, row(norm_final))


def _pick_tile(total, want):
    t = min(total, want)
    while total % t:
        t //= 2
    return t


def kernel(x, p, norm_mix, w_in, pool_w, pool_scale, rwkv_mu, rwkv_w0, rwkv_w2, rwkv_a0, rwkv_a2, rwkv_g2, rwkv_kk, rwkv_ka, rwkv_rk, rwkv_ln_w, rwkv_ln_b, sg_ln_w, sg_ln_b, sg_w, sg_b, hgrn_lb, hgrn_norm, w_branch, w_out, norm_ffn, ffn_up, ffn_conv, ffn_conv_b, ffn_down, norm_ple, ple_proj, ple_gate, norm_final):
    b, s, d = x.shape
    depth = w_in.shape[0]
    n = b * s
    for l in range(depth):
        ya, xb, yc, xd, g = _in_proj_mix(x, norm_mix[l], w_in[l].astype(BF16), pool_w[l].astype(BF16),
                                         pool_scale[l], sg_ln_w[l], sg_ln_b[l], sg_w[l], sg_b[l],
                                         _pick_tile(s, 512))
        yb, yd = _recurrent_mixers(xb, xd, rwkv_mu[l], rwkv_w0[l], rwkv_w2[l], rwkv_a0[l], rwkv_a2[l],
                                   rwkv_g2[l], rwkv_kk[l], rwkv_ka[l], rwkv_rk[l].reshape(-1),
                                   rwkv_ln_w[l], rwkv_ln_b[l], hgrn_lb, l, hgrn_norm[l],
                                   _pick_tile(s, 256))
        x1 = _merge(ya.reshape(n, MIX_W), yb.reshape(n, MIX_W), yc.reshape(n, MIX_W),
                    yd.reshape(n, MIX_W), g.reshape(n, 4 * D_MODEL), x.reshape(n, d),
                    w_branch[l].astype(BF16), w_out[l].astype(BF16), _pick_tile(n, 512))
        x = _ffn_ple(x1.reshape(b, s, d), norm_ffn[l], ffn_up[l].astype(BF16), ffn_conv[l],
                     ffn_conv_b[l], ffn_down[l].astype(BF16), p[l], norm_ple[l],
                     ple_proj[l].astype(BF16), ple_gate[l].astype(BF16), norm_final,
                     l == depth - 1, _pick_tile(s, 512))
    return x
```

```python
import functools

import jax
import jax.numpy as jnp
from jax import lax
from jax.experimental import pallas as pl
from jax.experimental.pallas import tpu as pltpu

F32 = jnp.float32
BF16 = jnp.bfloat16

D_MODEL = 1024
MIX_W = 512
PLE_DIM = 256
POOL_WINDOWS = (2, 4, 8, 16)
POOL_GROUP = 128
POOL_HALO = 16
RWKV_HEAD = 64
RWKV_IN = 1792
RWKV_LN_EPS = 64e-5
SG_CHUNK = 128
SG_GROUPS = 4
SG_LN_EPS = 1e-5
HGRN_HEADS = 4
HGRN_EXPAND = 128
GATE_FLOOR = 1e-30
D_FF = 2816
NORM_EPS = 1e-6
COL_A, COL_B, COL_C, COL_D = 512, 1792, 1024, 1536

LANES = 128
SUBLANES = 8
VMEM_LIMIT_BYTES = 56 * 1024 * 1024

RWKV_CHUNK = 64
HGRN_CHUNK = 64

FFN_BLOCK = 256
FFN_NB = D_FF // FFN_BLOCK
FFN_GROUP_ENDS = (4, 8, 10)


def _cparams(sem):
    return pltpu.CompilerParams(dimension_semantics=sem, vmem_limit_bytes=VMEM_LIMIT_BYTES)


def _nn(a, b):
    return lax.dot_general(a, b, (((1,), (0,)), ((), ())), preferred_element_type=F32)


def _nt(a, b):
    return lax.dot_general(a, b, (((1,), (1,)), ((), ())), preferred_element_type=F32)


def _tn(a, b):
    return lax.dot_general(a, b, (((0,), (0,)), ((), ())), preferred_element_type=F32)


def _split(a, terms):
    out = []
    rem = a
    for _ in range(terms - 1):
        hi = rem.astype(BF16)
        out.append(hi)
        rem = rem - hi.astype(F32)
    out.append(rem.astype(BF16))
    return out


def _mm(dot, a, b):
    return dot(a.astype(BF16), b.astype(BF16))


def _mm_exact_rhs(a, b_bf16, terms):
    acc = None
    for part in _split(a, terms):
        t = _nn(part, b_bf16)
        acc = t if acc is None else acc + t
    return acc


def _head_sums(arrays, ones_pair_bf16, terms=2):
    c = arrays[0].shape[0]
    slabs = [a[:, p * LANES:(p + 1) * LANES] for a in arrays for p in range(MIX_W // LANES)]
    s = _mm_exact_rhs(jnp.concatenate(slabs, axis=0), ones_pair_bf16, terms)
    per = MIX_W // LANES
    return [jnp.concatenate([s[(i * per + p) * c:(i * per + p + 1) * c, :] for p in range(per)], axis=1)
            for i in range(len(arrays))]


def _rms(x, gain):
    return x * lax.rsqrt(jnp.mean(x * x, axis=-1, keepdims=True) + NORM_EPS) * gain


def _sigmoid(x):
    return 1.0 / (1.0 + jnp.exp(-x))


def _gelu(x):
    return 0.5 * x * (1.0 + jnp.tanh(0.7978845608028654 * (x + 0.044715 * (x * x * x))))


def _iota(shape, dim):
    return lax.broadcasted_iota(jnp.int32, shape, dim)


def _round_robin(gens):
    results = [None] * len(gens)
    live = list(range(len(gens)))
    while live:
        for idx in list(live):
            try:
                next(gens[idx])
            except StopIteration as stop:
                results[idx] = stop.value
                live.remove(idx)
    return results


def _shift_rows(x, prev_rows, n):
    rolled = pltpu.roll(x, n, axis=0)
    row = _iota(x.shape, 0)
    out = rolled
    for j in range(n):
        src = prev_rows.shape[0] - n + j
        out = jnp.where(row == j, prev_rows[src:src + 1, :], out)
    return out


IN_PROJ_CHUNK = 512


def _pool_mix(a, carry_ref, first_pos, w_ref, scale_ref):
    t_tile = a.shape[0]
    ext = jnp.concatenate([carry_ref[...], a], axis=0)
    carry_ref[...] = a[t_tile - POOL_HALO:, :]
    count = (first_pos + _iota((t_tile, 1), 0) + 1).astype(F32)
    outs = []
    for gi, win in enumerate(POOL_WINDOWS):
        col = slice(gi * POOL_GROUP, (gi + 1) * POOL_GROUP)
        s = ext[:, col]
        span = 1
        while span < win:
            s = s + pltpu.roll(s, span, axis=0)
            span *= 2
        mean = s[POOL_HALO:, :] / jnp.minimum(count, float(win))
        diff = (mean - a[:, col]).astype(BF16)
        outs.append(_nn(diff, w_ref[gi]))
    return jnp.concatenate(outs, axis=1) * scale_ref[...]


def _sg_mix(xc, lnw_ref, lnb_ref, ws_ref, bs_ref, o_ref):
    t_tile = xc.shape[0]
    z = _gelu(xc)
    u = z[:, :MIX_W]
    v = z[:, MIX_W:]
    mean = jnp.mean(v, axis=-1, keepdims=True)
    vc = v - mean
    var = jnp.mean(vc * vc, axis=-1, keepdims=True)
    vn = (vc * lax.rsqrt(var + SG_LN_EPS) * lnw_ref[...] + lnb_ref[...]).astype(BF16)
    causal = _iota((SG_CHUNK, SG_CHUNK), 0) >= _iota((SG_CHUNK, SG_CHUNK), 1)
    w_m = [jnp.where(causal, ws_ref[g], 0.0).astype(BF16) for g in range(SG_GROUPS)]
    for n in range(t_tile // SG_CHUNK):
        rows = slice(n * SG_CHUNK, (n + 1) * SG_CHUNK)
        mixed = jnp.concatenate(
            [_nn(w_m[g], vn[rows, g * LANES:(g + 1) * LANES]) for g in range(SG_GROUPS)], axis=1)
        o_ref[0, rows, :] = (u[rows, :] * (mixed + bs_ref[...])).astype(o_ref.dtype)


def _in_proj_kernel(x_ref, g_ref, w_ref, pw_ref, pscale_ref, lnw_ref, lnb_ref, ws_ref, bs_ref,
                    ya_ref, xb_ref, yc_ref, xd_ref, gate_ref, carry_ref):
    tm = x_ref.shape[1]

    @pl.when(pl.program_id(1) == 0)
    def _():
        carry_ref[...] = jnp.zeros_like(carry_ref)

    h = _rms(x_ref[0], g_ref[...]).astype(BF16)

    def proj(c0, c1):
        parts = [_nn(h, w_ref[:, k0:min(k0 + IN_PROJ_CHUNK, c1)]) for k0 in range(c0, c1, IN_PROJ_CHUNK)]
        return parts[0] if len(parts) == 1 else jnp.concatenate(parts, axis=1)

    o0, o1, o2, o3 = COL_A, COL_A + COL_B, COL_A + COL_B + COL_C, COL_A + COL_B + COL_C + COL_D
    def emit(out_ref, c0, c1):
        for k0 in range(c0, c1, IN_PROJ_CHUNK):
            k1 = min(k0 + IN_PROJ_CHUNK, c1)
            out_ref[0, :, k0 - c0:k1 - c0] = _nn(h, w_ref[:, k0:k1]).astype(out_ref.dtype)

    xa = proj(0, o0)
    xc = proj(o1, o2)
    emit(xb_ref, o0, o1)
    emit(xd_ref, o2, o3)
    emit(gate_ref, o3, w_ref.shape[1])
    ya = _pool_mix(xa, carry_ref, pl.program_id(1) * tm, pw_ref, pscale_ref)
    ya_ref[0] = ya.astype(ya_ref.dtype)
    _sg_mix(xc, lnw_ref, lnb_ref, ws_ref, bs_ref, yc_ref)


def _in_proj_mix(x, gain, w_bf16, pool_w_bf16, pool_scale, sg_ln_w, sg_ln_b, sg_w, sg_b, tm):
    b, s, d = x.shape
    m = w_bf16.shape[1]
    b_full = jnp.repeat(jnp.swapaxes(sg_b, 0, 1), MIX_W // SG_GROUPS, axis=1)
    const = lambda shape: pl.BlockSpec(shape, lambda i, t: (0,) * len(shape),
                                       pipeline_mode=pl.Buffered(1))
    tile = lambda w: pl.BlockSpec((1, tm, w), lambda i, t: (i, t, 0))
    widths = (MIX_W, COL_B, MIX_W, COL_D, m - (COL_A + COL_B + COL_C + COL_D))
    dtypes = (BF16,) * len(widths)
    return pl.pallas_call(
        _in_proj_kernel,
        grid=(b, s // tm),
        in_specs=[
            tile(d), const((1, d)), const((d, m)),
            const((len(POOL_WINDOWS), POOL_GROUP, POOL_GROUP)), const((1, MIX_W)),
            const((1, MIX_W)), const((1, MIX_W)), const((SG_GROUPS, SG_CHUNK, SG_CHUNK)),
            const((SG_CHUNK, MIX_W)),
        ],
        out_specs=[tile(w) for w in widths],
        out_shape=[jax.ShapeDtypeStruct((b, s, w), dt) for w, dt in zip(widths, dtypes)],
        scratch_shapes=[pltpu.VMEM((POOL_HALO, MIX_W), F32)],
        compiler_params=_cparams(("arbitrary", "arbitrary")),
        name="in_proj_mix",
    )(x, gain.reshape(1, d), w_bf16, pool_w_bf16, pool_scale.reshape(1, MIX_W),
      sg_ln_w.reshape(1, MIX_W), sg_ln_b.reshape(1, MIX_W), sg_w, b_full)


RWKV_DECAY_SCALE = 0.6065306597126334
RWKV_NORM_FLOOR = 1e-12
RWKV_CUM_TERMS = 2
RWKV_SUM_TERMS = 1
MIXER_SEQS = 4
MIXER_HALO = 16


def _rwkv_parts(mu_ref, w0_ref, wla_ref, a0_ref, g2_ref, kk_ref, ka_ref, rk_ref,
                lnw_ref, lnb_ref, hsum_ref, tril_ref, o_ref):
    c = RWKV_CHUNK
    hd = RWKV_HEAD

    lane = _iota((1, LANES), 1)
    head0 = lane < hd
    col_s = jnp.bitwise_and(_iota((c, LANES), 1), hd - 1)
    row_t = _iota((c, LANES), 0)
    strict = row_t > col_s
    incl = row_t >= col_s
    bd_mask = (_iota((LANES, LANES), 0) < hd) == (_iota((LANES, LANES), 1) < hd)
    sys_w = 2 * LANES
    lane_head = _iota((c, sys_w), 1) // hd

    eye_sys = (_iota((c, sys_w), 0) == jnp.bitwise_and(_iota((c, sys_w), 1), hd - 1)).astype(F32)
    npair = MIX_W // LANES
    nsys = MIX_W // sys_w
    nsteps = c.bit_length() - 1

    def block_diag(m):
        zero = jnp.zeros((), m.dtype)
        return jnp.concatenate([jnp.where(lane_head == j, m, zero) for j in range(sys_w // hd)], axis=0)

    def pair_stack(m):
        zero = jnp.zeros((), m.dtype)
        return jnp.concatenate([jnp.where(head0, m, zero), jnp.where(head0, zero, m)], axis=0)

    def prepare(x, prev_rows):
        x = x.astype(F32)
        xprev = _shift_rows(x, prev_rows.astype(F32), 1)
        xs = x + mu_ref[...] * (xprev - x)
        r = xs[:, 0:MIX_W]
        k = xs[:, MIX_W:2 * MIX_W]
        v = xs[:, 2 * MIX_W:3 * MIX_W]
        slab = xs[:, 3 * MIX_W:3 * MIX_W + LANES]
        lg = xs[:, 3 * MIX_W + LANES:]
        slab = jnp.where(head0, jnp.tanh(slab), slab)
        low = _nn(slab.astype(BF16), wla_ref[...])
        gate = _nn(_sigmoid(lg).astype(BF16), g2_ref[...])
        yield
        ld = -RWKV_DECAY_SCALE * _sigmoid(w0_ref[...] + low[:, :MIX_W])
        cum = _mm_exact_rhs_left(tril_ref[...], ld, RWKV_CUM_TERMS)
        eta = _sigmoid(a0_ref[...] + low[:, MIX_W:])
        kkv = k * kk_ref[...]
        k2 = k * (1.0 + (eta - 1.0) * ka_ref[...])
        nrm2, rk_sum = _head_sums([kkv * kkv, r * k2 * rk_ref[...]], hsum_ref[...], RWKV_SUM_TERMS)
        yield
        kkn = kkv * lax.rsqrt(jnp.maximum(nrm2, RWKV_NORM_FLOOR ** 2))
        tot = cum[c - 1:c, :]
        e_neg = jnp.exp(-cum)
        rt = r * jnp.exp(cum)
        at = -kkn * jnp.exp(cum - ld)
        b_vec = kkn * eta
        bt = b_vec * e_neg
        kt = k2 * e_neg
        e_end = jnp.exp(tot - cum)
        b_end = b_vec * e_end
        k_end = k2 * e_end
        p_tot = jnp.exp(tot)

        at_b, rt_b = at.astype(BF16), rt.astype(BF16)
        yield
        a_ab, a_ak, a_rbk, v_bd = [], [], [], []
        for p in range(npair):
            sl = slice(p * LANES, (p + 1) * LANES)
            lhs = jnp.concatenate([at_b[:, sl], rt_b[:, sl]], axis=0)
            rhs = jnp.concatenate([pair_stack(bt[:, sl].astype(BF16)),
                                   pair_stack(kt[:, sl].astype(BF16))], axis=0)
            sc = _nt(lhs, rhs)
            a_ab.append(jnp.where(strict, sc[0:c, 0:LANES], 0.0))
            a_ak.append(jnp.where(strict, sc[0:c, LANES:], 0.0).astype(BF16))
            a_rbk.append(jnp.concatenate([jnp.where(incl, sc[c:, 0:LANES], 0.0),
                                          jnp.where(incl, sc[c:, LANES:], 0.0)], axis=1).astype(BF16))
            v_bd.append(pair_stack(v[:, sl].astype(BF16)))
            yield
        pws = [jnp.concatenate([a_ab[2 * q], a_ab[2 * q + 1]], axis=1).astype(BF16) for q in range(nsys)]
        tis = [pws[q].astype(F32) + eye_sys for q in range(nsys)]
        pws = [_nn(pws[q], block_diag(pws[q])).astype(BF16) for q in range(nsys)]
        yield
        for i in range(1, nsteps):
            for q in range(nsys):
                wts = block_diag(pws[q])
                if i + 1 < nsteps:
                    both = _nn(jnp.concatenate([pws[q], tis[q].astype(BF16)], axis=0), wts)
                    pws[q] = both[0:c].astype(BF16)
                    tis[q] = tis[q] + both[c:]
                else:
                    tis[q] = tis[q] + _nn(tis[q].astype(BF16), wts)
            yield
        tinv = [t.astype(BF16) for t in tis]
        return dict(at=at_b, rt=rt_b, a_ak=a_ak, a_rbk=a_rbk, v_bd=v_bd, v=v.astype(BF16), tinv=tinv,
                    b_end=b_end.astype(BF16), k_end=k_end.astype(BF16), p_tot=p_tot, gate=gate,
                    bonus=rk_sum * v)

    def advance(pre, s_mats):
        ws = []
        for p in range(npair):
            sl = slice(p * LANES, (p + 1) * LANES)
            s_b = s_mats[p].astype(BF16)
            ws.append(_nt(pre["at"][:, sl], s_b) + _nn(pre["a_ak"][p], pre["v_bd"][p]))
            yield
        us = []
        for q in range(nsys):
            w_sys = jnp.concatenate([ws[2 * q], ws[2 * q + 1]], axis=1).astype(BF16)
            us.append(_nn(pre["tinv"][q], block_diag(w_sys)))
            yield
        ys, new_states = [], []
        for p in range(npair):
            sl = slice(p * LANES, (p + 1) * LANES)
            u_p = us[p // 2][:, (p % 2) * LANES:(p % 2 + 1) * LANES].astype(BF16)
            s_mat = s_mats[p]
            y_p = (_nt(pre["rt"][:, sl], s_mat.astype(BF16))
                   + _nn(pre["a_rbk"][p], jnp.concatenate([pair_stack(u_p), pre["v_bd"][p]], axis=0)))
            ys.append(y_p)
            upd = _tn(jnp.concatenate([u_p, pre["v"][:, sl]], axis=0),
                      jnp.concatenate([pre["b_end"][:, sl], pre["k_end"][:, sl]], axis=0))
            new_states.append(jnp.where(bd_mask, s_mat * pre["p_tot"][:, sl] + upd, 0.0))
            yield
        y = jnp.concatenate(ys, axis=1)
        inv_hd = 1.0 / hd
        mean = _head_sums([y], hsum_ref[...], RWKV_SUM_TERMS)[0] * inv_hd
        yield
        yc = y - mean
        var = _head_sums([yc * yc], hsum_ref[...], RWKV_SUM_TERMS)[0] * inv_hd
        yn = yc * lax.rsqrt(var + RWKV_LN_EPS) * lnw_ref[...] + lnb_ref[...]
        return ((yn + pre["bonus"]) * pre["gate"]).astype(o_ref.dtype), new_states

    return prepare, advance


def _mm_exact_rhs_left(l_bf16, a, terms=3):
    acc = None
    for part in _split(a, terms):
        t = _nn(l_bf16, part)
        acc = t if acc is None else acc + t
    return acc


def _hgrn_levels(c):
    return [c >> (i + 1) for i in range(c.bit_length() - 1)]


def _block_mid_rows(a, half):
    rows, width = a.shape
    blk = 2 * half
    sub_row = _iota((SUBLANES, 1), 0)
    tiles = []
    for j in range(rows // SUBLANES):
        base = j * SUBLANES
        if blk >= SUBLANES:
            src = (base // blk) * blk + half - 1
            tiles.append(jnp.broadcast_to(a[src:src + 1, :], (SUBLANES, width)))
        else:
            tile = None
            for i in reversed(range(SUBLANES // blk)):
                src = base + i * blk + half - 1
                piece = jnp.broadcast_to(a[src:src + 1, :], (SUBLANES, width))
                tile = piece if tile is None else jnp.where(sub_row < (i + 1) * blk, piece, tile)
            tiles.append(tile)
    return jnp.concatenate(tiles, axis=0)


def _hgrn_parts(lb_ref, nw_ref, cmat_ref, o_ref, layer):
    c = HGRN_CHUNK
    levels = _hgrn_levels(c)

    lb_all = lb_ref[...]
    e_all = jnp.exp(lb_all - jnp.max(lb_all, axis=0, keepdims=True))
    probs = e_all / jnp.sum(e_all, axis=0, keepdims=True)
    lb = jnp.sum(probs[0:layer + 1, :], axis=0, keepdims=True) - probs[0:1, :]
    row_c = _iota((c, 1), 0)
    row_cc = _iota((c, c), 0)
    col_cc = _iota((c, c), 1)

    def one_chunk(x, s_mats):
        x = x.astype(F32)
        qr = x[:, 0:MIX_W]
        q = qr * _sigmoid(qr)
        sig = _sigmoid(x[:, MIX_W:2 * MIX_W])
        v = x[:, 2 * MIX_W:]
        lg = jnp.log(jnp.maximum(lb + (1.0 - lb) * sig, GATE_FLOOR))
        key = (1.0 - lb) * (1.0 - sig)
        cum = _mm_exact_rhs_left(cmat_ref[...], lg)
        tot = cum[c - 1:c, :]
        q_chunk = q * jnp.exp(cum)
        k_end = key * jnp.exp(tot - cum)
        yield
        q_lv, k_lv = [], []
        zero_tile = jnp.zeros((SUBLANES, MIX_W), F32)
        for li, half in enumerate(levels):
            if half >= SUBLANES:
                q_tiles, k_tiles = [], []
                for base in range(0, c, SUBLANES):
                    rows = slice(base, base + SUBLANES)
                    mid_row = (base // (2 * half)) * (2 * half) + half - 1
                    mid = cum[mid_row:mid_row + 1, :]
                    if base & half:
                        q_tiles.append(q[rows] * jnp.exp(cum[rows] - mid))
                        k_tiles.append(zero_tile)
                    else:
                        q_tiles.append(zero_tile)
                        k_tiles.append(key[rows] * jnp.exp(mid - cum[rows]))
                q_lv.append(jnp.concatenate(q_tiles, axis=0))
                k_lv.append(jnp.concatenate(k_tiles, axis=0))
            else:
                dec = jnp.exp(-jnp.abs(cum - _block_mid_rows(cum, half)))
                second = jnp.bitwise_and(row_c, half) != 0
                q_lv.append(jnp.where(second, q * dec, 0.0))
                k_lv.append(jnp.where(second, 0.0, key * dec))
        qk = q * key
        outs, new_states, all_scores = [], [], []
        for h in range(HGRN_HEADS):
            sl = slice(h * LANES, (h + 1) * LANES)
            scores = None
            for li, half in enumerate(levels):
                sc = _mm(_nt, q_lv[li][:, sl], k_lv[li][:, sl])
                if 2 * half < c:
                    blk = 2 * half
                    sc = jnp.where((row_cc // blk) == (col_cc // blk), sc, 0.0)
                scores = sc if scores is None else scores + sc
            all_scores.append(scores)
            yield
        for h in range(HGRN_HEADS):
            sl = slice(h * LANES, (h + 1) * LANES)
            s_mat = s_mats[h]
            v_h = v[:, sl]
            o_h = (_mm(_nt, q_chunk[:, sl], s_mat) + _mm(_nn, all_scores[h], v_h)
                   + jnp.sum(qk[:, sl], axis=-1, keepdims=True) * v_h)
            ms = jnp.mean(o_h * o_h, axis=-1, keepdims=True)
            outs.append(o_h * lax.rsqrt(ms + NORM_EPS))
            upd = _mm(_tn, v_h, k_end[:, sl])
            new_states.append(s_mat * jnp.exp(tot[:, sl]) + upd)
            yield
        o = jnp.concatenate(outs, axis=1) * nw_ref[...]
        return o.astype(o_ref.dtype), new_states

    return one_chunk


def _recurrent_kernel(xb_ref, mu_ref, w0_ref, wla_ref, a0_ref, g2_ref, kk_ref, ka_ref, rk_ref, lnw_ref,
                      lnb_ref, hsum_ref, tril_ref, xd_ref, lb_ref, nw_ref, yb_ref, yd_ref,
                      carry_ref, rstate_ref, hstate_ref, *, layer):
    t_tile = xb_ref.shape[1]
    c = RWKV_CHUNK
    assert HGRN_CHUNK == c
    nchunk = t_tile // c
    seqs = range(xb_ref.shape[0])
    npair = MIX_W // LANES
    halo = carry_ref.shape[1]

    @pl.when(pl.program_id(1) == 0)
    def _():
        carry_ref[...] = jnp.zeros_like(carry_ref)
        rstate_ref[...] = jnp.zeros_like(rstate_ref)
        hstate_ref[...] = jnp.zeros_like(hstate_ref)

    prepare, advance = _rwkv_parts(mu_ref, w0_ref, wla_ref, a0_ref, g2_ref, kk_ref, ka_ref, rk_ref,
                                   lnw_ref, lnb_ref, hsum_ref, tril_ref, yb_ref)
    hgrn_chunk = _hgrn_parts(lb_ref, nw_ref, tril_ref, yd_ref, layer)

    def step(r0, pre, nxt_gens):
        rstates = [[rstate_ref[g, p] for p in range(npair)] for g in seqs]
        hstates = [[hstate_ref[g, h] for h in range(HGRN_HEADS)] for g in seqs]
        xd_in = [xd_ref[g, pl.ds(r0, c), :] for g in seqs]
        res = _round_robin([advance(pre[g], rstates[g]) for g in seqs]
                           + [hgrn_chunk(xd_in[g], hstates[g]) for g in seqs] + nxt_gens)
        ns = len(seqs)
        for g in seqs:
            out, new_states = res[g]
            for p in range(npair):
                rstate_ref[g, p] = new_states[p]
            yb_ref[g, pl.ds(r0, c), :] = out
            out, new_states = res[ns + g]
            for h in range(HGRN_HEADS):
                hstate_ref[g, h] = new_states[h]
            yd_ref[g, pl.ds(r0, c), :] = out
        return res[2 * ns:]

    pre0 = _round_robin([prepare(xb_ref[g, 0:c, :], carry_ref[g]) for g in seqs])

    def chunk(ci, pre):
        r0 = pl.multiple_of(ci * c, c)
        r1 = pl.multiple_of(r0 + c, c)
        nxt = [xb_ref[g, pl.ds(r1, c), :] for g in seqs]
        prv = [xb_ref[g, pl.ds(pl.multiple_of(r1 - halo, halo), halo), :] for g in seqs]
        return step(r0, pre, [prepare(nxt[g], prv[g]) for g in seqs])

    pre_last = lax.fori_loop(0, nchunk - 1, chunk, pre0)
    step((nchunk - 1) * c, pre_last, [])
    for g in seqs:
        carry_ref[g] = xb_ref[g, t_tile - halo:, :].astype(F32)


def _recurrent_mixers(xb, xd, mu, w0, w2, a0, a2, g2, k_k, k_a, r_k, ln_w, ln_b, lb_all, layer,
                      hgrn_norm, t_tile):
    b, s, _ = xb.shape
    c = RWKV_CHUNK
    depth = lb_all.shape[0]
    rank = w2.shape[0]
    wla = jnp.zeros((LANES, 2 * MIX_W), F32)
    wla = wla.at[:rank, :MIX_W].set(w2).at[rank:, MIX_W:].set(a2).astype(BF16)
    gs = MIXER_SEQS if b % MIXER_SEQS == 0 else 1
    hsum = jnp.kron(jnp.eye(LANES // RWKV_HEAD, dtype=F32),
                    jnp.ones((RWKV_HEAD, RWKV_HEAD), F32)).astype(BF16)
    tril = jnp.tril(jnp.ones((c, c), F32)).astype(BF16)
    row = lambda a: a.reshape(1, -1).astype(F32)
    full = lambda shape: pl.BlockSpec(shape, lambda i, j: (0,) * len(shape))
    tile = lambda w: pl.BlockSpec((gs, t_tile, w), lambda i, j: (i, j, 0))
    return pl.pallas_call(
        functools.partial(_recurrent_kernel, layer=layer),
        grid=(b // gs, s // t_tile),
        in_specs=[
            tile(RWKV_IN),
            full((1, RWKV_IN)), full((1, MIX_W)), full((LANES, 2 * MIX_W)), full((1, MIX_W)),
            full((LANES, MIX_W)), full((1, MIX_W)), full((1, MIX_W)), full((1, MIX_W)),
            full((1, MIX_W)), full((1, MIX_W)), full((LANES, LANES)), full((c, c)),
            tile(COL_D), full((depth, MIX_W)), full((1, MIX_W)),
        ],
        out_specs=[tile(MIX_W), tile(MIX_W)],
        out_shape=[jax.ShapeDtypeStruct((b, s, MIX_W), BF16)] * 2,
        scratch_shapes=[pltpu.VMEM((gs, MIXER_HALO, RWKV_IN), F32),
                        pltpu.VMEM((gs, MIX_W // LANES, LANES, LANES), F32),
                        pltpu.VMEM((gs, HGRN_HEADS, HGRN_EXPAND, HGRN_EXPAND), F32)],
        compiler_params=_cparams(("arbitrary", "arbitrary")),
        name="recurrent_mixers",
    )(xb, row(mu), row(w0), wla, row(a0), g2.astype(BF16), row(k_k), row(k_a), row(r_k),
      row(ln_w), row(ln_b), hsum, tril, xd, lb_all.astype(F32), hgrn_norm.reshape(1, MIX_W))


def _merge_kernel(ya_ref, yb_ref, yc_ref, yd_ref, g_ref, x_ref, wbr_ref, wout_ref, o_ref):
    merged = None
    for kk, y_ref in enumerate((ya_ref, yb_ref, yc_ref, yd_ref)):
        gate = _sigmoid(g_ref[:, kk * D_MODEL:(kk + 1) * D_MODEL].astype(F32))
        term = gate * _nn(y_ref[...], wbr_ref[kk])
        merged = term if merged is None else merged + term
    o_ref[...] = x_ref[...] + _nn(merged.astype(BF16), wout_ref[...])


def _merge(ya, yb, yc, yd, g, x, wbr_bf16, wout_bf16, tm):
    n = x.shape[0]
    ytile = pl.BlockSpec((tm, MIX_W), lambda i: (i, 0))
    return pl.pallas_call(
        _merge_kernel,
        grid=(n // tm,),
        in_specs=[
            ytile, ytile, ytile, ytile,
            pl.BlockSpec((tm, 4 * D_MODEL), lambda i: (i, 0)),
            pl.BlockSpec((tm, D_MODEL), lambda i: (i, 0)),
            pl.BlockSpec((4, MIX_W, D_MODEL), lambda i: (0, 0, 0)),
            pl.BlockSpec((D_MODEL, D_MODEL), lambda i: (0, 0)),
        ],
        out_specs=pl.BlockSpec((tm, D_MODEL), lambda i: (i, 0)),
        out_shape=jax.ShapeDtypeStruct((n, D_MODEL), F32),
        compiler_params=_cparams(("arbitrary",)),
        name="merge_out",
    )(ya, yb, yc, yd, g, x, wbr_bf16, wout_bf16)


def _ffn_kernel(x_ref, nf_ref, wup_ref, cw_ref, cb_ref, wd_ref, p_ref, np_ref, pproj_ref, pgate_ref,
                nfin_ref, o_ref, cg_ref, cv_ref, ug_ref, uv_ref, act_ref, acc_ref, *, final_norm):
    tm = x_ref.shape[1]
    blk = FFN_BLOCK
    nb = FFN_NB

    @pl.when(pl.program_id(1) == 0)
    def _():
        cg_ref[...] = jnp.zeros_like(cg_ref)
        cv_ref[...] = jnp.zeros_like(cv_ref)

    x = x_ref[0]
    pp = _nn(p_ref[0].astype(BF16), pproj_ref[...])
    h = _rms(x, nf_ref[...]).astype(BF16)

    def conv(u, carry_ref, j, col0):
        prev = carry_ref[j]
        u1 = _shift_rows(u, prev, 1)
        u2 = _shift_rows(u, prev, 2)
        carry_ref[j] = u[tm - SUBLANES:, :]
        cw = cw_ref[:, col0:col0 + blk]
        return cb_ref[:, col0:col0 + blk] + cw[2:3, :] * u + cw[1:2, :] * u1 + cw[0:1, :] * u2

    def up(j):
        ug_ref[j % 2] = _nn(h, wup_ref[:, j * blk:(j + 1) * blk])
        uv_ref[j % 2] = _nn(h, wup_ref[:, D_FF + j * blk:D_FF + (j + 1) * blk])

    group_ends = [e for e in FFN_GROUP_ENDS if e < nb] + [nb]
    group_start = {end: start for start, end in zip([0] + group_ends[:-1], group_ends)}

    def down(end):
        k0, k1 = group_start[end] * blk, end * blk
        d = _nn(act_ref[:, k0:k1], wd_ref[k0:k1, :])
        if k0 == 0:
            acc_ref[...] = d
        else:
            acc_ref[...] += d

    up(0)
    for j in range(nb):
        if j + 1 < nb:
            up(j + 1)
        if j in group_start:
            down(j)
        ug = conv(ug_ref[j % 2], cg_ref, j, j * blk)
        uv = conv(uv_ref[j % 2], cv_ref, j, D_FF + j * blk)
        act_ref[:, j * blk:(j + 1) * blk] = (_gelu(ug) * uv).astype(BF16)
    down(nb)

    x2 = x + acc_ref[...]
    hp = _rms(x2, np_ref[...]).astype(BF16)
    gate = _sigmoid(_nn(hp, pgate_ref[...]))
    x3 = x2 + pp * gate
    if final_norm:
        x3 = _rms(x3, nfin_ref[...])
    o_ref[0] = x3


def _ffn_ple(x1, norm_ffn, wup_bf16, conv_w, conv_b, wdown_bf16, p_l, norm_ple, pproj_bf16,
             pgate_bf16, norm_final, final_norm, tm):
    b, s, d = x1.shape
    nb, blk = FFN_NB, FFN_BLOCK
    row = lambda a: a.reshape(1, -1)
    const = lambda shape: pl.BlockSpec(shape, lambda i, t: (0,) * len(shape),
                                       pipeline_mode=pl.Buffered(1))
    return pl.pallas_call(
        functools.partial(_ffn_kernel, final_norm=final_norm),
        grid=(b, s // tm),
        in_specs=[
            pl.BlockSpec((1, tm, d), lambda i, t: (i, t, 0)),
            const((1, d)),
            const((d, 2 * D_FF)), const((3, 2 * D_FF)), const((1, 2 * D_FF)), const((D_FF, d)),
            pl.BlockSpec((1, tm, PLE_DIM), lambda i, t: (i, t, 0)),
            const((1, d)),
            const((PLE_DIM, d)),
            const((d, d)),
            const((1, d)),
        ],
        out_specs=pl.BlockSpec((1, tm, d), lambda i, t: (i, t, 0)),
        out_shape=jax.ShapeDtypeStruct((b, s, d), F32),
        scratch_shapes=[pltpu.VMEM((nb, SUBLANES, blk), F32),
                        pltpu.VMEM((nb, SUBLANES, blk), F32),
                        pltpu.VMEM((2, tm, blk), F32),
                        pltpu.VMEM((2, tm, blk), F32),
                        pltpu.VMEM((tm, D_FF), BF16),
                        pltpu.VMEM((tm, d), F32)],
        compiler_params=_cparams(("arbitrary", "arbitrary")),
        name="ffn_ple",
    )(x1, row(norm_ffn), wup_bf16, conv_w, row(conv_b), wdown_bf16, p_l, row(norm_ple), pproj_bf16,
      pgate_bf16, row(norm_final))


def _pick_tile(total, want):
    t = min(total, want)
    while total % t:
        t //= 2
    return t


IN_PROJ_TILE = 512
MIXER_TILE = 256
MERGE_TILE = 512
FFN_TILE = 512


def kernel(x, p, norm_mix, w_in, pool_w, pool_scale, rwkv_mu, rwkv_w0, rwkv_w2, rwkv_a0, rwkv_a2, rwkv_g2, rwkv_kk, rwkv_ka, rwkv_rk, rwkv_ln_w, rwkv_ln_b, sg_ln_w, sg_ln_b, sg_w, sg_b, hgrn_lb, hgrn_norm, w_branch, w_out, norm_ffn, ffn_up, ffn_conv, ffn_conv_b, ffn_down, norm_ple, ple_proj, ple_gate, norm_final):
    b, s, d = x.shape
    depth = w_in.shape[0]
    n = b * s
    for l in range(depth):
        ya, xb, yc, xd, g = _in_proj_mix(x, norm_mix[l], w_in[l].astype(BF16), pool_w[l].astype(BF16),
                                         pool_scale[l], sg_ln_w[l], sg_ln_b[l], sg_w[l], sg_b[l],
                                         _pick_tile(s, IN_PROJ_TILE))
        yb, yd = _recurrent_mixers(xb, xd, rwkv_mu[l], rwkv_w0[l], rwkv_w2[l], rwkv_a0[l], rwkv_a2[l],
                                   rwkv_g2[l], rwkv_kk[l], rwkv_ka[l], rwkv_rk[l].reshape(-1),
                                   rwkv_ln_w[l], rwkv_ln_b[l], hgrn_lb, l, hgrn_norm[l],
                                   _pick_tile(s, MIXER_TILE))
        x1 = _merge(ya.reshape(n, MIX_W), yb.reshape(n, MIX_W), yc.reshape(n, MIX_W),
                    yd.reshape(n, MIX_W), g.reshape(n, 4 * D_MODEL), x.reshape(n, d),
                    w_branch[l].astype(BF16), w_out[l].astype(BF16), _pick_tile(n, MERGE_TILE))
        x = _ffn_ple(x1.reshape(b, s, d), norm_ffn[l], ffn_up[l].astype(BF16), ffn_conv[l],
                     ffn_conv_b[l], ffn_down[l].astype(BF16), p[l], norm_ple[l],
                     ple_proj[l].astype(BF16), ple_gate[l].astype(BF16), norm_final,
                     l == depth - 1, _pick_tile(s, FFN_TILE))
    return x
```

```python
import functools

import jax
import jax.numpy as jnp
from jax import lax
from jax.experimental import pallas as pl
from jax.experimental.pallas import tpu as pltpu

F32 = jnp.float32
BF16 = jnp.bfloat16

D_MODEL = 1024
MIX_W = 512
PLE_DIM = 256
POOL_WINDOWS = (2, 4, 8, 16)
POOL_GROUP = 128
POOL_HALO = 16
RWKV_HEAD = 64
RWKV_IN = 1792
RWKV_LN_EPS = 64e-5
SG_CHUNK = 128
SG_GROUPS = 4
SG_LN_EPS = 1e-5
HGRN_HEADS = 4
HGRN_EXPAND = 128
GATE_FLOOR = 1e-30
D_FF = 2816
NORM_EPS = 1e-6
COL_A, COL_B, COL_C, COL_D = 512, 1792, 1024, 1536

LANES = 128
SUBLANES = 8
VMEM_LIMIT_BYTES = 56 * 1024 * 1024

RWKV_CHUNK = 64
HGRN_CHUNK = 64

FFN_BLOCK = 256
FFN_NB = D_FF // FFN_BLOCK
FFN_GROUP_ENDS = (4, 8, 10)


def _cparams(sem):
    return pltpu.CompilerParams(dimension_semantics=sem, vmem_limit_bytes=VMEM_LIMIT_BYTES)


def _nn(a, b):
    return lax.dot_general(a, b, (((1,), (0,)), ((), ())), preferred_element_type=F32)


def _nt(a, b):
    return lax.dot_general(a, b, (((1,), (1,)), ((), ())), preferred_element_type=F32)


def _tn(a, b):
    return lax.dot_general(a, b, (((0,), (0,)), ((), ())), preferred_element_type=F32)


def _split(a, terms):
    out = []
    rem = a
    for _ in range(terms - 1):
        hi = rem.astype(BF16)
        out.append(hi)
        rem = rem - hi.astype(F32)
    out.append(rem.astype(BF16))
    return out


def _mm(dot, a, b):
    return dot(a.astype(BF16), b.astype(BF16))


def _mm_exact_rhs(a, b_bf16, terms):
    acc = None
    for part in _split(a, terms):
        t = _nn(part, b_bf16)
        acc = t if acc is None else acc + t
    return acc


def _head_sums(arrays, ones_pair_bf16, terms=2):
    c = arrays[0].shape[0]
    slabs = [a[:, p * LANES:(p + 1) * LANES] for a in arrays for p in range(MIX_W // LANES)]
    s = _mm_exact_rhs(jnp.concatenate(slabs, axis=0), ones_pair_bf16, terms)
    per = MIX_W // LANES
    return [jnp.concatenate([s[(i * per + p) * c:(i * per + p + 1) * c, :] for p in range(per)], axis=1)
            for i in range(len(arrays))]


def _rms(x, gain):
    return x * lax.rsqrt(jnp.mean(x * x, axis=-1, keepdims=True) + NORM_EPS) * gain


def _sigmoid(x):
    return 1.0 / (1.0 + jnp.exp(-x))


def _gelu(x):
    return 0.5 * x * (1.0 + jnp.tanh(0.7978845608028654 * (x + 0.044715 * (x * x * x))))


def _iota(shape, dim):
    return lax.broadcasted_iota(jnp.int32, shape, dim)


def _round_robin(gens):
    results = [None] * len(gens)
    live = list(range(len(gens)))
    while live:
        for idx in list(live):
            try:
                next(gens[idx])
            except StopIteration as stop:
                results[idx] = stop.value
                live.remove(idx)
    return results


def _shift_rows(x, prev_rows, n):
    rolled = pltpu.roll(x, n, axis=0)
    row = _iota(x.shape, 0)
    out = rolled
    for j in range(n):
        src = prev_rows.shape[0] - n + j
        out = jnp.where(row == j, prev_rows[src:src + 1, :], out)
    return out


IN_PROJ_CHUNK = 512


def _pool_mix(a, carry_ref, first_pos, w_ref, scale_ref):
    t_tile = a.shape[0]
    ext = jnp.concatenate([carry_ref[...], a], axis=0)
    carry_ref[...] = a[t_tile - POOL_HALO:, :]
    count = (first_pos + _iota((t_tile, 1), 0) + 1).astype(F32)
    outs = []
    for gi, win in enumerate(POOL_WINDOWS):
        col = slice(gi * POOL_GROUP, (gi + 1) * POOL_GROUP)
        s = ext[:, col]
        span = 1
        while span < win:
            s = s + pltpu.roll(s, span, axis=0)
            span *= 2
        mean = s[POOL_HALO:, :] / jnp.minimum(count, float(win))
        diff = (mean - a[:, col]).astype(BF16)
        outs.append(_nn(diff, w_ref[gi]))
    return jnp.concatenate(outs, axis=1) * scale_ref[...]


def _sg_mix(xc, lnw_ref, lnb_ref, ws_ref, bs_ref, o_ref):
    t_tile = xc.shape[0]
    z = _gelu(xc)
    u = z[:, :MIX_W]
    v = z[:, MIX_W:]
    mean = jnp.mean(v, axis=-1, keepdims=True)
    vc = v - mean
    var = jnp.mean(vc * vc, axis=-1, keepdims=True)
    vn = (vc * lax.rsqrt(var + SG_LN_EPS) * lnw_ref[...] + lnb_ref[...]).astype(BF16)
    causal = _iota((SG_CHUNK, SG_CHUNK), 0) >= _iota((SG_CHUNK, SG_CHUNK), 1)
    w_m = [jnp.where(causal, ws_ref[g], 0.0).astype(BF16) for g in range(SG_GROUPS)]
    for n in range(t_tile // SG_CHUNK):
        rows = slice(n * SG_CHUNK, (n + 1) * SG_CHUNK)
        mixed = jnp.concatenate(
            [_nn(w_m[g], vn[rows, g * LANES:(g + 1) * LANES]) for g in range(SG_GROUPS)], axis=1)
        o_ref[0, rows, :] = (u[rows, :] * (mixed + bs_ref[...])).astype(o_ref.dtype)


def _in_proj_kernel(x_ref, g_ref, w_ref, pw_ref, pscale_ref, lnw_ref, lnb_ref, ws_ref, bs_ref,
                    ya_ref, xb_ref, yc_ref, xd_ref, gate_ref, carry_ref):
    tm = x_ref.shape[1]

    @pl.when(pl.program_id(1) == 0)
    def _():
        carry_ref[...] = jnp.zeros_like(carry_ref)

    h = _rms(x_ref[0], g_ref[...]).astype(BF16)

    def proj(c0, c1):
        parts = [_nn(h, w_ref[:, k0:min(k0 + IN_PROJ_CHUNK, c1)]) for k0 in range(c0, c1, IN_PROJ_CHUNK)]
        return parts[0] if len(parts) == 1 else jnp.concatenate(parts, axis=1)

    o0, o1, o2, o3 = COL_A, COL_A + COL_B, COL_A + COL_B + COL_C, COL_A + COL_B + COL_C + COL_D
    def emit(out_ref, c0, c1):
        for k0 in range(c0, c1, IN_PROJ_CHUNK):
            k1 = min(k0 + IN_PROJ_CHUNK, c1)
            out_ref[0, :, k0 - c0:k1 - c0] = _nn(h, w_ref[:, k0:k1]).astype(out_ref.dtype)

    xa = proj(0, o0)
    xc = proj(o1, o2)
    emit(xb_ref, o0, o1)
    emit(xd_ref, o2, o3)
    emit(gate_ref, o3, w_ref.shape[1])
    ya = _pool_mix(xa, carry_ref, pl.program_id(1) * tm, pw_ref, pscale_ref)
    ya_ref[0] = ya.astype(ya_ref.dtype)
    _sg_mix(xc, lnw_ref, lnb_ref, ws_ref, bs_ref, yc_ref)


def _in_proj_mix(x, gain, w_bf16, pool_w_bf16, pool_scale, sg_ln_w, sg_ln_b, sg_w, sg_b, tm):
    b, s, d = x.shape
    m = w_bf16.shape[1]
    b_full = jnp.repeat(jnp.swapaxes(sg_b, 0, 1), MIX_W // SG_GROUPS, axis=1)
    const = lambda shape: pl.BlockSpec(shape, lambda i, t: (0,) * len(shape),
                                       pipeline_mode=pl.Buffered(1))
    tile = lambda w: pl.BlockSpec((1, tm, w), lambda i, t: (i, t, 0))
    widths = (MIX_W, COL_B, MIX_W, COL_D, m - (COL_A + COL_B + COL_C + COL_D))
    dtypes = (BF16,) * len(widths)
    return pl.pallas_call(
        _in_proj_kernel,
        grid=(b, s // tm),
        in_specs=[
            tile(d), const((1, d)), const((d, m)),
            const((len(POOL_WINDOWS), POOL_GROUP, POOL_GROUP)), const((1, MIX_W)),
            const((1, MIX_W)), const((1, MIX_W)), const((SG_GROUPS, SG_CHUNK, SG_CHUNK)),
            const((SG_CHUNK, MIX_W)),
        ],
        out_specs=[tile(w) for w in widths],
        out_shape=[jax.ShapeDtypeStruct((b, s, w), dt) for w, dt in zip(widths, dtypes)],
        scratch_shapes=[pltpu.VMEM((POOL_HALO, MIX_W), F32)],
        compiler_params=_cparams(("arbitrary", "arbitrary")),
        name="in_proj_mix",
    )(x, gain.reshape(1, d), w_bf16, pool_w_bf16, pool_scale.reshape(1, MIX_W),
      sg_ln_w.reshape(1, MIX_W), sg_ln_b.reshape(1, MIX_W), sg_w, b_full)


RWKV_DECAY_SCALE = 0.6065306597126334
RWKV_NORM_FLOOR = 1e-12
RWKV_CUM_TERMS = 2
RWKV_SUM_TERMS = 1
MIXER_SEQS = 4
MIXER_HALO = 16


def _rwkv_parts(mu_ref, w0_ref, wla_ref, a0_ref, g2_ref, kk_ref, ka_ref, rk_ref,
                lnw_ref, lnb_ref, hsum_ref, tril_ref, o_ref):
    c = RWKV_CHUNK
    hd = RWKV_HEAD

    lane = _iota((1, LANES), 1)
    head0 = lane < hd
    col_s = jnp.bitwise_and(_iota((c, LANES), 1), hd - 1)
    row_t = _iota((c, LANES), 0)
    strict = row_t > col_s
    incl = row_t >= col_s
    bd_mask = (_iota((LANES, LANES), 0) < hd) == (_iota((LANES, LANES), 1) < hd)
    sys_w = 2 * LANES
    lane_head = _iota((c, sys_w), 1) // hd

    eye_sys = (_iota((c, sys_w), 0) == jnp.bitwise_and(_iota((c, sys_w), 1), hd - 1)).astype(F32)
    npair = MIX_W // LANES
    nsys = MIX_W // sys_w
    nsteps = c.bit_length() - 1

    def block_diag(m):
        zero = jnp.zeros((), m.dtype)
        return jnp.concatenate([jnp.where(lane_head == j, m, zero) for j in range(sys_w // hd)], axis=0)

    def pair_stack(m):
        zero = jnp.zeros((), m.dtype)
        return jnp.concatenate([jnp.where(head0, m, zero), jnp.where(head0, zero, m)], axis=0)

    def prepare(x, prev_rows):
        x = x.astype(F32)
        xprev = _shift_rows(x, prev_rows.astype(F32), 1)
        xs = x + mu_ref[...] * (xprev - x)
        r = xs[:, 0:MIX_W]
        k = xs[:, MIX_W:2 * MIX_W]
        v = xs[:, 2 * MIX_W:3 * MIX_W]
        slab = xs[:, 3 * MIX_W:3 * MIX_W + LANES]
        lg = xs[:, 3 * MIX_W + LANES:]
        slab = jnp.where(head0, jnp.tanh(slab), slab)
        low = _nn(slab.astype(BF16), wla_ref[...])
        gate = _nn(_sigmoid(lg).astype(BF16), g2_ref[...])
        yield
        ld = -RWKV_DECAY_SCALE * _sigmoid(w0_ref[...] + low[:, :MIX_W])
        cum = _mm_exact_rhs_left(tril_ref[...], ld, RWKV_CUM_TERMS)
        eta = _sigmoid(a0_ref[...] + low[:, MIX_W:])
        kkv = k * kk_ref[...]
        k2 = k * (1.0 + (eta - 1.0) * ka_ref[...])
        nrm2, rk_sum = _head_sums([kkv * kkv, r * k2 * rk_ref[...]], hsum_ref[...], RWKV_SUM_TERMS)
        yield
        kkn = kkv * lax.rsqrt(jnp.maximum(nrm2, RWKV_NORM_FLOOR ** 2))
        tot = cum[c - 1:c, :]
        e_neg = jnp.exp(-cum)
        rt = r * jnp.exp(cum)
        at = -kkn * jnp.exp(cum - ld)
        b_vec = kkn * eta
        bt = b_vec * e_neg
        kt = k2 * e_neg
        e_end = jnp.exp(tot - cum)
        b_end = b_vec * e_end
        k_end = k2 * e_end
        p_tot = jnp.exp(tot)

        at_b, rt_b = at.astype(BF16), rt.astype(BF16)
        yield
        a_ab, a_ak, a_rbk, v_bd = [], [], [], []
        for p in range(npair):
            sl = slice(p * LANES, (p + 1) * LANES)
            lhs = jnp.concatenate([at_b[:, sl], rt_b[:, sl]], axis=0)
            rhs = jnp.concatenate([pair_stack(bt[:, sl].astype(BF16)),
                                   pair_stack(kt[:, sl].astype(BF16))], axis=0)
            sc = _nt(lhs, rhs)
            a_ab.append(jnp.where(strict, sc[0:c, 0:LANES], 0.0))
            a_ak.append(jnp.where(strict, sc[0:c, LANES:], 0.0).astype(BF16))
            a_rbk.append(jnp.concatenate([jnp.where(incl, sc[c:, 0:LANES], 0.0),
                                          jnp.where(incl, sc[c:, LANES:], 0.0)], axis=1).astype(BF16))
            v_bd.append(pair_stack(v[:, sl].astype(BF16)))
            yield
        pws = [jnp.concatenate([a_ab[2 * q], a_ab[2 * q + 1]], axis=1).astype(BF16) for q in range(nsys)]
        tis = [pws[q].astype(F32) + eye_sys for q in range(nsys)]
        pws = [_nn(pws[q], block_diag(pws[q])).astype(BF16) for q in range(nsys)]
        yield
        for i in range(1, nsteps):
            for q in range(nsys):
                wts = block_diag(pws[q])
                if i + 1 < nsteps:
                    both = _nn(jnp.concatenate([pws[q], tis[q].astype(BF16)], axis=0), wts)
                    pws[q] = both[0:c].astype(BF16)
                    tis[q] = tis[q] + both[c:]
                else:
                    tis[q] = tis[q] + _nn(tis[q].astype(BF16), wts)
            yield
        tinv = [t.astype(BF16) for t in tis]
        return dict(at=at_b, rt=rt_b, a_ak=a_ak, a_rbk=a_rbk, v_bd=v_bd, v=v.astype(BF16), tinv=tinv,
                    b_end=b_end.astype(BF16), k_end=k_end.astype(BF16), p_tot=p_tot, gate=gate,
                    bonus=rk_sum * v)

    def advance(pre, s_mats):
        ws = []
        for p in range(npair):
            sl = slice(p * LANES, (p + 1) * LANES)
            s_b = s_mats[p].astype(BF16)
            ws.append(_nt(pre["at"][:, sl], s_b) + _nn(pre["a_ak"][p], pre["v_bd"][p]))
            yield
        us = []
        for q in range(nsys):
            w_sys = jnp.concatenate([ws[2 * q], ws[2 * q + 1]], axis=1).astype(BF16)
            us.append(_nn(pre["tinv"][q], block_diag(w_sys)))
            yield
        ys, new_states = [], []
        for p in range(npair):
            sl = slice(p * LANES, (p + 1) * LANES)
            u_p = us[p // 2][:, (p % 2) * LANES:(p % 2 + 1) * LANES].astype(BF16)
            s_mat = s_mats[p]
            y_p = (_nt(pre["rt"][:, sl], s_mat.astype(BF16))
                   + _nn(pre["a_rbk"][p], jnp.concatenate([pair_stack(u_p), pre["v_bd"][p]], axis=0)))
            ys.append(y_p)
            upd = _tn(jnp.concatenate([u_p, pre["v"][:, sl]], axis=0),
                      jnp.concatenate([pre["b_end"][:, sl], pre["k_end"][:, sl]], axis=0))
            new_states.append(jnp.where(bd_mask, s_mat * pre["p_tot"][:, sl] + upd, 0.0))
            yield
        y = jnp.concatenate(ys, axis=1)
        inv_hd = 1.0 / hd
        mean = _head_sums([y], hsum_ref[...], RWKV_SUM_TERMS)[0] * inv_hd
        yield
        yc = y - mean
        var = _head_sums([yc * yc], hsum_ref[...], RWKV_SUM_TERMS)[0] * inv_hd
        yn = yc * lax.rsqrt(var + RWKV_LN_EPS) * lnw_ref[...] + lnb_ref[...]
        return ((yn + pre["bonus"]) * pre["gate"]).astype(o_ref.dtype), new_states

    return prepare, advance


def _mm_exact_rhs_left(l_bf16, a, terms=3):
    acc = None
    for part in _split(a, terms):
        t = _nn(l_bf16, part)
        acc = t if acc is None else acc + t
    return acc


def _hgrn_levels(c):
    return [c >> (i + 1) for i in range(c.bit_length() - 1)]


def _block_mid_rows(a, half):
    rows, width = a.shape
    blk = 2 * half
    sub_row = _iota((SUBLANES, 1), 0)
    tiles = []
    for j in range(rows // SUBLANES):
        base = j * SUBLANES
        if blk >= SUBLANES:
            src = (base // blk) * blk + half - 1
            tiles.append(jnp.broadcast_to(a[src:src + 1, :], (SUBLANES, width)))
        else:
            tile = None
            for i in reversed(range(SUBLANES // blk)):
                src = base + i * blk + half - 1
                piece = jnp.broadcast_to(a[src:src + 1, :], (SUBLANES, width))
                tile = piece if tile is None else jnp.where(sub_row < (i + 1) * blk, piece, tile)
            tiles.append(tile)
    return jnp.concatenate(tiles, axis=0)


def _hgrn_parts(lb_ref, nw_ref, cmat_ref, o_ref, layer):
    c = HGRN_CHUNK
    levels = _hgrn_levels(c)

    lb_all = lb_ref[...]
    e_all = jnp.exp(lb_all - jnp.max(lb_all, axis=0, keepdims=True))
    probs = e_all / jnp.sum(e_all, axis=0, keepdims=True)
    lb = jnp.sum(probs[0:layer + 1, :], axis=0, keepdims=True) - probs[0:1, :]
    row_c = _iota((c, 1), 0)
    row_cc = _iota((c, c), 0)
    col_cc = _iota((c, c), 1)

    def one_chunk(x, s_mats):
        x = x.astype(F32)
        qr = x[:, 0:MIX_W]
        q = qr * _sigmoid(qr)
        sig = _sigmoid(x[:, MIX_W:2 * MIX_W])
        v = x[:, 2 * MIX_W:]
        lg = jnp.log(jnp.maximum(lb + (1.0 - lb) * sig, GATE_FLOOR))
        key = (1.0 - lb) * (1.0 - sig)
        cum = _mm_exact_rhs_left(cmat_ref[...], lg)
        tot = cum[c - 1:c, :]
        q_chunk = q * jnp.exp(cum)
        k_end = key * jnp.exp(tot - cum)
        yield
        q_lv, k_lv = [], []
        zero_tile = jnp.zeros((SUBLANES, MIX_W), F32)
        for li, half in enumerate(levels):
            if half >= SUBLANES:
                q_tiles, k_tiles = [], []
                for base in range(0, c, SUBLANES):
                    rows = slice(base, base + SUBLANES)
                    mid_row = (base // (2 * half)) * (2 * half) + half - 1
                    mid = cum[mid_row:mid_row + 1, :]
                    if base & half:
                        q_tiles.append(q[rows] * jnp.exp(cum[rows] - mid))
                        k_tiles.append(zero_tile)
                    else:
                        q_tiles.append(zero_tile)
                        k_tiles.append(key[rows] * jnp.exp(mid - cum[rows]))
                q_lv.append(jnp.concatenate(q_tiles, axis=0))
                k_lv.append(jnp.concatenate(k_tiles, axis=0))
            else:
                dec = jnp.exp(-jnp.abs(cum - _block_mid_rows(cum, half)))
                second = jnp.bitwise_and(row_c, half) != 0
                q_lv.append(jnp.where(second, q * dec, 0.0))
                k_lv.append(jnp.where(second, 0.0, key * dec))
        qk = q * key
        outs, new_states, all_scores = [], [], []
        for h in range(HGRN_HEADS):
            sl = slice(h * LANES, (h + 1) * LANES)
            scores = None
            for li, half in enumerate(levels):
                sc = _mm(_nt, q_lv[li][:, sl], k_lv[li][:, sl])
                if 2 * half < c:
                    blk = 2 * half
                    sc = jnp.where((row_cc // blk) == (col_cc // blk), sc, 0.0)
                scores = sc if scores is None else scores + sc
            all_scores.append(scores)
            yield
        for h in range(HGRN_HEADS):
            sl = slice(h * LANES, (h + 1) * LANES)
            s_mat = s_mats[h]
            v_h = v[:, sl]
            o_h = (_mm(_nt, q_chunk[:, sl], s_mat) + _mm(_nn, all_scores[h], v_h)
                   + jnp.sum(qk[:, sl], axis=-1, keepdims=True) * v_h)
            ms = jnp.mean(o_h * o_h, axis=-1, keepdims=True)
            outs.append(o_h * lax.rsqrt(ms + NORM_EPS))
            upd = _mm(_tn, v_h, k_end[:, sl])
            new_states.append(s_mat * jnp.exp(tot[:, sl]) + upd)
            yield
        o = jnp.concatenate(outs, axis=1) * nw_ref[...]
        return o.astype(o_ref.dtype), new_states

    return one_chunk


def _recurrent_kernel(xb_ref, mu_ref, w0_ref, wla_ref, a0_ref, g2_ref, kk_ref, ka_ref, rk_ref, lnw_ref,
                      lnb_ref, hsum_ref, tril_ref, xd_ref, lb_ref, nw_ref, yb_ref, yd_ref,
                      carry_ref, rstate_ref, hstate_ref, *, layer):
    t_tile = xb_ref.shape[1]
    c = RWKV_CHUNK
    assert HGRN_CHUNK == c
    nchunk = t_tile // c
    seqs = range(xb_ref.shape[0])
    npair = MIX_W // LANES
    halo = carry_ref.shape[1]

    @pl.when(pl.program_id(1) == 0)
    def _():
        carry_ref[...] = jnp.zeros_like(carry_ref)
        rstate_ref[...] = jnp.zeros_like(rstate_ref)
        hstate_ref[...] = jnp.zeros_like(hstate_ref)

    prepare, advance = _rwkv_parts(mu_ref, w0_ref, wla_ref, a0_ref, g2_ref, kk_ref, ka_ref, rk_ref,
                                   lnw_ref, lnb_ref, hsum_ref, tril_ref, yb_ref)
    hgrn_chunk = _hgrn_parts(lb_ref, nw_ref, tril_ref, yd_ref, layer)

    def step(r0, pre, nxt_gens):
        rstates = [[rstate_ref[g, p] for p in range(npair)] for g in seqs]
        hstates = [[hstate_ref[g, h] for h in range(HGRN_HEADS)] for g in seqs]
        xd_in = [xd_ref[g, pl.ds(r0, c), :] for g in seqs]
        res = _round_robin([advance(pre[g], rstates[g]) for g in seqs]
                           + [hgrn_chunk(xd_in[g], hstates[g]) for g in seqs] + nxt_gens)
        ns = len(seqs)
        for g in seqs:
            out, new_states = res[g]
            for p in range(npair):
                rstate_ref[g, p] = new_states[p]
            yb_ref[g, pl.ds(r0, c), :] = out
            out, new_states = res[ns + g]
            for h in range(HGRN_HEADS):
                hstate_ref[g, h] = new_states[h]
            yd_ref[g, pl.ds(r0, c), :] = out
        return res[2 * ns:]

    pre0 = _round_robin([prepare(xb_ref[g, 0:c, :], carry_ref[g]) for g in seqs])

    def chunk(ci, pre):
        r0 = pl.multiple_of(ci * c, c)
        r1 = pl.multiple_of(r0 + c, c)
        nxt = [xb_ref[g, pl.ds(r1, c), :] for g in seqs]
        prv = [xb_ref[g, pl.ds(pl.multiple_of(r1 - halo, halo), halo), :] for g in seqs]
        return step(r0, pre, [prepare(nxt[g], prv[g]) for g in seqs])

    pre_last = lax.fori_loop(0, nchunk - 1, chunk, pre0)
    step((nchunk - 1) * c, pre_last, [])
    for g in seqs:
        carry_ref[g] = xb_ref[g, t_tile - halo:, :].astype(F32)


def _recurrent_mixers(xb, xd, mu, w0, w2, a0, a2, g2, k_k, k_a, r_k, ln_w, ln_b, lb_all, layer,
                      hgrn_norm, t_tile):
    b, s, _ = xb.shape
    c = RWKV_CHUNK
    depth = lb_all.shape[0]
    rank = w2.shape[0]
    wla = jnp.zeros((LANES, 2 * MIX_W), F32)
    wla = wla.at[:rank, :MIX_W].set(w2).at[rank:, MIX_W:].set(a2).astype(BF16)
    gs = MIXER_SEQS if b % MIXER_SEQS == 0 else 1
    hsum = jnp.kron(jnp.eye(LANES // RWKV_HEAD, dtype=F32),
                    jnp.ones((RWKV_HEAD, RWKV_HEAD), F32)).astype(BF16)
    tril = jnp.tril(jnp.ones((c, c), F32)).astype(BF16)
    row = lambda a: a.reshape(1, -1).astype(F32)
    full = lambda shape: pl.BlockSpec(shape, lambda i, j: (0,) * len(shape))
    tile = lambda w: pl.BlockSpec((gs, t_tile, w), lambda i, j: (i, j, 0))
    return pl.pallas_call(
        functools.partial(_recurrent_kernel, layer=layer),
        grid=(b // gs, s // t_tile),
        in_specs=[
            tile(RWKV_IN),
            full((1, RWKV_IN)), full((1, MIX_W)), full((LANES, 2 * MIX_W)), full((1, MIX_W)),
            full((LANES, MIX_W)), full((1, MIX_W)), full((1, MIX_W)), full((1, MIX_W)),
            full((1, MIX_W)), full((1, MIX_W)), full((LANES, LANES)), full((c, c)),
            tile(COL_D), full((depth, MIX_W)), full((1, MIX_W)),
        ],
        out_specs=[tile(MIX_W), tile(MIX_W)],
        out_shape=[jax.ShapeDtypeStruct((b, s, MIX_W), BF16)] * 2,
        scratch_shapes=[pltpu.VMEM((gs, MIXER_HALO, RWKV_IN), F32),
                        pltpu.VMEM((gs, MIX_W // LANES, LANES, LANES), F32),
                        pltpu.VMEM((gs, HGRN_HEADS, HGRN_EXPAND, HGRN_EXPAND), F32)],
        compiler_params=_cparams(("arbitrary", "arbitrary")),
        name="recurrent_mixers",
    )(xb, row(mu), row(w0), wla, row(a0), g2.astype(BF16), row(k_k), row(k_a), row(r_k),
      row(ln_w), row(ln_b), hsum, tril, xd, lb_all.astype(F32), hgrn_norm.reshape(1, MIX_W))


def _merge_kernel(ya_ref, yb_ref, yc_ref, yd_ref, g_ref, x_ref, wbr_ref, wout_ref, o_ref):
    merged = None
    for kk, y_ref in enumerate((ya_ref, yb_ref, yc_ref, yd_ref)):
        gate = _sigmoid(g_ref[:, kk * D_MODEL:(kk + 1) * D_MODEL].astype(F32))
        term = gate * _nn(y_ref[...], wbr_ref[kk])
        merged = term if merged is None else merged + term
    o_ref[...] = x_ref[...] + _nn(merged.astype(BF16), wout_ref[...])


def _merge(ya, yb, yc, yd, g, x, wbr_bf16, wout_bf16, tm):
    n = x.shape[0]
    ytile = pl.BlockSpec((tm, MIX_W), lambda i: (i, 0))
    return pl.pallas_call(
        _merge_kernel,
        grid=(n // tm,),
        in_specs=[
            ytile, ytile, ytile, ytile,
            pl.BlockSpec((tm, 4 * D_MODEL), lambda i: (i, 0)),
            pl.BlockSpec((tm, D_MODEL), lambda i: (i, 0)),
            pl.BlockSpec((4, MIX_W, D_MODEL), lambda i: (0, 0, 0)),
            pl.BlockSpec((D_MODEL, D_MODEL), lambda i: (0, 0)),
        ],
        out_specs=pl.BlockSpec((tm, D_MODEL), lambda i: (i, 0)),
        out_shape=jax.ShapeDtypeStruct((n, D_MODEL), F32),
        compiler_params=_cparams(("arbitrary",)),
        name="merge_out",
    )(ya, yb, yc, yd, g, x, wbr_bf16, wout_bf16)


def _ffn_kernel(x_ref, nf_ref, wup_ref, cw_ref, cb_ref, wd_ref, p_ref, np_ref, pproj_ref, pgate_ref,
                nfin_ref, o_ref, cg_ref, cv_ref, ug_ref, uv_ref, act_ref, acc_ref, *, final_norm):
    tm = x_ref.shape[1]
    blk = FFN_BLOCK
    nb = FFN_NB

    @pl.when(pl.program_id(1) == 0)
    def _():
        cg_ref[...] = jnp.zeros_like(cg_ref)
        cv_ref[...] = jnp.zeros_like(cv_ref)

    x = x_ref[0]
    pp = _nn(p_ref[0, 0].astype(BF16), pproj_ref[...])
    h = _rms(x, nf_ref[...]).astype(BF16)

    def conv(u, carry_ref, j, col0):
        prev = carry_ref[j]
        u1 = _shift_rows(u, prev, 1)
        u2 = _shift_rows(u, prev, 2)
        carry_ref[j] = u[tm - SUBLANES:, :]
        cw = cw_ref[:, col0:col0 + blk]
        return cb_ref[:, col0:col0 + blk] + cw[2:3, :] * u + cw[1:2, :] * u1 + cw[0:1, :] * u2

    def up(j):
        ug_ref[j % 2] = _nn(h, wup_ref[:, j * blk:(j + 1) * blk])
        uv_ref[j % 2] = _nn(h, wup_ref[:, D_FF + j * blk:D_FF + (j + 1) * blk])

    group_ends = [e for e in FFN_GROUP_ENDS if e < nb] + [nb]
    group_start = {end: start for start, end in zip([0] + group_ends[:-1], group_ends)}

    def down(end):
        k0, k1 = group_start[end] * blk, end * blk
        d = _nn(act_ref[:, k0:k1], wd_ref[k0:k1, :])
        if k0 == 0:
            acc_ref[...] = d
        else:
            acc_ref[...] += d

    up(0)
    for j in range(nb):
        if j + 1 < nb:
            up(j + 1)
        if j in group_start:
            down(j)
        ug = conv(ug_ref[j % 2], cg_ref, j, j * blk)
        uv = conv(uv_ref[j % 2], cv_ref, j, D_FF + j * blk)
        act_ref[:, j * blk:(j + 1) * blk] = (_gelu(ug) * uv).astype(BF16)
    down(nb)

    x2 = x + acc_ref[...]
    hp = _rms(x2, np_ref[...]).astype(BF16)
    gate = _sigmoid(_nn(hp, pgate_ref[...]))
    x3 = x2 + pp * gate
    if final_norm:
        x3 = _rms(x3, nfin_ref[...])
    o_ref[0] = x3


def _ffn_ple(x1, norm_ffn, wup_bf16, conv_w, conv_b, wdown_bf16, p_all, layer, norm_ple, pproj_bf16,
             pgate_bf16, norm_final, final_norm, tm):
    b, s, d = x1.shape
    nb, blk = FFN_NB, FFN_BLOCK
    row = lambda a: a.reshape(1, -1)
    const = lambda shape: pl.BlockSpec(shape, lambda i, t: (0,) * len(shape),
                                       pipeline_mode=pl.Buffered(1))
    return pl.pallas_call(
        functools.partial(_ffn_kernel, final_norm=final_norm),
        grid=(b, s // tm),
        in_specs=[
            pl.BlockSpec((1, tm, d), lambda i, t: (i, t, 0)),
            const((1, d)),
            const((d, 2 * D_FF)), const((3, 2 * D_FF)), const((1, 2 * D_FF)), const((D_FF, d)),
            pl.BlockSpec((1, 1, tm, PLE_DIM), lambda i, t: (layer, i, t, 0)),
            const((1, d)),
            const((PLE_DIM, d)),
            const((d, d)),
            const((1, d)),
        ],
        out_specs=pl.BlockSpec((1, tm, d), lambda i, t: (i, t, 0)),
        out_shape=jax.ShapeDtypeStruct((b, s, d), F32),
        scratch_shapes=[pltpu.VMEM((nb, SUBLANES, blk), F32),
                        pltpu.VMEM((nb, SUBLANES, blk), F32),
                        pltpu.VMEM((2, tm, blk), F32),
                        pltpu.VMEM((2, tm, blk), F32),
                        pltpu.VMEM((tm, D_FF), BF16),
                        pltpu.VMEM((tm, d), F32)],
        compiler_params=_cparams(("arbitrary", "arbitrary")),
        name="ffn_ple",
    )(x1, row(norm_ffn), wup_bf16, conv_w, row(conv_b), wdown_bf16, p_all, row(norm_ple), pproj_bf16,
      pgate_bf16, row(norm_final))


def _pick_tile(total, want):
    t = min(total, want)
    while total % t:
        t //= 2
    return t


IN_PROJ_TILE = 512
MIXER_TILE = 256
MERGE_TILE = 512
FFN_TILE = 512


def kernel(x, p, norm_mix, w_in, pool_w, pool_scale, rwkv_mu, rwkv_w0, rwkv_w2, rwkv_a0, rwkv_a2, rwkv_g2, rwkv_kk, rwkv_ka, rwkv_rk, rwkv_ln_w, rwkv_ln_b, sg_ln_w, sg_ln_b, sg_w, sg_b, hgrn_lb, hgrn_norm, w_branch, w_out, norm_ffn, ffn_up, ffn_conv, ffn_conv_b, ffn_down, norm_ple, ple_proj, ple_gate, norm_final):
    b, s, d = x.shape
    depth = w_in.shape[0]
    n = b * s
    for l in range(depth):
        ya, xb, yc, xd, g = _in_proj_mix(x, norm_mix[l], w_in[l].astype(BF16), pool_w[l].astype(BF16),
                                         pool_scale[l], sg_ln_w[l], sg_ln_b[l], sg_w[l], sg_b[l],
                                         _pick_tile(s, IN_PROJ_TILE))
        yb, yd = _recurrent_mixers(xb, xd, rwkv_mu[l], rwkv_w0[l], rwkv_w2[l], rwkv_a0[l], rwkv_a2[l],
                                   rwkv_g2[l], rwkv_kk[l], rwkv_ka[l], rwkv_rk[l].reshape(-1),
                                   rwkv_ln_w[l], rwkv_ln_b[l], hgrn_lb, l, hgrn_norm[l],
                                   _pick_tile(s, MIXER_TILE))
        x1 = _merge(ya.reshape(n, MIX_W), yb.reshape(n, MIX_W), yc.reshape(n, MIX_W),
                    yd.reshape(n, MIX_W), g.reshape(n, 4 * D_MODEL), x.reshape(n, d),
                    w_branch[l].astype(BF16), w_out[l].astype(BF16), _pick_tile(n, MERGE_TILE))
        x = _ffn_ple(x1.reshape(b, s, d), norm_ffn[l], ffn_up[l].astype(BF16), ffn_conv[l],
                     ffn_conv_b[l], ffn_down[l].astype(BF16), p, l, norm_ple[l],
                     ple_proj[l].astype(BF16), ple_gate[l].astype(BF16), norm_final,
                     l == depth - 1, _pick_tile(s, FFN_TILE))
    return x
```

```python
import functools

import jax
import jax.numpy as jnp
from jax import lax
from jax.experimental import pallas as pl
from jax.experimental.pallas import tpu as pltpu

F32 = jnp.float32
BF16 = jnp.bfloat16

D_MODEL = 1024
MIX_W = 512
PLE_DIM = 256
POOL_WINDOWS = (2, 4, 8, 16)
POOL_GROUP = 128
POOL_HALO = 16
RWKV_HEAD = 64
RWKV_IN = 1792
RWKV_LN_EPS = 64e-5
SG_CHUNK = 128
SG_GROUPS = 4
SG_LN_EPS = 1e-5
HGRN_HEADS = 4
HGRN_EXPAND = 128
GATE_FLOOR = 1e-30
D_FF = 2816
NORM_EPS = 1e-6
COL_A, COL_B, COL_C, COL_D = 512, 1792, 1024, 1536

LANES = 128
SUBLANES = 8
VMEM_LIMIT_BYTES = 56 * 1024 * 1024

RWKV_CHUNK = 64
HGRN_CHUNK = 64

FFN_BLOCK = 256
FFN_NB = D_FF // FFN_BLOCK
FFN_GROUP_ENDS = (5, 9, 10)


def _cparams(sem):
    return pltpu.CompilerParams(dimension_semantics=sem, vmem_limit_bytes=VMEM_LIMIT_BYTES)


def _nn(a, b):
    return lax.dot_general(a, b, (((1,), (0,)), ((), ())), preferred_element_type=F32)


def _nt(a, b):
    return lax.dot_general(a, b, (((1,), (1,)), ((), ())), preferred_element_type=F32)


def _tn(a, b):
    return lax.dot_general(a, b, (((0,), (0,)), ((), ())), preferred_element_type=F32)


def _split(a, terms):
    out = []
    rem = a
    for _ in range(terms - 1):
        hi = rem.astype(BF16)
        out.append(hi)
        rem = rem - hi.astype(F32)
    out.append(rem.astype(BF16))
    return out


def _mm(dot, a, b):
    return dot(a.astype(BF16), b.astype(BF16))


def _mm_exact_rhs(a, b_bf16, terms):
    acc = None
    for part in _split(a, terms):
        t = _nn(part, b_bf16)
        acc = t if acc is None else acc + t
    return acc


def _head_sums(arrays, ones_pair_bf16, terms=2):
    c = arrays[0].shape[0]
    slabs = [a[:, p * LANES:(p + 1) * LANES] for a in arrays for p in range(MIX_W // LANES)]
    s = _mm_exact_rhs(jnp.concatenate(slabs, axis=0), ones_pair_bf16, terms)
    per = MIX_W // LANES
    return [jnp.concatenate([s[(i * per + p) * c:(i * per + p + 1) * c, :] for p in range(per)], axis=1)
            for i in range(len(arrays))]


def _rms(x, gain):
    return x * lax.rsqrt(jnp.mean(x * x, axis=-1, keepdims=True) + NORM_EPS) * gain


def _sigmoid(x):
    return 1.0 / (1.0 + jnp.exp(-x))


def _gelu(x):
    return 0.5 * x * (1.0 + jnp.tanh(0.7978845608028654 * (x + 0.044715 * (x * x * x))))


def _iota(shape, dim):
    return lax.broadcasted_iota(jnp.int32, shape, dim)


def _round_robin(gens):
    results = [None] * len(gens)
    live = list(range(len(gens)))
    while live:
        for idx in list(live):
            try:
                next(gens[idx])
            except StopIteration as stop:
                results[idx] = stop.value
                live.remove(idx)
    return results


def _shift_rows(x, prev_rows, n):
    rolled = pltpu.roll(x, n, axis=0)
    row = _iota(x.shape, 0)
    out = rolled
    for j in range(n):
        src = prev_rows.shape[0] - n + j
        out = jnp.where(row == j, prev_rows[src:src + 1, :], out)
    return out


IN_PROJ_CHUNK = 512


def _pool_mix(a, carry_ref, first_pos, w_ref, scale_ref):
    t_tile = a.shape[0]
    ext = jnp.concatenate([carry_ref[...], a], axis=0)
    carry_ref[...] = a[t_tile - POOL_HALO:, :]
    count = (first_pos + _iota((t_tile, 1), 0) + 1).astype(F32)
    outs = []
    for gi, win in enumerate(POOL_WINDOWS):
        col = slice(gi * POOL_GROUP, (gi + 1) * POOL_GROUP)
        s = ext[:, col]
        span = 1
        while span < win:
            s = s + pltpu.roll(s, span, axis=0)
            span *= 2
        mean = s[POOL_HALO:, :] / jnp.minimum(count, float(win))
        diff = (mean - a[:, col]).astype(BF16)
        outs.append(_nn(diff, w_ref[gi]))
    return jnp.concatenate(outs, axis=1) * scale_ref[...]


def _sg_mix(xc, lnw_ref, lnb_ref, ws_ref, bs_ref, o_ref):
    t_tile = xc.shape[0]
    z = _gelu(xc)
    u = z[:, :MIX_W]
    v = z[:, MIX_W:]
    mean = jnp.mean(v, axis=-1, keepdims=True)
    vc = v - mean
    var = jnp.mean(vc * vc, axis=-1, keepdims=True)
    vn = (vc * lax.rsqrt(var + SG_LN_EPS) * lnw_ref[...] + lnb_ref[...]).astype(BF16)
    causal = _iota((SG_CHUNK, SG_CHUNK), 0) >= _iota((SG_CHUNK, SG_CHUNK), 1)
    w_m = [jnp.where(causal, ws_ref[g], 0.0).astype(BF16) for g in range(SG_GROUPS)]
    for n in range(t_tile // SG_CHUNK):
        rows = slice(n * SG_CHUNK, (n + 1) * SG_CHUNK)
        mixed = jnp.concatenate(
            [_nn(w_m[g], vn[rows, g * LANES:(g + 1) * LANES]) for g in range(SG_GROUPS)], axis=1)
        o_ref[0, rows, :] = (u[rows, :] * (mixed + bs_ref[...])).astype(o_ref.dtype)


def _in_proj_kernel(x_ref, g_ref, w_ref, pw_ref, pscale_ref, lnw_ref, lnb_ref, ws_ref, bs_ref,
                    ya_ref, xb_ref, yc_ref, xd_ref, gate_ref, carry_ref):
    tm = x_ref.shape[1]

    @pl.when(pl.program_id(1) == 0)
    def _():
        carry_ref[...] = jnp.zeros_like(carry_ref)

    h = _rms(x_ref[0], g_ref[...]).astype(BF16)

    def proj(c0, c1):
        parts = [_nn(h, w_ref[:, k0:min(k0 + IN_PROJ_CHUNK, c1)]) for k0 in range(c0, c1, IN_PROJ_CHUNK)]
        return parts[0] if len(parts) == 1 else jnp.concatenate(parts, axis=1)

    o0, o1, o2, o3 = COL_A, COL_A + COL_B, COL_A + COL_B + COL_C, COL_A + COL_B + COL_C + COL_D
    def emit(out_ref, c0, c1):
        for k0 in range(c0, c1, IN_PROJ_CHUNK):
            k1 = min(k0 + IN_PROJ_CHUNK, c1)
            out_ref[0, :, k0 - c0:k1 - c0] = _nn(h, w_ref[:, k0:k1]).astype(out_ref.dtype)

    xa = proj(0, o0)
    xc = proj(o1, o2)
    emit(xb_ref, o0, o1)
    emit(xd_ref, o2, o3)
    emit(gate_ref, o3, w_ref.shape[1])
    ya = _pool_mix(xa, carry_ref, pl.program_id(1) * tm, pw_ref, pscale_ref)
    ya_ref[0] = ya.astype(ya_ref.dtype)
    _sg_mix(xc, lnw_ref, lnb_ref, ws_ref, bs_ref, yc_ref)


def _in_proj_mix(x, gain, w_bf16, pool_w_bf16, pool_scale, sg_ln_w, sg_ln_b, sg_w, sg_b, tm):
    b, s, d = x.shape
    m = w_bf16.shape[1]
    b_full = jnp.repeat(jnp.swapaxes(sg_b, 0, 1), MIX_W // SG_GROUPS, axis=1)
    const = lambda shape: pl.BlockSpec(shape, lambda i, t: (0,) * len(shape),
                                       pipeline_mode=pl.Buffered(1))
    tile = lambda w: pl.BlockSpec((1, tm, w), lambda i, t: (i, t, 0))
    widths = (MIX_W, COL_B, MIX_W, COL_D, m - (COL_A + COL_B + COL_C + COL_D))
    dtypes = (BF16,) * len(widths)
    return pl.pallas_call(
        _in_proj_kernel,
        grid=(b, s // tm),
        in_specs=[
            tile(d), const((1, d)), const((d, m)),
            const((len(POOL_WINDOWS), POOL_GROUP, POOL_GROUP)), const((1, MIX_W)),
            const((1, MIX_W)), const((1, MIX_W)), const((SG_GROUPS, SG_CHUNK, SG_CHUNK)),
            const((SG_CHUNK, MIX_W)),
        ],
        out_specs=[tile(w) for w in widths],
        out_shape=[jax.ShapeDtypeStruct((b, s, w), dt) for w, dt in zip(widths, dtypes)],
        scratch_shapes=[pltpu.VMEM((POOL_HALO, MIX_W), F32)],
        compiler_params=_cparams(("arbitrary", "arbitrary")),
        name="in_proj_mix",
    )(x, gain.reshape(1, d), w_bf16, pool_w_bf16, pool_scale.reshape(1, MIX_W),
      sg_ln_w.reshape(1, MIX_W), sg_ln_b.reshape(1, MIX_W), sg_w, b_full)


RWKV_DECAY_SCALE = 0.6065306597126334
RWKV_NORM_FLOOR = 1e-12
RWKV_CUM_TERMS = 2
RWKV_SUM_TERMS = 1
MIXER_SEQS = 4
MIXER_HALO = 16


def _rwkv_parts(mu_ref, w0_ref, wla_ref, a0_ref, g2_ref, kk_ref, ka_ref, rk_ref,
                lnw_ref, lnb_ref, hsum_ref, tril_ref, o_ref):
    c = RWKV_CHUNK
    hd = RWKV_HEAD

    lane = _iota((1, LANES), 1)
    head0 = lane < hd
    col_s = jnp.bitwise_and(_iota((c, LANES), 1), hd - 1)
    row_t = _iota((c, LANES), 0)
    strict = row_t > col_s
    incl = row_t >= col_s
    bd_mask = (_iota((LANES, LANES), 0) < hd) == (_iota((LANES, LANES), 1) < hd)
    sys_w = 2 * LANES
    lane_head = _iota((c, sys_w), 1) // hd

    eye_sys = (_iota((c, sys_w), 0) == jnp.bitwise_and(_iota((c, sys_w), 1), hd - 1)).astype(F32)
    npair = MIX_W // LANES
    nsys = MIX_W // sys_w
    nsteps = c.bit_length() - 1

    def block_diag(m):
        zero = jnp.zeros((), m.dtype)
        return jnp.concatenate([jnp.where(lane_head == j, m, zero) for j in range(sys_w // hd)], axis=0)

    def pair_stack(m):
        zero = jnp.zeros((), m.dtype)
        return jnp.concatenate([jnp.where(head0, m, zero), jnp.where(head0, zero, m)], axis=0)

    def prepare(x, prev_rows):
        x = x.astype(F32)
        xprev = _shift_rows(x, prev_rows.astype(F32), 1)
        xs = x + mu_ref[...] * (xprev - x)
        r = xs[:, 0:MIX_W]
        k = xs[:, MIX_W:2 * MIX_W]
        v = xs[:, 2 * MIX_W:3 * MIX_W]
        slab = xs[:, 3 * MIX_W:3 * MIX_W + LANES]
        lg = xs[:, 3 * MIX_W + LANES:]
        slab = jnp.where(head0, jnp.tanh(slab), slab)
        low = _nn(slab.astype(BF16), wla_ref[...])
        gate = _nn(_sigmoid(lg).astype(BF16), g2_ref[...])
        yield
        ld = -RWKV_DECAY_SCALE * _sigmoid(w0_ref[...] + low[:, :MIX_W])
        cum = _mm_exact_rhs_left(tril_ref[...], ld, RWKV_CUM_TERMS)
        eta = _sigmoid(a0_ref[...] + low[:, MIX_W:])
        kkv = k * kk_ref[...]
        k2 = k * (1.0 + (eta - 1.0) * ka_ref[...])
        nrm2, rk_sum = _head_sums([kkv * kkv, r * k2 * rk_ref[...]], hsum_ref[...], RWKV_SUM_TERMS)
        yield
        kkn = kkv * lax.rsqrt(jnp.maximum(nrm2, RWKV_NORM_FLOOR ** 2))
        tot = cum[c - 1:c, :]
        e_neg = jnp.exp(-cum)
        rt = r * jnp.exp(cum)
        at = -kkn * jnp.exp(cum - ld)
        b_vec = kkn * eta
        bt = b_vec * e_neg
        kt = k2 * e_neg
        e_end = jnp.exp(tot - cum)
        b_end = b_vec * e_end
        k_end = k2 * e_end
        p_tot = jnp.exp(tot)

        at_b, rt_b = at.astype(BF16), rt.astype(BF16)
        yield
        a_ab, a_ak, a_rbk, v_bd = [], [], [], []
        for p in range(npair):
            sl = slice(p * LANES, (p + 1) * LANES)
            lhs = jnp.concatenate([at_b[:, sl], rt_b[:, sl]], axis=0)
            rhs = jnp.concatenate([pair_stack(bt[:, sl].astype(BF16)),
                                   pair_stack(kt[:, sl].astype(BF16))], axis=0)
            sc = _nt(lhs, rhs)
            a_ab.append(jnp.where(strict, sc[0:c, 0:LANES], 0.0))
            a_ak.append(jnp.where(strict, sc[0:c, LANES:], 0.0).astype(BF16))
            a_rbk.append(jnp.concatenate([jnp.where(incl, sc[c:, 0:LANES], 0.0),
                                          jnp.where(incl, sc[c:, LANES:], 0.0)], axis=1).astype(BF16))
            v_bd.append(pair_stack(v[:, sl].astype(BF16)))
            yield
        pws = [jnp.concatenate([a_ab[2 * q], a_ab[2 * q + 1]], axis=1).astype(BF16) for q in range(nsys)]
        tis = [pws[q].astype(F32) + eye_sys for q in range(nsys)]
        pws = [_nn(pws[q], block_diag(pws[q])).astype(BF16) for q in range(nsys)]
        yield
        for i in range(1, nsteps):
            for q in range(nsys):
                wts = block_diag(pws[q])
                if i + 1 < nsteps:
                    both = _nn(jnp.concatenate([pws[q], tis[q].astype(BF16)], axis=0), wts)
                    pws[q] = both[0:c].astype(BF16)
                    tis[q] = tis[q] + both[c:]
                else:
                    tis[q] = tis[q] + _nn(tis[q].astype(BF16), wts)
            yield
        tinv = [t.astype(BF16) for t in tis]
        return dict(at=at_b, rt=rt_b, a_ak=a_ak, a_rbk=a_rbk, v_bd=v_bd, v=v.astype(BF16), tinv=tinv,
                    b_end=b_end.astype(BF16), k_end=k_end.astype(BF16), p_tot=p_tot, gate=gate,
                    bonus=rk_sum * v)

    def advance(pre, s_mats):
        ws = []
        for p in range(npair):
            sl = slice(p * LANES, (p + 1) * LANES)
            s_b = s_mats[p].astype(BF16)
            ws.append(_nt(pre["at"][:, sl], s_b) + _nn(pre["a_ak"][p], pre["v_bd"][p]))
            yield
        us = []
        for q in range(nsys):
            w_sys = jnp.concatenate([ws[2 * q], ws[2 * q + 1]], axis=1).astype(BF16)
            us.append(_nn(pre["tinv"][q], block_diag(w_sys)))
            yield
        ys, new_states = [], []
        for p in range(npair):
            sl = slice(p * LANES, (p + 1) * LANES)
            u_p = us[p // 2][:, (p % 2) * LANES:(p % 2 + 1) * LANES].astype(BF16)
            s_mat = s_mats[p]
            y_p = (_nt(pre["rt"][:, sl], s_mat.astype(BF16))
                   + _nn(pre["a_rbk"][p], jnp.concatenate([pair_stack(u_p), pre["v_bd"][p]], axis=0)))
            ys.append(y_p)
            upd = _tn(jnp.concatenate([u_p, pre["v"][:, sl]], axis=0),
                      jnp.concatenate([pre["b_end"][:, sl], pre["k_end"][:, sl]], axis=0))
            new_states.append(jnp.where(bd_mask, s_mat * pre["p_tot"][:, sl] + upd, 0.0))
            yield
        y = jnp.concatenate(ys, axis=1)
        inv_hd = 1.0 / hd
        mean = _head_sums([y], hsum_ref[...], RWKV_SUM_TERMS)[0] * inv_hd
        yield
        yc = y - mean
        var = _head_sums([yc * yc], hsum_ref[...], RWKV_SUM_TERMS)[0] * inv_hd
        yn = yc * lax.rsqrt(var + RWKV_LN_EPS) * lnw_ref[...] + lnb_ref[...]
        return ((yn + pre["bonus"]) * pre["gate"]).astype(o_ref.dtype), new_states

    return prepare, advance


def _mm_exact_rhs_left(l_bf16, a, terms=3):
    acc = None
    for part in _split(a, terms):
        t = _nn(l_bf16, part)
        acc = t if acc is None else acc + t
    return acc


def _hgrn_levels(c):
    return [c >> (i + 1) for i in range(c.bit_length() - 1)]


def _block_mid_rows(a, half):
    rows, width = a.shape
    blk = 2 * half
    sub_row = _iota((SUBLANES, 1), 0)
    tiles = []
    for j in range(rows // SUBLANES):
        base = j * SUBLANES
        if blk >= SUBLANES:
            src = (base // blk) * blk + half - 1
            tiles.append(jnp.broadcast_to(a[src:src + 1, :], (SUBLANES, width)))
        else:
            tile = None
            for i in reversed(range(SUBLANES // blk)):
                src = base + i * blk + half - 1
                piece = jnp.broadcast_to(a[src:src + 1, :], (SUBLANES, width))
                tile = piece if tile is None else jnp.where(sub_row < (i + 1) * blk, piece, tile)
            tiles.append(tile)
    return jnp.concatenate(tiles, axis=0)


def _hgrn_parts(lb_ref, nw_ref, cmat_ref, o_ref, layer):
    c = HGRN_CHUNK
    levels = _hgrn_levels(c)

    lb_all = lb_ref[...]
    e_all = jnp.exp(lb_all - jnp.max(lb_all, axis=0, keepdims=True))
    probs = e_all / jnp.sum(e_all, axis=0, keepdims=True)
    lb = jnp.sum(probs[0:layer + 1, :], axis=0, keepdims=True) - probs[0:1, :]
    row_c = _iota((c, 1), 0)
    row_cc = _iota((c, c), 0)
    col_cc = _iota((c, c), 1)

    def one_chunk(x, s_mats):
        x = x.astype(F32)
        qr = x[:, 0:MIX_W]
        q = qr * _sigmoid(qr)
        sig = _sigmoid(x[:, MIX_W:2 * MIX_W])
        v = x[:, 2 * MIX_W:]
        lg = jnp.log(jnp.maximum(lb + (1.0 - lb) * sig, GATE_FLOOR))
        key = (1.0 - lb) * (1.0 - sig)
        cum = _mm_exact_rhs_left(cmat_ref[...], lg)
        tot = cum[c - 1:c, :]
        q_chunk = q * jnp.exp(cum)
        k_end = key * jnp.exp(tot - cum)
        yield
        q_lv, k_lv = [], []
        zero_tile = jnp.zeros((SUBLANES, MIX_W), F32)
        for li, half in enumerate(levels):
            if half >= SUBLANES:
                q_tiles, k_tiles = [], []
                for base in range(0, c, SUBLANES):
                    rows = slice(base, base + SUBLANES)
                    mid_row = (base // (2 * half)) * (2 * half) + half - 1
                    mid = cum[mid_row:mid_row + 1, :]
                    if base & half:
                        q_tiles.append(q[rows] * jnp.exp(cum[rows] - mid))
                        k_tiles.append(zero_tile)
                    else:
                        q_tiles.append(zero_tile)
                        k_tiles.append(key[rows] * jnp.exp(mid - cum[rows]))
                q_lv.append(jnp.concatenate(q_tiles, axis=0))
                k_lv.append(jnp.concatenate(k_tiles, axis=0))
            else:
                dec = jnp.exp(-jnp.abs(cum - _block_mid_rows(cum, half)))
                second = jnp.bitwise_and(row_c, half) != 0
                q_lv.append(jnp.where(second, q * dec, 0.0))
                k_lv.append(jnp.where(second, 0.0, key * dec))
        qk = q * key
        outs, new_states, all_scores = [], [], []
        for h in range(HGRN_HEADS):
            sl = slice(h * LANES, (h + 1) * LANES)
            scores = None
            for li, half in enumerate(levels):
                sc = _mm(_nt, q_lv[li][:, sl], k_lv[li][:, sl])
                if 2 * half < c:
                    blk = 2 * half
                    sc = jnp.where((row_cc // blk) == (col_cc // blk), sc, 0.0)
                scores = sc if scores is None else scores + sc
            all_scores.append(scores)
            yield
        for h in range(HGRN_HEADS):
            sl = slice(h * LANES, (h + 1) * LANES)
            s_mat = s_mats[h]
            v_h = v[:, sl]
            o_h = (_mm(_nt, q_chunk[:, sl], s_mat) + _mm(_nn, all_scores[h], v_h)
                   + jnp.sum(qk[:, sl], axis=-1, keepdims=True) * v_h)
            ms = jnp.mean(o_h * o_h, axis=-1, keepdims=True)
            outs.append(o_h * lax.rsqrt(ms + NORM_EPS))
            upd = _mm(_tn, v_h, k_end[:, sl])
            new_states.append(s_mat * jnp.exp(tot[:, sl]) + upd)
            yield
        o = jnp.concatenate(outs, axis=1) * nw_ref[...]
        return o.astype(o_ref.dtype), new_states

    return one_chunk


def _recurrent_kernel(xb_ref, mu_ref, w0_ref, wla_ref, a0_ref, g2_ref, kk_ref, ka_ref, rk_ref, lnw_ref,
                      lnb_ref, hsum_ref, tril_ref, xd_ref, lb_ref, nw_ref, yb_ref, yd_ref,
                      carry_ref, rstate_ref, hstate_ref, *, layer):
    t_tile = xb_ref.shape[1]
    c = RWKV_CHUNK
    assert HGRN_CHUNK == c
    nchunk = t_tile // c
    seqs = range(xb_ref.shape[0])
    npair = MIX_W // LANES
    halo = carry_ref.shape[1]

    @pl.when(pl.program_id(1) == 0)
    def _():
        carry_ref[...] = jnp.zeros_like(carry_ref)
        rstate_ref[...] = jnp.zeros_like(rstate_ref)
        hstate_ref[...] = jnp.zeros_like(hstate_ref)

    prepare, advance = _rwkv_parts(mu_ref, w0_ref, wla_ref, a0_ref, g2_ref, kk_ref, ka_ref, rk_ref,
                                   lnw_ref, lnb_ref, hsum_ref, tril_ref, yb_ref)
    hgrn_chunk = _hgrn_parts(lb_ref, nw_ref, tril_ref, yd_ref, layer)

    def step(r0, pre, nxt_gens):
        rstates = [[rstate_ref[g, p] for p in range(npair)] for g in seqs]
        hstates = [[hstate_ref[g, h] for h in range(HGRN_HEADS)] for g in seqs]
        xd_in = [xd_ref[g, pl.ds(r0, c), :] for g in seqs]
        res = _round_robin([advance(pre[g], rstates[g]) for g in seqs]
                           + [hgrn_chunk(xd_in[g], hstates[g]) for g in seqs] + nxt_gens)
        ns = len(seqs)
        for g in seqs:
            out, new_states = res[g]
            for p in range(npair):
                rstate_ref[g, p] = new_states[p]
            yb_ref[g, pl.ds(r0, c), :] = out
            out, new_states = res[ns + g]
            for h in range(HGRN_HEADS):
                hstate_ref[g, h] = new_states[h]
            yd_ref[g, pl.ds(r0, c), :] = out
        return res[2 * ns:]

    pre0 = _round_robin([prepare(xb_ref[g, 0:c, :], carry_ref[g]) for g in seqs])

    def chunk(ci, pre):
        r0 = pl.multiple_of(ci * c, c)
        r1 = pl.multiple_of(r0 + c, c)
        nxt = [xb_ref[g, pl.ds(r1, c), :] for g in seqs]
        prv = [xb_ref[g, pl.ds(pl.multiple_of(r1 - halo, halo), halo), :] for g in seqs]
        return step(r0, pre, [prepare(nxt[g], prv[g]) for g in seqs])

    pre_last = lax.fori_loop(0, nchunk - 1, chunk, pre0)
    step((nchunk - 1) * c, pre_last, [])
    for g in seqs:
        carry_ref[g] = xb_ref[g, t_tile - halo:, :].astype(F32)


def _recurrent_mixers(xb, xd, mu, w0, w2, a0, a2, g2, k_k, k_a, r_k, ln_w, ln_b, lb_all, layer,
                      hgrn_norm, t_tile):
    b, s, _ = xb.shape
    c = RWKV_CHUNK
    depth = lb_all.shape[0]
    rank = w2.shape[0]
    wla = jnp.zeros((LANES, 2 * MIX_W), F32)
    wla = wla.at[:rank, :MIX_W].set(w2).at[rank:, MIX_W:].set(a2).astype(BF16)
    gs = MIXER_SEQS if b % MIXER_SEQS == 0 else 1
    hsum = jnp.kron(jnp.eye(LANES // RWKV_HEAD, dtype=F32),
                    jnp.ones((RWKV_HEAD, RWKV_HEAD), F32)).astype(BF16)
    tril = jnp.tril(jnp.ones((c, c), F32)).astype(BF16)
    row = lambda a: a.reshape(1, -1).astype(F32)
    full = lambda shape: pl.BlockSpec(shape, lambda i, j: (0,) * len(shape))
    tile = lambda w: pl.BlockSpec((gs, t_tile, w), lambda i, j: (i, j, 0))
    return pl.pallas_call(
        functools.partial(_recurrent_kernel, layer=layer),
        grid=(b // gs, s // t_tile),
        in_specs=[
            tile(RWKV_IN),
            full((1, RWKV_IN)), full((1, MIX_W)), full((LANES, 2 * MIX_W)), full((1, MIX_W)),
            full((LANES, MIX_W)), full((1, MIX_W)), full((1, MIX_W)), full((1, MIX_W)),
            full((1, MIX_W)), full((1, MIX_W)), full((LANES, LANES)), full((c, c)),
            tile(COL_D), full((depth, MIX_W)), full((1, MIX_W)),
        ],
        out_specs=[tile(MIX_W), tile(MIX_W)],
        out_shape=[jax.ShapeDtypeStruct((b, s, MIX_W), BF16)] * 2,
        scratch_shapes=[pltpu.VMEM((gs, MIXER_HALO, RWKV_IN), F32),
                        pltpu.VMEM((gs, MIX_W // LANES, LANES, LANES), F32),
                        pltpu.VMEM((gs, HGRN_HEADS, HGRN_EXPAND, HGRN_EXPAND), F32)],
        compiler_params=_cparams(("arbitrary", "arbitrary")),
        name="recurrent_mixers",
    )(xb, row(mu), row(w0), wla, row(a0), g2.astype(BF16), row(k_k), row(k_a), row(r_k),
      row(ln_w), row(ln_b), hsum, tril, xd, lb_all.astype(F32), hgrn_norm.reshape(1, MIX_W))


def _merge_kernel(ya_ref, yb_ref, yc_ref, yd_ref, g_ref, x_ref, wbr_ref, wout_ref, o_ref):
    merged = None
    for kk, y_ref in enumerate((ya_ref, yb_ref, yc_ref, yd_ref)):
        gate = _sigmoid(g_ref[:, kk * D_MODEL:(kk + 1) * D_MODEL].astype(F32))
        term = gate * _nn(y_ref[...], wbr_ref[kk])
        merged = term if merged is None else merged + term
    o_ref[...] = x_ref[...] + _nn(merged.astype(BF16), wout_ref[...])


def _merge(ya, yb, yc, yd, g, x, wbr_bf16, wout_bf16, tm):
    n = x.shape[0]
    ytile = pl.BlockSpec((tm, MIX_W), lambda i: (i, 0))
    return pl.pallas_call(
        _merge_kernel,
        grid=(n // tm,),
        in_specs=[
            ytile, ytile, ytile, ytile,
            pl.BlockSpec((tm, 4 * D_MODEL), lambda i: (i, 0)),
            pl.BlockSpec((tm, D_MODEL), lambda i: (i, 0)),
            pl.BlockSpec((4, MIX_W, D_MODEL), lambda i: (0, 0, 0)),
            pl.BlockSpec((D_MODEL, D_MODEL), lambda i: (0, 0)),
        ],
        out_specs=pl.BlockSpec((tm, D_MODEL), lambda i: (i, 0)),
        out_shape=jax.ShapeDtypeStruct((n, D_MODEL), F32),
        compiler_params=_cparams(("arbitrary",)),
        name="merge_out",
    )(ya, yb, yc, yd, g, x, wbr_bf16, wout_bf16)


def _ffn_kernel(x_ref, nf_ref, wup_ref, cw_ref, cb_ref, wd_ref, p_ref, np_ref, pproj_ref, pgate_ref,
                nfin_ref, o_ref, cg_ref, cv_ref, ug_ref, uv_ref, act_ref, acc_ref, *, final_norm):
    tm = x_ref.shape[1]
    blk = FFN_BLOCK
    nb = FFN_NB

    @pl.when(pl.program_id(1) == 0)
    def _():
        cg_ref[...] = jnp.zeros_like(cg_ref)
        cv_ref[...] = jnp.zeros_like(cv_ref)

    x = x_ref[0]
    pp = _nn(p_ref[0, 0].astype(BF16), pproj_ref[...])
    h = _rms(x, nf_ref[...]).astype(BF16)

    def conv(u, carry_ref, j, col0):
        prev = carry_ref[j]
        u1 = _shift_rows(u, prev, 1)
        u2 = _shift_rows(u, prev, 2)
        carry_ref[j] = u[tm - SUBLANES:, :]
        cw = cw_ref[:, col0:col0 + blk]
        return cb_ref[:, col0:col0 + blk] + cw[2:3, :] * u + cw[1:2, :] * u1 + cw[0:1, :] * u2

    def up(j):
        ug_ref[j % 2] = _nn(h, wup_ref[:, j * blk:(j + 1) * blk])
        uv_ref[j % 2] = _nn(h, wup_ref[:, D_FF + j * blk:D_FF + (j + 1) * blk])

    group_ends = [e for e in FFN_GROUP_ENDS if e < nb] + [nb]
    group_start = {end: start for start, end in zip([0] + group_ends[:-1], group_ends)}

    def down(end):
        k0, k1 = group_start[end] * blk, end * blk
        d = _nn(act_ref[:, k0:k1], wd_ref[k0:k1, :])
        if k0 == 0:
            acc_ref[...] = d
        else:
            acc_ref[...] += d

    up(0)
    for j in range(nb):
        if j + 1 < nb:
            up(j + 1)
        if j in group_start:
            down(j)
        ug = conv(ug_ref[j % 2], cg_ref, j, j * blk)
        uv = conv(uv_ref[j % 2], cv_ref, j, D_FF + j * blk)
        act_ref[:, j * blk:(j + 1) * blk] = (_gelu(ug) * uv).astype(BF16)
    down(nb)

    x2 = x + acc_ref[...]
    hp = _rms(x2, np_ref[...]).astype(BF16)
    gate = _sigmoid(_nn(hp, pgate_ref[...]))
    x3 = x2 + pp * gate
    if final_norm:
        x3 = _rms(x3, nfin_ref[...])
    o_ref[0] = x3


def _ffn_ple(x1, norm_ffn, wup_bf16, conv_w, conv_b, wdown_bf16, p_all, layer, norm_ple, pproj_bf16,
             pgate_bf16, norm_final, final_norm, tm):
    b, s, d = x1.shape
    nb, blk = FFN_NB, FFN_BLOCK
    row = lambda a: a.reshape(1, -1)
    const = lambda shape: pl.BlockSpec(shape, lambda i, t: (0,) * len(shape),
                                       pipeline_mode=pl.Buffered(1))
    return pl.pallas_call(
        functools.partial(_ffn_kernel, final_norm=final_norm),
        grid=(b, s // tm),
        in_specs=[
            pl.BlockSpec((1, tm, d), lambda i, t: (i, t, 0)),
            const((1, d)),
            const((d, 2 * D_FF)), const((3, 2 * D_FF)), const((1, 2 * D_FF)), const((D_FF, d)),
            pl.BlockSpec((1, 1, tm, PLE_DIM), lambda i, t: (layer, i, t, 0)),
            const((1, d)),
            const((PLE_DIM, d)),
            const((d, d)),
            const((1, d)),
        ],
        out_specs=pl.BlockSpec((1, tm, d), lambda i, t: (i, t, 0)),
        out_shape=jax.ShapeDtypeStruct((b, s, d), F32),
        scratch_shapes=[pltpu.VMEM((nb, SUBLANES, blk), F32),
                        pltpu.VMEM((nb, SUBLANES, blk), F32),
                        pltpu.VMEM((2, tm, blk), F32),
                        pltpu.VMEM((2, tm, blk), F32),
                        pltpu.VMEM((tm, D_FF), BF16),
                        pltpu.VMEM((tm, d), F32)],
        compiler_params=_cparams(("arbitrary", "arbitrary")),
        name="ffn_ple",
    )(x1, row(norm_ffn), wup_bf16, conv_w, row(conv_b), wdown_bf16, p_all, row(norm_ple), pproj_bf16,
      pgate_bf16, row(norm_final))


def _pick_tile(total, want):
    t = min(total, want)
    while total % t:
        t //= 2
    return t


IN_PROJ_TILE = 512
MIXER_TILE = 512
MERGE_TILE = 512
FFN_TILE = 512


def kernel(x, p, norm_mix, w_in, pool_w, pool_scale, rwkv_mu, rwkv_w0, rwkv_w2, rwkv_a0, rwkv_a2, rwkv_g2, rwkv_kk, rwkv_ka, rwkv_rk, rwkv_ln_w, rwkv_ln_b, sg_ln_w, sg_ln_b, sg_w, sg_b, hgrn_lb, hgrn_norm, w_branch, w_out, norm_ffn, ffn_up, ffn_conv, ffn_conv_b, ffn_down, norm_ple, ple_proj, ple_gate, norm_final):
    b, s, d = x.shape
    depth = w_in.shape[0]
    n = b * s
    for l in range(depth):
        ya, xb, yc, xd, g = _in_proj_mix(x, norm_mix[l], w_in[l].astype(BF16), pool_w[l].astype(BF16),
                                         pool_scale[l], sg_ln_w[l], sg_ln_b[l], sg_w[l], sg_b[l],
                                         _pick_tile(s, IN_PROJ_TILE))
        yb, yd = _recurrent_mixers(xb, xd, rwkv_mu[l], rwkv_w0[l], rwkv_w2[l], rwkv_a0[l], rwkv_a2[l],
                                   rwkv_g2[l], rwkv_kk[l], rwkv_ka[l], rwkv_rk[l].reshape(-1),
                                   rwkv_ln_w[l], rwkv_ln_b[l], hgrn_lb, l, hgrn_norm[l],
                                   _pick_tile(s, MIXER_TILE))
        x1 = _merge(ya.reshape(n, MIX_W), yb.reshape(n, MIX_W), yc.reshape(n, MIX_W),
                    yd.reshape(n, MIX_W), g.reshape(n, 4 * D_MODEL), x.reshape(n, d),
                    w_branch[l].astype(BF16), w_out[l].astype(BF16), _pick_tile(n, MERGE_TILE))
        x = _ffn_ple(x1.reshape(b, s, d), norm_ffn[l], ffn_up[l].astype(BF16), ffn_conv[l],
                     ffn_conv_b[l], ffn_down[l].astype(BF16), p, l, norm_ple[l],
                     ple_proj[l].astype(BF16), ple_gate[l].astype(BF16), norm_final,
                     l == depth - 1, _pick_tile(s, FFN_TILE))
    return x
```
